```python
import math
import jax, jax.numpy as jnp
from jax import lax
import numpy as np

D_MODEL = 4096
BATCH = 32
SEQ = 256
DEPTH = 2
DEC_BATCH = 2
DEC_SEQ = 2048
PAST_LEN = 512

GRID_W = 64
HEAD_DIM = 128
N_HEADS = 16
N_KV_HEADS = 4
KV_GROUP = N_HEADS // N_KV_HEADS
ATTN_W = N_HEADS * HEAD_DIM
KV_W = N_KV_HEADS * HEAD_DIM
Q_BLOCK = 128
ATTN_SCALE = HEAD_DIM ** -0.5
ROPE_THETA = 10000.0
ROPE_PAIRS_AXIS = HEAD_DIM // 4
FNET_W = D_MODEL // 4
FNET_GROUPS = 4
FNET_GROUP_W = FNET_W // FNET_GROUPS
HYENA_W = D_MODEL // 4
HYENA_ORDER = 2
HYENA_SHORT = 3
HYENA_BANDS = 16
HYENA_POS_DIM = 1 + 2 * HYENA_BANDS
HYENA_FFN_W = 64
HYENA_MIN_DECAY = math.log(1e-2) / 0.3
HYENA_MAX_DECAY = math.log(1e-2) / 1.5
N_BRANCHES = 3
IN_SPLITS = (ATTN_W, KV_W, KV_W, ATTN_W, FNET_W, FNET_W, HYENA_W, HYENA_W, HYENA_W, HYENA_W, D_MODEL, D_MODEL, D_MODEL)
IN_W = 2 * ATTN_W + 2 * KV_W + 2 * FNET_W + 4 * HYENA_W + N_BRANCHES * D_MODEL
EPS = 1e-6

kernel_name = "hybrid_gqa_fnet_hyena_diffusion_step"


def in_offsets():
    offs, s = [], 0
    for w in IN_SPLITS[:-1]:
        s += w
        offs.append(s)
    return offs


def rms_norm(x, g):
    x32 = x.astype(jnp.float32)
    y = x32 * lax.rsqrt(jnp.mean(x32 * x32, axis=-1, keepdims=True) + EPS)
    return (y * g.astype(jnp.float32)).astype(x.dtype)


def axial_rope_tables(L):
    rows = L // GRID_W
    row = jnp.repeat(jnp.arange(rows, dtype=jnp.float32), GRID_W)
    col = jnp.tile(jnp.arange(GRID_W, dtype=jnp.float32), rows)
    inv = ROPE_THETA ** (-jnp.arange(ROPE_PAIRS_AXIS, dtype=jnp.float32) / ROPE_PAIRS_AXIS)
    ang = jnp.concatenate([row[:, None] * inv, col[:, None] * inv], axis=-1)
    return jnp.cos(ang), jnp.sin(ang)


def apply_rope(x, cos, sin):
    x32 = x.astype(jnp.float32)
    x1, x2 = x32[..., :HEAD_DIM // 2], x32[..., HEAD_DIM // 2:]
    c, s = cos[None, :, None, :], sin[None, :, None, :]
    return jnp.concatenate([x1 * c - x2 * s, x2 * c + x1 * s], axis=-1).astype(x.dtype)


def block_attention(q, k, v):
    B, L = q.shape[0], q.shape[1]
    nblk = L // Q_BLOCK
    qb = q.astype(jnp.float32).reshape(B, nblk, Q_BLOCK, N_KV_HEADS, KV_GROUP, HEAD_DIM).transpose(1, 0, 2, 3, 4, 5)
    k32, v32 = k.astype(jnp.float32), v.astype(jnp.float32)

    def one_block(qblk):
        s = jnp.einsum('bqkgd,bskd->bkgqs', qblk, k32) * ATTN_SCALE
        p = jax.nn.softmax(s, axis=-1)
        return jnp.einsum('bkgqs,bskd->bqkgd', p, v32)

    o = lax.map(one_block, qb)
    return o.transpose(1, 0, 2, 3, 4, 5).reshape(B, L, ATTN_W).astype(q.dtype)


def fourier_mix(u):
    B, L, W = u.shape
    ug = u.astype(jnp.float32).reshape(B, L, FNET_GROUPS, FNET_GROUP_W)
    f = jnp.fft.fft2(ug, axes=(1, 3), norm='ortho').real
    return f.reshape(B, L, W).astype(u.dtype)


def short_conv(u, w, b):
    L = u.shape[1]
    pad = HYENA_SHORT // 2
    up = jnp.pad(u, ((0, 0), (pad, pad), (0, 0)))
    y = b
    for j in range(HYENA_SHORT):
        y = y + up[:, j:j + L] * w[j]
    return y


def hyena_filters(L, w1, b1, w2, b2, w3, b3, freq):
    f32 = jnp.float32
    t = jnp.arange(L, dtype=f32)[:, None] / L
    bands = jnp.arange(1, HYENA_BANDS + 1, dtype=f32)[None, :]
    feats = jnp.concatenate([t, jnp.cos(2 * math.pi * t * bands), jnp.sin(2 * math.pi * t * bands)], axis=-1)
    fr = freq.astype(f32)
    h = jnp.sin(fr * (feats @ w1.astype(f32) + b1.astype(f32)))
    h = jnp.sin(fr * (h @ w2.astype(f32) + b2.astype(f32)))
    h = (h @ w3.astype(f32) + b3.astype(f32)).reshape(L, HYENA_ORDER, 2, HYENA_W)
    deltas = jnp.abs(jnp.linspace(HYENA_MIN_DECAY, HYENA_MAX_DECAY, HYENA_W, dtype=f32))
    decay = jnp.exp(-t * deltas[None, :])
    h = h * decay[:, None, None, :]
    return h / (jnp.sum(jnp.abs(h), axis=(0, 2), keepdims=True) + EPS)


def bidir_long_conv(z, h_fwd, h_bwd, bias):
    L, W = h_fwd.shape
    g = jnp.concatenate([h_fwd.at[0].add(h_bwd[0]), jnp.zeros((1, W), jnp.float32), h_bwd[:0:-1]], axis=0)
    z32 = z.astype(jnp.float32)
    y = jnp.fft.irfft(jnp.fft.rfft(z32, n=2 * L, axis=1) * jnp.fft.rfft(g, axis=0)[None], n=2 * L, axis=1)[:, :L]
    return (y + z32 * bias.astype(jnp.float32)).astype(z.dtype)


def hyena_branch(hv, hx1, hx2, short_w, short_b, filt, bias):
    u = short_conv(jnp.concatenate([hv, hx1, hx2], axis=-1), short_w, short_b)
    v, x1, x2 = jnp.split(u, 3, axis=-1)
    z = x1 * bidir_long_conv(v, filt[:, 0, 0], filt[:, 0, 1], bias[0])
    z = x2 * bidir_long_conv(z, filt[:, 1, 0], filt[:, 1, 1], bias[1])
    return z


def layer(x, cond, ctx_k, ctx_v, p):
    B, L, _ = x.shape
    dt = x.dtype
    mod = jax.nn.silu(cond.astype(jnp.float32)) @ p['w_mod'].astype(jnp.float32) + p['b_mod'].astype(jnp.float32)
    shift, scale, gate = jnp.split(mod.astype(dt)[:, None, :], 3, axis=-1)
    h = rms_norm(x, p['g_pre']) * (1 + scale) + shift
    proj = h @ p['w_in']
    q, k, v, a_gate, f_in, f_gate, hv, hx1, hx2, h_gate, g_a, g_f, g_h = jnp.split(proj, in_offsets(), axis=-1)
    q = rms_norm(q.reshape(B, L, N_HEADS, HEAD_DIM), p['q_norm'])
    k = rms_norm(k.reshape(B, L, N_KV_HEADS, HEAD_DIM), p['k_norm'])
    v = v.reshape(B, L, N_KV_HEADS, HEAD_DIM)
    if ctx_k is None:
        k_all, v_all = k, v
    else:
        cos, sin = axial_rope_tables(L)
        q = apply_rope(q, cos, sin)
        k_all = jnp.concatenate([apply_rope(k, cos, sin), ctx_k.astype(dt)], axis=1)
        v_all = jnp.concatenate([v, ctx_v.astype(dt)], axis=1)
    attn = block_attention(q, k_all, v_all) * jax.nn.silu(a_gate)
    fnet = fourier_mix(f_in) * jax.nn.silu(f_gate)
    filt = hyena_filters(L, p['hy_ffn_w1'], p['hy_ffn_b1'], p['hy_ffn_w2'], p['hy_ffn_b2'],
                         p['hy_ffn_w3'], p['hy_ffn_b3'], p['hy_sin_freq'])
    hy = hyena_branch(hv, hx1, hx2, p['hy_short_w'], p['hy_short_b'], filt, p['hy_bias']) * jax.nn.silu(h_gate)
    merged = (jax.nn.sigmoid(g_a) * (attn @ p['w_attn_o'])
              + jax.nn.sigmoid(g_f) * (fnet @ p['w_fnet_o'])
              + jax.nn.sigmoid(g_h) * (hy @ p['w_hy_o']))
    out = rms_norm(merged @ p['w_out'], p['g_post'])
    return x + gate * out, k, v


def setup_inputs(seed: int = 0) -> dict:
    key = jax.random.key(seed)
    ks = jax.random.split(key, 27)
    f32 = jnp.float32

    def nrm(k, shape, scale=1.0):
        return jax.random.normal(k, shape, f32) * scale

    return {
        'x_prompt': nrm(ks[0], (BATCH, SEQ, D_MODEL)),
        'x_sample': nrm(ks[1], (DEC_BATCH, DEC_SEQ, D_MODEL)),
        'cache_k': nrm(ks[2], (DEC_BATCH, DEPTH, PAST_LEN, N_KV_HEADS, HEAD_DIM)),
        'cache_v': nrm(ks[3], (DEC_BATCH, DEPTH, PAST_LEN, N_KV_HEADS, HEAD_DIM)),
        'c': nrm(ks[4], (DEC_BATCH, D_MODEL)),
        'c_ctx': nrm(ks[5], (D_MODEL,)),
        'w_mod': nrm(ks[6], (DEPTH, D_MODEL, 3 * D_MODEL), 0.3 * D_MODEL ** -0.5),
        'b_mod': nrm(ks[7], (DEPTH, 3 * D_MODEL), 0.01),
        'g_pre': 1.0 + nrm(ks[8], (DEPTH, D_MODEL), 0.1),
        'w_in': nrm(ks[9], (DEPTH, D_MODEL, IN_W), D_MODEL ** -0.5),
        'q_norm': 1.0 + nrm(ks[10], (DEPTH, HEAD_DIM), 0.1),
        'k_norm': 1.0 + nrm(ks[11], (DEPTH, HEAD_DIM), 0.1),
        'hy_short_w': nrm(ks[12], (DEPTH, HYENA_SHORT, 3 * HYENA_W), HYENA_SHORT ** -0.5),
        'hy_short_b': nrm(ks[13], (DEPTH, 3 * HYENA_W), 0.01),
        'hy_ffn_w1': nrm(ks[14], (DEPTH, HYENA_POS_DIM, HYENA_FFN_W), HYENA_POS_DIM ** -0.5),
        'hy_ffn_b1': nrm(ks[15], (DEPTH, HYENA_FFN_W), 0.1),
        'hy_ffn_w2': nrm(ks[16], (DEPTH, HYENA_FFN_W, HYENA_FFN_W), HYENA_FFN_W ** -0.5),
        'hy_ffn_b2': nrm(ks[17], (DEPTH, HYENA_FFN_W), 0.1),
        'hy_ffn_w3': nrm(ks[18], (DEPTH, HYENA_FFN_W, HYENA_ORDER * 2 * HYENA_W), HYENA_FFN_W ** -0.5),
        'hy_ffn_b3': nrm(ks[19], (DEPTH, HYENA_ORDER * 2 * HYENA_W), 0.01),
        'hy_sin_freq': 1.0 + nrm(ks[20], (DEPTH, HYENA_FFN_W), 0.1),
        'hy_bias': nrm(ks[21], (DEPTH, HYENA_ORDER, HYENA_W), 0.1),
        'w_attn_o': nrm(ks[22], (DEPTH, ATTN_W, D_MODEL), ATTN_W ** -0.5),
        'w_fnet_o': nrm(ks[23], (DEPTH, FNET_W, D_MODEL), FNET_W ** -0.5),
        'w_hy_o': nrm(ks[24], (DEPTH, HYENA_W, D_MODEL), HYENA_W ** -0.5),
        'w_out': nrm(ks[25], (DEPTH, D_MODEL, D_MODEL), D_MODEL ** -0.5),
        'g_post': 1.0 + nrm(ks[26], (DEPTH, D_MODEL), 0.1),
    }


def reference(x_prompt, x_sample, cache_k, cache_v, c, c_ctx, w_mod, b_mod, g_pre, w_in, q_norm, k_norm,
              hy_short_w, hy_short_b, hy_ffn_w1, hy_ffn_b1, hy_ffn_w2, hy_ffn_b2, hy_ffn_w3, hy_ffn_b3,
              hy_sin_freq, hy_bias, w_attn_o, w_fnet_o, w_hy_o, w_out, g_post):
    params = [dict(w_mod=w_mod[l], b_mod=b_mod[l], g_pre=g_pre[l], w_in=w_in[l], q_norm=q_norm[l],
                   k_norm=k_norm[l], hy_short_w=hy_short_w[l], hy_short_b=hy_short_b[l],
                   hy_ffn_w1=hy_ffn_w1[l], hy_ffn_b1=hy_ffn_b1[l], hy_ffn_w2=hy_ffn_w2[l],
                   hy_ffn_b2=hy_ffn_b2[l], hy_ffn_w3=hy_ffn_w3[l], hy_ffn_b3=hy_ffn_b3[l],
                   hy_sin_freq=hy_sin_freq[l], hy_bias=hy_bias[l], w_attn_o=w_attn_o[l],
                   w_fnet_o=w_fnet_o[l], w_hy_o=w_hy_o[l], w_out=w_out[l], g_post=g_post[l])
              for l in range(DEPTH)]

    xp = x_prompt
    ctx_cond = c_ctx[None, :]
    ks_list, vs_list = [], []
    for l in range(DEPTH):
        xp, k_l, v_l = layer(xp, ctx_cond, None, None, params[l])
        ks_list.append(k_l)
        vs_list.append(v_l)
    y_prompt = xp
    new_cache_k = jnp.stack(ks_list, axis=1)
    new_cache_v = jnp.stack(vs_list, axis=1)

    xs = x_sample
    for l in range(DEPTH):
        xs, _, _ = layer(xs, c, cache_k[:, l], cache_v[:, l], params[l])
    y_sample = xs

    return (y_prompt, y_sample, new_cache_k, new_cache_v)
```

```python
import functools
import math

import jax
import jax.numpy as jnp
from jax import lax
from jax.experimental import pallas as pl
from jax.experimental.pallas import tpu as pltpu

F32 = jnp.float32
BF16 = jnp.bfloat16

D_MODEL = 4096
DEPTH = 2
GRID_W = 64
HEAD_DIM = 128
N_HEADS = 16
N_KV_HEADS = 4
KV_GROUP = N_HEADS // N_KV_HEADS
ATTN_W = N_HEADS * HEAD_DIM
KV_W = N_KV_HEADS * HEAD_DIM
GROUP_W = KV_GROUP * HEAD_DIM
ATTN_SCALE = HEAD_DIM ** -0.5
ROPE_THETA = 10000.0
ROPE_PAIRS_AXIS = HEAD_DIM // 4
FNET_W = D_MODEL // 4
FNET_GROUPS = 4
FNET_GROUP_W = FNET_W // FNET_GROUPS
HYENA_W = D_MODEL // 4
HYENA_ORDER = 2
HYENA_SHORT = 3
HYENA_BANDS = 16
HYENA_POS_DIM = 1 + 2 * HYENA_BANDS
HYENA_POS_PAD = 128
HYENA_FFN_W = 64
HYENA_MIN_DECAY = math.log(1e-2) / 0.3
HYENA_MAX_DECAY = math.log(1e-2) / 1.5
EPS = 1e-6

OFF_Q = 0
OFF_K = OFF_Q + ATTN_W
OFF_V = OFF_K + KV_W
OFF_AGATE = OFF_V + KV_W
OFF_FIN = OFF_AGATE + ATTN_W
OFF_FGATE = OFF_FIN + FNET_W
OFF_HV = OFF_FGATE + FNET_W
OFF_HX1 = OFF_HV + HYENA_W
OFF_HX2 = OFF_HX1 + HYENA_W
OFF_HGATE = OFF_HX2 + HYENA_W
OFF_GA = OFF_HGATE + HYENA_W
OFF_GF = OFF_GA + D_MODEL
OFF_GH = OFF_GF + D_MODEL
IN_W = OFF_GH + D_MODEL

MOD_ROWS = 8
VMEM_LIMIT = 56 * 1024 * 1024


def _params(*sem):
    return pltpu.CompilerParams(dimension_semantics=sem, vmem_limit_bytes=VMEM_LIMIT)


def _silu(x):
    return x * jax.nn.sigmoid(x)


def _dot(a, b):
    return jnp.dot(a, b, preferred_element_type=F32)


def _dot_exact(a, b):
    return jnp.dot(a, b, preferred_element_type=F32, precision=lax.Precision.HIGHEST)


def _completing(kern):
    def wrapped(_buf_ref, *refs):
        kern(*refs)
    return wrapped


def _buffer_call(kern, buf, in_specs, args, **kw):
    if buf is None:
        return pl.pallas_call(kern, in_specs=in_specs, **kw)(*args)
    return pl.pallas_call(_completing(kern), in_specs=[pl.BlockSpec(memory_space=pl.ANY)] + in_specs,
                          input_output_aliases={0: 0}, **kw)(buf, *args)


def _mod_kernel(c_ref, w_ref, b_ref, o_ref):
    s = _silu(c_ref[...]).astype(BF16)
    o_ref[0] = _dot(s, w_ref[0].astype(BF16)) + b_ref[0]


def _modulation(cond, w_mod, b_mod, tn=512):
    n = w_mod.shape[2]
    return pl.pallas_call(
        _mod_kernel,
        out_shape=jax.ShapeDtypeStruct((DEPTH, MOD_ROWS, n), F32),
        grid=(DEPTH, n // tn),
        in_specs=[
            pl.BlockSpec((MOD_ROWS, D_MODEL), lambda l, j: (0, 0)),
            pl.BlockSpec((1, D_MODEL, tn), lambda l, j: (l, 0, j)),
            pl.BlockSpec((1, 1, tn), lambda l, j: (l, 0, j)),
        ],
        out_specs=pl.BlockSpec((1, MOD_ROWS, tn), lambda l, j: (l, 0, j)),
        compiler_params=_params("arbitrary", "arbitrary"),
        name="modulation",
    )(cond, w_mod, b_mod.reshape(DEPTH, 1, n))


def _mod_row(i, n_ctx_tiles, tiles_per_latent_seq):
    return jnp.where(i < n_ctx_tiles, 0, 1 + (i - n_ctx_tiles) // tiles_per_latent_seq)


def _prenorm_kernel(x_ref, g_ref, shift_ref, scale_ref, o_ref):
    x = x_ref[...]
    y = x * lax.rsqrt(jnp.mean(x * x, axis=-1, keepdims=True) + EPS) * g_ref[...]
    o_ref[...] = (y * (1.0 + scale_ref[0]) + shift_ref[0]).astype(BF16)


def _prenorm(x, g, mod3, n_ctx, lat_seq, tm=256):
    t = x.shape[0]
    row = functools.partial(_mod_row, n_ctx_tiles=n_ctx // tm, tiles_per_latent_seq=lat_seq // tm)
    return pl.pallas_call(
        _prenorm_kernel,
        out_shape=jax.ShapeDtypeStruct((t, D_MODEL), BF16),
        grid=(t // tm,),
        in_specs=[
            pl.BlockSpec((tm, D_MODEL), lambda i: (i, 0)),
            pl.BlockSpec((1, D_MODEL), lambda i: (0, 0)),
            pl.BlockSpec((1, 1, D_MODEL), lambda i: (row(i), 0, 0)),
            pl.BlockSpec((1, 1, D_MODEL), lambda i: (row(i), 0, 1)),
        ],
        out_specs=pl.BlockSpec((tm, D_MODEL), lambda i: (i, 0)),
        compiler_params=_params("arbitrary"),
        name="prenorm",
    )(x, g.reshape(1, D_MODEL), mod3, mod3)


def _mm_kernel(a_ref, b_ref, o_ref):
    o_ref[...] = _dot(a_ref[...], b_ref[...]).astype(o_ref.dtype)


def _matmul(a, b, out_dtype, tm, tn, name):
    m, k = a.shape
    n = b.shape[1]
    tm, tn = min(tm, m), min(tn, n)
    return pl.pallas_call(
        _mm_kernel,
        out_shape=jax.ShapeDtypeStruct((m, n), out_dtype),
        grid=(m // tm, n // tn),
        in_specs=[
            pl.BlockSpec((tm, k), lambda i, j: (i, 0)),
            pl.BlockSpec((k, tn), lambda i, j: (0, j)),
        ],
        out_specs=pl.BlockSpec((tm, tn), lambda i, j: (i, j)),
        compiler_params=_params("arbitrary", "arbitrary"),
        name=name,
    )(a, b)


def _kv_kernel(h_ref, w_ref, kn_ref, k_ref, v_ref):
    acc = _dot(h_ref[...], w_ref[...])
    for hd in range(N_KV_HEADS):
        sl = slice(hd * HEAD_DIM, (hd + 1) * HEAD_DIM)
        k_ref[:, sl] = _norm_head(acc[:, sl], kn_ref[...])
    v_ref[...] = acc[:, KV_W:]


def _kv_proj(h, w_in, k_norm, tm=512):
    t = h.shape[0]
    return pl.pallas_call(
        _kv_kernel,
        out_shape=(jax.ShapeDtypeStruct((t, KV_W), F32), jax.ShapeDtypeStruct((t, KV_W), F32)),
        grid=(t // tm,),
        in_specs=[
            pl.BlockSpec((tm, D_MODEL), lambda i: (i, 0)),
            pl.BlockSpec((D_MODEL, 2 * KV_W), lambda i: (0, OFF_K // (2 * KV_W))),
            pl.BlockSpec((1, HEAD_DIM), lambda i: (0, 0)),
        ],
        out_specs=(pl.BlockSpec((tm, KV_W), lambda i: (i, 0)), pl.BlockSpec((tm, KV_W), lambda i: (i, 0))),
        compiler_params=_params("arbitrary"),
        name="kv_proj",
    )(h, w_in, k_norm.reshape(1, HEAD_DIM))


def _softmax_pv(s, v):
    m = jnp.max(s, axis=-1, keepdims=True)
    p = jnp.exp(s - m)
    l = jnp.sum(p, axis=-1, keepdims=True)
    return _dot(p.astype(BF16), v) / l


def _qk(q, k):
    return lax.dot_general(q, k, (((1,), (1,)), ((), ())), preferred_element_type=F32)


def _norm_head(q, g):
    return q * lax.rsqrt(jnp.mean(q * q, axis=-1, keepdims=True) + EPS) * g


def _rope(x, cos2, sin2):
    return x * cos2 + pltpu.roll(x, HEAD_DIM // 2, 1) * sin2


def _attn_ctx_kernel(q_ref, k_ref, v_ref, ag_ref, qn_ref, o_ref):
    k = k_ref[...].astype(BF16)
    v = v_ref[...].astype(BF16)
    for g in range(KV_GROUP):
        sl = slice(g * HEAD_DIM, (g + 1) * HEAD_DIM)
        q = _norm_head(q_ref[:, sl].astype(F32), qn_ref[...]) * ATTN_SCALE
        o = _softmax_pv(_qk(q.astype(BF16), k), v)
        o_ref[:, sl] = (o * _silu(ag_ref[:, sl].astype(F32))).astype(BF16)


def _attention_ctx(proj, k, v, q_norm, n_seq, seq, out_rows):
    return pl.pallas_call(
        _attn_ctx_kernel,
        out_shape=jax.ShapeDtypeStruct((out_rows, ATTN_W), BF16),
        grid=(n_seq, N_KV_HEADS),
        in_specs=[
            pl.BlockSpec((seq, GROUP_W), lambda b, h: (b, OFF_Q // GROUP_W + h)),
            pl.BlockSpec((seq, HEAD_DIM), lambda b, h: (b, h)),
            pl.BlockSpec((seq, HEAD_DIM), lambda b, h: (b, h)),
            pl.BlockSpec((seq, GROUP_W), lambda b, h: (b, OFF_AGATE // GROUP_W + h)),
            pl.BlockSpec((1, HEAD_DIM), lambda b, h: (0, 0)),
        ],
        out_specs=pl.BlockSpec((seq, GROUP_W), lambda b, h: (b, h)),
        compiler_params=_params("arbitrary", "arbitrary"),
        name="attn_ctx",
    )(proj, k, v, proj, q_norm.reshape(1, HEAD_DIM))


def _attn_lat_kernel(q_ref, k_ref, v_ref, ck_ref, cv_ref, ag_ref, qn_ref, cosq_ref, sinq_ref,
                     cosk_ref, sink_ref, o_ref, k_s, v_s, *, seq):
    @pl.when(pl.program_id(2) == 0)
    def _():
        k_s[:seq] = _rope(k_ref[...], cosk_ref[...], sink_ref[...]).astype(BF16)
        k_s[seq:] = ck_ref[...].astype(BF16)
        v_s[:seq] = v_ref[...].astype(BF16)
        v_s[seq:] = cv_ref[...].astype(BF16)

    k = k_s[...]
    v = v_s[...]
    for g in range(KV_GROUP):
        sl = slice(g * HEAD_DIM, (g + 1) * HEAD_DIM)
        q = _norm_head(q_ref[:, sl].astype(F32), qn_ref[...])
        q = _rope(q, cosq_ref[...], sinq_ref[...]) * ATTN_SCALE
        o = _softmax_pv(_qk(q.astype(BF16), k), v)
        o_ref[:, sl] = (o * _silu(ag_ref[:, sl].astype(F32))).astype(BF16)


def _attention_lat(buf, proj, k, v, cache_k, cache_v, layer, q_norm, cos2, sin2, n_ctx, n_seq, seq, tq=512):
    past = cache_k.shape[2]
    ck = cache_k.reshape(cache_k.shape[0], DEPTH, past, KV_W)
    cv = cache_v.reshape(cache_v.shape[0], DEPTH, past, KV_W)
    nq = seq // tq
    row0, seq0 = n_ctx // tq, n_ctx // seq
    in_specs = [
        pl.BlockSpec((tq, GROUP_W), lambda b, h, i: (row0 + b * nq + i, OFF_Q // GROUP_W + h)),
        pl.BlockSpec((seq, HEAD_DIM), lambda b, h, i: (seq0 + b, h)),
        pl.BlockSpec((seq, HEAD_DIM), lambda b, h, i: (seq0 + b, h)),
        pl.BlockSpec((None, None, past, HEAD_DIM), lambda b, h, i: (b, layer, 0, h)),
        pl.BlockSpec((None, None, past, HEAD_DIM), lambda b, h, i: (b, layer, 0, h)),
        pl.BlockSpec((tq, GROUP_W), lambda b, h, i: (row0 + b * nq + i, OFF_AGATE // GROUP_W + h)),
        pl.BlockSpec((1, HEAD_DIM), lambda b, h, i: (0, 0)),
        pl.BlockSpec((tq, HEAD_DIM), lambda b, h, i: (i, 0)),
        pl.BlockSpec((tq, HEAD_DIM), lambda b, h, i: (i, 0)),
        pl.BlockSpec((seq, HEAD_DIM), lambda b, h, i: (0, 0)),
        pl.BlockSpec((seq, HEAD_DIM), lambda b, h, i: (0, 0)),
    ]
    return _buffer_call(
        functools.partial(_attn_lat_kernel, seq=seq), buf, in_specs,
        (proj, k, v, ck, cv, proj, q_norm.reshape(1, HEAD_DIM), cos2, sin2, cos2, sin2),
        out_shape=jax.ShapeDtypeStruct(buf.shape, BF16),
        grid=(n_seq, N_KV_HEADS, nq),
        out_specs=pl.BlockSpec((tq, GROUP_W), lambda b, h, i: (row0 + b * nq + i, h)),
        scratch_shapes=[pltpu.VMEM((seq + past, HEAD_DIM), BF16), pltpu.VMEM((seq + past, HEAD_DIM), BF16)],
        compiler_params=_params("arbitrary", "arbitrary", "arbitrary"),
        name="attn_lat",
    )


def _rope_tables(seq):
    rows = seq // GRID_W
    row = jnp.repeat(jnp.arange(rows, dtype=F32), GRID_W)
    col = jnp.tile(jnp.arange(GRID_W, dtype=F32), rows)
    inv = ROPE_THETA ** (-jnp.arange(ROPE_PAIRS_AXIS, dtype=F32) / ROPE_PAIRS_AXIS)
    ang = jnp.concatenate([row[:, None] * inv, col[:, None] * inv], axis=-1)
    cos, sin = jnp.cos(ang), jnp.sin(ang)
    return jnp.concatenate([cos, cos], axis=-1), jnp.concatenate([-sin, sin], axis=-1)


def _fnet_kernel(u_ref, fg_ref, csc_ref, ml_ref, o_ref, t_s, *, seq, scale):
    @pl.when(pl.program_id(1) == 0)
    def _():
        for g in range(FNET_GROUPS):
            sl = slice(g * FNET_GROUP_W, (g + 1) * FNET_GROUP_W)
            t = _dot(u_ref[:, sl], csc_ref[...])
            t_s[:seq, sl] = t[:, :FNET_GROUP_W].astype(BF16)
            t_s[seq:, sl] = t[:, FNET_GROUP_W:].astype(BF16)

    y = _dot(ml_ref[...], t_s[...]) * scale
    o_ref[...] = (y * _silu(fg_ref[...].astype(F32))).astype(BF16)


def _dft_angle(i, j, n):
    return (2.0 * math.pi / n) * ((i[:, None] * j[None, :]) % n).astype(F32)


def _fnet_tables(seq):
    j = jnp.arange(FNET_GROUP_W, dtype=jnp.int32)
    ang = _dft_angle(j, j, FNET_GROUP_W)
    csc = jnp.concatenate([jnp.cos(ang), jnp.sin(ang)], axis=1).astype(BF16)
    t = jnp.arange(seq, dtype=jnp.int32)
    ang = _dft_angle(t, t, seq)
    ml = jnp.concatenate([jnp.cos(ang), -jnp.sin(ang)], axis=1).astype(BF16)
    return csc, ml


def _fnet(buf, proj, tables, row_start, n_seq, seq, out_rows, tr=256):
    csc, ml = tables
    nr = seq // tr
    s0, r0 = row_start // seq, row_start // tr
    in_specs = [
        pl.BlockSpec((seq, FNET_W), lambda b, r: (s0 + b, OFF_FIN // FNET_W)),
        pl.BlockSpec((tr, FNET_W), lambda b, r: (r0 + b * nr + r, OFF_FGATE // FNET_W)),
        pl.BlockSpec((FNET_GROUP_W, 2 * FNET_GROUP_W), lambda b, r: (0, 0)),
        pl.BlockSpec((tr, 2 * seq), lambda b, r: (r, 0)),
    ]
    return _buffer_call(
        functools.partial(_fnet_kernel, seq=seq, scale=1.0 / math.sqrt(seq * FNET_GROUP_W)), buf, in_specs,
        (proj, proj, csc, ml),
        out_shape=jax.ShapeDtypeStruct((out_rows, FNET_W), BF16),
        grid=(n_seq, nr),
        out_specs=pl.BlockSpec((tr, FNET_W), lambda b, r: (r0 + b * nr + r, 0)),
        scratch_shapes=[pltpu.VMEM((2 * seq, FNET_W), BF16)],
        compiler_params=_params("arbitrary", "arbitrary"),
        name=f"fnet_{seq}",
    )


def _short_kernel(x_ref, w_ref, b_ref, o_ref):
    x = x_ref[...].astype(F32)
    n = x.shape[0]
    row = lax.broadcasted_iota(jnp.int32, x.shape, 0)
    prev = jnp.where(row == 0, 0.0, pltpu.roll(x, 1, 0))
    nxt = jnp.where(row == n - 1, 0.0, pltpu.roll(x, n - 1, 0))
    y = b_ref[...] + prev * w_ref[0:1, :] + x * w_ref[1:2, :] + nxt * w_ref[2:3, :]
    o_ref[...] = y.astype(BF16)


def _short_conv(buf, proj, w, b, row_start, n_seq, seq, out_rows, tw=512):
    width = 3 * HYENA_W
    s0 = row_start // seq
    in_specs = [
        pl.BlockSpec((seq, tw), lambda s, c: (s0 + s, OFF_HV // tw + c)),
        pl.BlockSpec((HYENA_SHORT, tw), lambda s, c: (0, c)),
        pl.BlockSpec((1, tw), lambda s, c: (0, c)),
    ]
    return _buffer_call(
        _short_kernel, buf, in_specs, (proj, w, b.reshape(1, width)),
        out_shape=jax.ShapeDtypeStruct((out_rows, width), BF16),
        grid=(n_seq, width // tw),
        out_specs=pl.BlockSpec((seq, tw), lambda s, c: (s0 + s, c)),
        compiler_params=_params("arbitrary", "arbitrary"),
        name=f"short_conv_{seq}",
    )


def _filt_kernel(feats_ref, t_ref, w1_ref, b1_ref, w2_ref, b2_ref, w3f_ref, w3b_ref, b3f_ref, b3b_ref,
                 fr_ref, dl_ref, of_ref, ob_ref):
    fr = fr_ref[...]
    h = jnp.sin(fr * (_dot_exact(feats_ref[...], w1_ref[...]) + b1_ref[...]))
    h = jnp.sin(fr * (_dot_exact(h, w2_ref[...]) + b2_ref[...]))
    decay = jnp.exp(-t_ref[...] * dl_ref[...])
    hf = (_dot_exact(h, w3f_ref[...]) + b3f_ref[...]) * decay
    hb = (_dot_exact(h, w3b_ref[...]) + b3b_ref[...]) * decay
    den = jnp.sum(jnp.abs(hf), axis=0, keepdims=True) + jnp.sum(jnp.abs(hb), axis=0, keepdims=True) + EPS
    of_ref[...] = (hf / den).astype(BF16)
    ob_ref[...] = (hb / den).astype(BF16)


def _hyena_filters(seq, w1, b1, w2, b2, w3, b3, freq, tw=256):
    t = jnp.arange(seq, dtype=F32)[:, None] / seq
    bands = jnp.arange(1, HYENA_BANDS + 1, dtype=F32)[None, :]
    feats = jnp.concatenate([t, jnp.cos(2 * math.pi * t * bands), jnp.sin(2 * math.pi * t * bands),
                             jnp.zeros((seq, HYENA_POS_PAD - HYENA_POS_DIM), F32)], axis=-1)
    w1p = jnp.concatenate([w1, jnp.zeros((HYENA_POS_PAD - HYENA_POS_DIM, HYENA_FFN_W), F32)], axis=0)
    deltas = jnp.abs(jnp.linspace(HYENA_MIN_DECAY, HYENA_MAX_DECAY, HYENA_W, dtype=F32))[None, :]
    nw = HYENA_W // tw
    small = lambda shape: pl.BlockSpec(shape, lambda o, c: (0, 0))
    out_sds = jax.ShapeDtypeStruct((seq, HYENA_ORDER * HYENA_W), BF16)
    fwd = lambda o, c: (0, (2 * o) * nw + c)
    bwd = lambda o, c: (0, (2 * o + 1) * nw + c)
    b3r = b3.reshape(1, -1)
    return pl.pallas_call(
        _filt_kernel,
        out_shape=(out_sds, out_sds),
        grid=(HYENA_ORDER, nw),
        in_specs=[
            small((seq, HYENA_POS_PAD)), small((seq, 1)),
            small((HYENA_POS_PAD, HYENA_FFN_W)), small((1, HYENA_FFN_W)),
            small((HYENA_FFN_W, HYENA_FFN_W)), small((1, HYENA_FFN_W)),
            pl.BlockSpec((HYENA_FFN_W, tw), fwd), pl.BlockSpec((HYENA_FFN_W, tw), bwd),
            pl.BlockSpec((1, tw), fwd), pl.BlockSpec((1, tw), bwd),
            small((1, HYENA_FFN_W)),
            pl.BlockSpec((1, tw), lambda o, c: (0, c)),
        ],
        out_specs=(pl.BlockSpec((seq, tw), lambda o, c: (0, o * nw + c)),
                   pl.BlockSpec((seq, tw), lambda o, c: (0, o * nw + c))),
        compiler_params=_params("arbitrary", "arbitrary"),
        name=f"hyena_filters_{seq}",
    )(feats, t, w1p, b1.reshape(1, -1), w2, b2.reshape(1, -1), w3, w3, b3r, b3r, freq.reshape(1, -1), deltas)


def _conv_tables(seq):
    n = 2 * seq
    i = jnp.arange(seq, dtype=jnp.int32)
    ang = _dft_angle(i, i, n)
    nyq = (1 - 2 * (i % 2)).astype(F32)
    mc = jnp.cos(ang)
    ms = jnp.where(i[:, None] == 0, nyq[None, :], jnp.sin(ang))
    wgt = jnp.where(i == 0, 1.0, 2.0)[None, :]
    ic = mc * wgt
    is_ = jnp.where(i[None, :] == 0, nyq[:, None], jnp.sin(ang) * wgt)
    return mc.astype(BF16), ms.astype(BF16), ic.astype(BF16), is_.astype(BF16)


def _lconv_kernel(*refs, n, gated):
    if gated:
        (z_ref, xm_ref, gate_ref, bias_ref, mc_ref, ms_ref, ic_ref, is_ref,
         ucf_ref, ucb_ref, usf_ref, usb_ref, o_ref, acc_ref) = refs
    else:
        (z_ref, xm_ref, bias_ref, mc_ref, ms_ref, ic_ref, is_ref,
         ucf_ref, ucb_ref, usf_ref, usb_ref, o_ref, acc_ref) = refs
    f = pl.program_id(2)
    z = z_ref[...]
    zr = _dot(mc_ref[...], z)
    zi = _dot(ms_ref[...], z)
    usf, usb = usf_ref[...], usb_ref[...]
    gr = ucf_ref[...] + ucb_ref[...]
    gi = usf - usb
    packed = jnp.logical_and(lax.broadcasted_iota(jnp.int32, gr.shape, 0) == 0, f == 0)
    gi = jnp.where(packed, 0.0, gi)
    gn = jnp.where(packed, usf + usb, gr)
    pr = (zr * gr - zi * gi).astype(BF16)
    pi = (zr * gi + zi * gn).astype(BF16)
    contrib = _dot(ic_ref[...], pr) + _dot(is_ref[...], pi)

    @pl.when(f == 0)
    def _():
        acc_ref[...] = contrib

    @pl.when(f > 0)
    def _():
        acc_ref[...] += contrib

    @pl.when(f == pl.num_programs(2) - 1)
    def _():
        y = acc_ref[...] * (1.0 / n) + z.astype(F32) * bias_ref[...]
        y = y * xm_ref[...].astype(F32)
        if gated:
            y = y * _silu(gate_ref[...].astype(F32))
        o_ref[...] = y.astype(BF16)


def _long_conv(buf, z, z_col, xm, xm_col, gate, bias, idx, order, tables, spec_f, spec_b,
               row_start, n_seq, seq, out_rows, tw=512, fc=256):
    mc, ms, ic, is_ = tables
    fc = min(fc, seq)
    nw, nf = HYENA_W // tw, seq // fc
    s0 = row_start // seq
    gated = gate is not None
    tok = lambda col: pl.BlockSpec((seq, tw), lambda s, w, f: (s0 + s, col * nw + w))
    spec_c = pl.BlockSpec((fc, tw), lambda s, w, f: (f, order * nw + w))
    spec_s = pl.BlockSpec((fc, tw), lambda s, w, f: (nf + f, order * nw + w))
    in_specs = [tok(z_col), tok(xm_col)]
    args = [z, xm]
    if gated:
        in_specs.append(tok(OFF_HGATE // HYENA_W))
        args.append(gate)
    in_specs += [
        pl.BlockSpec((None, 1, tw), lambda s, w, f: (idx, 0, w)),
        pl.BlockSpec((fc, seq), lambda s, w, f: (f, 0)),
        pl.BlockSpec((fc, seq), lambda s, w, f: (f, 0)),
        pl.BlockSpec((seq, fc), lambda s, w, f: (0, f)),
        pl.BlockSpec((seq, fc), lambda s, w, f: (0, f)),
        spec_c, spec_c, spec_s, spec_s,
    ]
    args += [bias, mc, ms, ic, is_, spec_f, spec_b, spec_f, spec_b]
    return _buffer_call(
        functools.partial(_lconv_kernel, n=2 * seq, gated=gated), buf, in_specs, args,
        out_shape=jax.ShapeDtypeStruct((out_rows, HYENA_W), BF16),
        grid=(n_seq, nw, nf),
        out_specs=pl.BlockSpec((seq, tw), lambda s, w, f: (s0 + s, w)),
        scratch_shapes=[pltpu.VMEM((seq, tw), F32)],
        compiler_params=_params("arbitrary", "arbitrary", "arbitrary"),
        name=f"long_conv_{seq}_{order}",
    )


def _merge_kernel(a_ref, f_ref, h_ref, wa_ref, wf_ref, wh_ref, ga_ref, gf_ref, gh_ref, o_ref):
    m = jax.nn.sigmoid(ga_ref[...].astype(F32)) * _dot(a_ref[...], wa_ref[...])
    m = m + jax.nn.sigmoid(gf_ref[...].astype(F32)) * _dot(f_ref[...], wf_ref[...])
    m = m + jax.nn.sigmoid(gh_ref[...].astype(F32)) * _dot(h_ref[...], wh_ref[...])
    o_ref[...] = m.astype(BF16)


def _merge(attn, fnet, hy, w_a, w_f, w_h, proj, tm=1024, tn=512):
    t = attn.shape[0]
    act = lambda w: pl.BlockSpec((tm, w), lambda i, j: (i, 0))
    wgt = lambda k: pl.BlockSpec((k, tn), lambda i, j: (0, j))
    gate = lambda off: pl.BlockSpec((tm, tn), lambda i, j: (i, off // tn + j))
    return pl.pallas_call(
        _merge_kernel,
        out_shape=jax.ShapeDtypeStruct((t, D_MODEL), BF16),
        grid=(t // tm, D_MODEL // tn),
        in_specs=[act(ATTN_W), act(FNET_W), act(HYENA_W), wgt(ATTN_W), wgt(FNET_W), wgt(HYENA_W),
                  gate(OFF_GA), gate(OFF_GF), gate(OFF_GH)],
        out_specs=pl.BlockSpec((tm, tn), lambda i, j: (i, j)),
        compiler_params=_params("arbitrary", "arbitrary"),
        name="merge",
    )(attn, fnet, hy, w_a, w_f, w_h, proj, proj, proj)


def _post_kernel(x_ref, r_ref, g_ref, gate_ref, o_ref):
    r = r_ref[...]
    y = r * lax.rsqrt(jnp.mean(r * r, axis=-1, keepdims=True) + EPS) * g_ref[...]
    o_ref[...] = x_ref[...] + gate_ref[0] * y


def _post(x, r, g, mod3, n_ctx, lat_seq, tm=256):
    t = x.shape[0]
    row = functools.partial(_mod_row, n_ctx_tiles=n_ctx // tm, tiles_per_latent_seq=lat_seq // tm)
    return pl.pallas_call(
        _post_kernel,
        out_shape=jax.ShapeDtypeStruct((t, D_MODEL), F32),
        grid=(t // tm,),
        in_specs=[
            pl.BlockSpec((tm, D_MODEL), lambda i: (i, 0)),
            pl.BlockSpec((tm, D_MODEL), lambda i: (i, 0)),
            pl.BlockSpec((1, D_MODEL), lambda i: (0, 0)),
            pl.BlockSpec((1, 1, D_MODEL), lambda i: (row(i), 0, 2)),
        ],
        out_specs=pl.BlockSpec((tm, D_MODEL), lambda i: (i, 0)),
        compiler_params=_params("arbitrary"),
        name="post",
    )(x, r, g.reshape(1, D_MODEL), mod3)


def kernel(x_prompt, x_sample, cache_k, cache_v, c, c_ctx, w_mod, b_mod, g_pre, w_in, q_norm, k_norm, hy_short_w, hy_short_b, hy_ffn_w1, hy_ffn_b1, hy_ffn_w2, hy_ffn_b2, hy_ffn_w3, hy_ffn_b3, hy_sin_freq, hy_bias, w_attn_o, w_fnet_o, w_hy_o, w_out, g_post):
    nb, seq, _ = x_prompt.shape
    db, dseq, _ = x_sample.shape
    assert 1 + db <= MOD_ROWS
    n_ctx, n_lat = nb * seq, db * dseq
    t = n_ctx + n_lat
    groups = ((None, 0, nb, seq), ("buf", n_ctx, db, dseq))

    x = jnp.concatenate([x_prompt.reshape(n_ctx, D_MODEL), x_sample.reshape(n_lat, D_MODEL)], axis=0)
    cond = jnp.concatenate([c_ctx[None, :], c, jnp.zeros((MOD_ROWS - 1 - db, D_MODEL), F32)], axis=0)
    mod = _modulation(cond, w_mod, b_mod)

    cos2, sin2 = _rope_tables(dseq)
    fnet_tables = {s: _fnet_tables(s) for s in (seq, dseq)}
    conv_tables = {s: _conv_tables(s) for s in (seq, dseq)}
    bias = hy_bias.reshape(DEPTH * HYENA_ORDER, 1, HYENA_W)

    new_k, new_v = [], []
    for l in range(DEPTH):
        mod3 = mod[l].reshape(MOD_ROWS, 1, 3 * D_MODEL)
        h = _prenorm(x, g_pre[l], mod3, n_ctx, dseq)
        w_in_l = w_in[l].astype(BF16)
        proj = _matmul(h, w_in_l, BF16, 1024, 512, "in_proj")
        k, v = _kv_proj(h, w_in_l, k_norm[l])
        new_k.append(k[:n_ctx].reshape(nb, seq, N_KV_HEADS, HEAD_DIM))
        new_v.append(v[:n_ctx].reshape(nb, seq, N_KV_HEADS, HEAD_DIM))

        attn = _attention_ctx(proj, k, v, q_norm[l], nb, seq, t)
        attn = _attention_lat(attn, proj, k, v, cache_k, cache_v, l, q_norm[l], cos2, sin2, n_ctx, db, dseq)

        fnet = u = z1 = hy = None
        for first, row_start, n_seq, s in groups:
            fnet = _fnet(fnet if first else None, proj, fnet_tables[s], row_start, n_seq, s, t)
            u = _short_conv(u if first else None, proj, hy_short_w[l], hy_short_b[l], row_start, n_seq, s, t)
        for first, row_start, n_seq, s in groups:
            filt_f, filt_b = _hyena_filters(s, hy_ffn_w1[l], hy_ffn_b1[l], hy_ffn_w2[l], hy_ffn_b2[l],
                                            hy_ffn_w3[l], hy_ffn_b3[l], hy_sin_freq[l])
            mc, ms = conv_tables[s][:2]
            fwd = jnp.concatenate([mc, ms], axis=0)
            spec_f = _matmul(fwd, filt_f, F32, 512, 512, f"filter_spectrum_{s}")
            spec_b = _matmul(fwd, filt_b, F32, 512, 512, f"filter_spectrum_{s}")
            z1 = _long_conv(z1 if first else None, u, 0, u, 1, None, bias, l * HYENA_ORDER, 0, conv_tables[s],
                            spec_f, spec_b, row_start, n_seq, s, t)
            hy = _long_conv(hy if first else None, z1, 0, u, 2, proj, bias, l * HYENA_ORDER + 1, 1, conv_tables[s],
                            spec_f, spec_b, row_start, n_seq, s, t)

        merged = _merge(attn, fnet, hy, w_attn_o[l].astype(BF16), w_fnet_o[l].astype(BF16),
                        w_hy_o[l].astype(BF16), proj)
        r = _matmul(merged, w_out[l].astype(BF16), F32, 1024, 512, "out_proj")
        x = _post(x, r, g_post[l], mod3, n_ctx, dseq)

    y_prompt = x[:n_ctx].reshape(nb, seq, D_MODEL)
    y_sample = x[n_ctx:].reshape(db, dseq, D_MODEL)
    return (y_prompt, y_sample, jnp.stack(new_k, axis=1), jnp.stack(new_v, axis=1))
```

```python
import functools
import math

import jax
import jax.numpy as jnp
from jax import lax
from jax.experimental import pallas as pl
from jax.experimental.pallas import tpu as pltpu

F32 = jnp.float32
BF16 = jnp.bfloat16

D_MODEL = 4096
DEPTH = 2
GRID_W = 64
HEAD_DIM = 128
N_HEADS = 16
N_KV_HEADS = 4
KV_GROUP = N_HEADS // N_KV_HEADS
ATTN_W = N_HEADS * HEAD_DIM
KV_W = N_KV_HEADS * HEAD_DIM
GROUP_W = KV_GROUP * HEAD_DIM
ATTN_SCALE = HEAD_DIM ** -0.5
ROPE_THETA = 10000.0
ROPE_PAIRS_AXIS = HEAD_DIM // 4
FNET_W = D_MODEL // 4
FNET_GROUPS = 4
FNET_GROUP_W = FNET_W // FNET_GROUPS
HYENA_W = D_MODEL // 4
HYENA_ORDER = 2
HYENA_SHORT = 3
HYENA_BANDS = 16
HYENA_POS_DIM = 1 + 2 * HYENA_BANDS
HYENA_POS_PAD = 128
HYENA_FFN_W = 64
HYENA_MIN_DECAY = math.log(1e-2) / 0.3
HYENA_MAX_DECAY = math.log(1e-2) / 1.5
EPS = 1e-6

W_OFF_K = ATTN_W
W_KV = 2 * KV_W
OFF_Q = 0
OFF_AGATE = OFF_Q + ATTN_W
OFF_FIN = OFF_AGATE + ATTN_W
OFF_FGATE = OFF_FIN + FNET_W
OFF_HV = OFF_FGATE + FNET_W
OFF_HX1 = OFF_HV + HYENA_W
OFF_HX2 = OFF_HX1 + HYENA_W
OFF_HGATE = OFF_HX2 + HYENA_W
OFF_GA = OFF_HGATE + HYENA_W
OFF_GF = OFF_GA + D_MODEL
OFF_GH = OFF_GF + D_MODEL
PROJ_W = OFF_GH + D_MODEL

MOD_ROWS = 8
ROW_TILE = 256
VMEM_LIMIT = 56 * 1024 * 1024


def _params(*sem):
    return pltpu.CompilerParams(dimension_semantics=sem, vmem_limit_bytes=VMEM_LIMIT)


def _silu(x):
    return x * jax.nn.sigmoid(x)


def _dot(a, b):
    return jnp.dot(a, b, preferred_element_type=F32)


def _dot_exact(a, b):
    return jnp.dot(a, b, preferred_element_type=F32, precision=lax.Precision.HIGHEST)


def _completing(kern):
    def wrapped(_buf_ref, *refs):
        kern(*refs)
    return wrapped


def _buffer_call(kern, buf, in_specs, args, **kw):
    if buf is None:
        return pl.pallas_call(kern, in_specs=in_specs, **kw)(*args)
    return pl.pallas_call(_completing(kern), in_specs=[pl.BlockSpec(memory_space=pl.ANY)] + in_specs,
                          input_output_aliases={0: 0}, **kw)(buf, *args)


def _mod_kernel(c_ref, w_ref, b_ref, o_ref):
    s = _silu(c_ref[...]).astype(BF16)
    o_ref[0] = _dot(s, w_ref[0].astype(BF16)) + b_ref[0]


def _modulation(cond, w_mod, b_mod, tn=512):
    n = w_mod.shape[2]
    return pl.pallas_call(
        _mod_kernel,
        out_shape=jax.ShapeDtypeStruct((DEPTH, MOD_ROWS, n), F32),
        grid=(DEPTH, n // tn),
        in_specs=[
            pl.BlockSpec((MOD_ROWS, D_MODEL), lambda l, j: (0, 0)),
            pl.BlockSpec((1, D_MODEL, tn), lambda l, j: (l, 0, j)),
            pl.BlockSpec((1, 1, tn), lambda l, j: (l, 0, j)),
        ],
        out_specs=pl.BlockSpec((1, MOD_ROWS, tn), lambda l, j: (l, 0, j)),
        compiler_params=_params("arbitrary", "arbitrary"),
        name="modulation",
    )(cond, w_mod, b_mod.reshape(DEPTH, 1, n))


def _prenorm_kernel(x_ref, g_ref, shift_ref, scale_ref, o_ref):
    x = x_ref[...]
    y = x * lax.rsqrt(jnp.mean(x * x, axis=-1, keepdims=True) + EPS) * g_ref[...]
    o_ref[...] = (y * (1.0 + scale_ref[0]) + shift_ref[0]).astype(BF16)


def _prenorm(buf, x, in_row0, g, mod3, mod_row, out_row0, n_rows, out_rows):
    tm = ROW_TILE
    i0, o0 = in_row0 // tm, out_row0 // tm
    in_specs = [
        pl.BlockSpec((tm, D_MODEL), lambda i: (i0 + i, 0)),
        pl.BlockSpec((1, D_MODEL), lambda i: (0, 0)),
        pl.BlockSpec((1, 1, D_MODEL), lambda i: (mod_row(i), 0, 0)),
        pl.BlockSpec((1, 1, D_MODEL), lambda i: (mod_row(i), 0, 1)),
    ]
    return _buffer_call(
        _prenorm_kernel, buf, in_specs, (x, g.reshape(1, D_MODEL), mod3, mod3),
        out_shape=jax.ShapeDtypeStruct((out_rows, D_MODEL), BF16),
        grid=(n_rows // tm,),
        out_specs=pl.BlockSpec((tm, D_MODEL), lambda i: (o0 + i, 0)),
        compiler_params=_params("arbitrary"),
        name="prenorm",
    )


def _post_kernel(x_ref, r_ref, g_ref, gate_ref, o_ref):
    r = r_ref[...].astype(F32)
    y = r * lax.rsqrt(jnp.mean(r * r, axis=-1, keepdims=True) + EPS) * g_ref[...]
    o_ref[...] = x_ref[...] + gate_ref[0] * y


def _post(buf, x, in_row0, r, r_row0, g, mod3, mod_row, out_row0, n_rows, out_rows):
    tm = ROW_TILE
    i0, r0, o0 = in_row0 // tm, r_row0 // tm, out_row0 // tm
    in_specs = [
        pl.BlockSpec((tm, D_MODEL), lambda i: (i0 + i, 0)),
        pl.BlockSpec((tm, D_MODEL), lambda i: (r0 + i, 0)),
        pl.BlockSpec((1, D_MODEL), lambda i: (0, 0)),
        pl.BlockSpec((1, 1, D_MODEL), lambda i: (mod_row(i), 0, 2)),
    ]
    return _buffer_call(
        _post_kernel, buf, in_specs, (x, r, g.reshape(1, D_MODEL), mod3),
        out_shape=jax.ShapeDtypeStruct((out_rows, D_MODEL), F32),
        grid=(n_rows // tm,),
        out_specs=pl.BlockSpec((tm, D_MODEL), lambda i: (o0 + i, 0)),
        compiler_params=_params("arbitrary"),
        name="post",
    )


def _mm_kernel(a_ref, b_ref, o_ref):
    o_ref[...] = _dot(a_ref[...], b_ref[...]).astype(o_ref.dtype)


def _matmul(a, b, out_dtype, tm, tn, name, layer=None, n_out=None, col_map=None):
    m, k = a.shape
    n = n_out or b.shape[-1]
    tm, tn = min(tm, m), min(tn, n)
    col = col_map or (lambda j: j)
    if layer is None:
        b_spec = pl.BlockSpec((k, tn), lambda i, j: (0, col(j)))
    else:
        b_spec = pl.BlockSpec((None, k, tn), lambda i, j: (layer, 0, col(j)))
    return pl.pallas_call(
        _mm_kernel,
        out_shape=jax.ShapeDtypeStruct((m, n), out_dtype),
        grid=(m // tm, n // tn),
        in_specs=[pl.BlockSpec((tm, k), lambda i, j: (i, 0)), b_spec],
        out_specs=pl.BlockSpec((tm, tn), lambda i, j: (i, j)),
        compiler_params=_params("arbitrary", "arbitrary"),
        name=name,
    )(a, b)


def _kv_kernel(h_ref, w_ref, kn_ref, k_ref, v_ref):
    acc = _dot(h_ref[...], w_ref[...])
    for hd in range(N_KV_HEADS):
        sl = slice(hd * HEAD_DIM, (hd + 1) * HEAD_DIM)
        k_ref[:, sl] = _norm_head(acc[:, sl], kn_ref[...])
    v_ref[...] = acc[:, KV_W:]


def _kv_proj(h, w_in, layer, k_norm, tm=512):
    t = h.shape[0]
    return pl.pallas_call(
        _kv_kernel,
        out_shape=(jax.ShapeDtypeStruct((t, KV_W), F32), jax.ShapeDtypeStruct((t, KV_W), F32)),
        grid=(t // tm,),
        in_specs=[
            pl.BlockSpec((tm, D_MODEL), lambda i: (i, 0)),
            pl.BlockSpec((None, D_MODEL, W_KV), lambda i: (layer, 0, W_OFF_K // W_KV)),
            pl.BlockSpec((1, HEAD_DIM), lambda i: (0, 0)),
        ],
        out_specs=(pl.BlockSpec((tm, KV_W), lambda i: (i, 0)), pl.BlockSpec((tm, KV_W), lambda i: (i, 0))),
        compiler_params=_params("arbitrary"),
        name="kv_proj",
    )(h, w_in, k_norm.reshape(1, HEAD_DIM))


def _softmax_pv(s, v):
    m = jnp.max(s, axis=-1, keepdims=True)
    p = jnp.exp(s - m)
    l = jnp.sum(p, axis=-1, keepdims=True)
    return _dot(p.astype(BF16), v) / l


def _qk(q, k):
    return lax.dot_general(q, k, (((1,), (1,)), ((), ())), preferred_element_type=F32)


def _norm_head(q, g):
    return q * lax.rsqrt(jnp.mean(q * q, axis=-1, keepdims=True) + EPS) * g


def _rope(x, cos2, sin2):
    return x * cos2 + pltpu.roll(x, HEAD_DIM // 2, 1) * sin2


def _attn_ctx_kernel(q_ref, k_ref, v_ref, ag_ref, qn_ref, o_ref):
    seq = q_ref.shape[0]
    for hd in range(N_KV_HEADS):
        kv = slice(hd * HEAD_DIM, (hd + 1) * HEAD_DIM)
        cols = [slice((hd * KV_GROUP + g) * HEAD_DIM, (hd * KV_GROUP + g + 1) * HEAD_DIM) for g in range(KV_GROUP)]
        q = jnp.concatenate([(_norm_head(q_ref[:, c].astype(F32), qn_ref[...]) * ATTN_SCALE).astype(BF16)
                             for c in cols], axis=0)
        o = _softmax_pv(_qk(q, k_ref[:, kv].astype(BF16)), v_ref[:, kv].astype(BF16))
        for g, c in enumerate(cols):
            o_ref[:, c] = (o[g * seq:(g + 1) * seq] * _silu(ag_ref[:, c].astype(F32))).astype(BF16)


def _attention_ctx(proj, k, v, q_norm, n_seq, seq, out_rows):
    return pl.pallas_call(
        _attn_ctx_kernel,
        out_shape=jax.ShapeDtypeStruct((out_rows, ATTN_W), BF16),
        grid=(n_seq,),
        in_specs=[
            pl.BlockSpec((seq, ATTN_W), lambda b: (b, OFF_Q // ATTN_W)),
            pl.BlockSpec((seq, KV_W), lambda b: (b, 0)),
            pl.BlockSpec((seq, KV_W), lambda b: (b, 0)),
            pl.BlockSpec((seq, ATTN_W), lambda b: (b, OFF_AGATE // ATTN_W)),
            pl.BlockSpec((1, HEAD_DIM), lambda b: (0, 0)),
        ],
        out_specs=pl.BlockSpec((seq, ATTN_W), lambda b: (b, 0)),
        compiler_params=_params("arbitrary"),
        name="attn_ctx",
    )(proj, k, v, proj, q_norm.reshape(1, HEAD_DIM))


def _attn_lat_kernel(q_ref, k_ref, v_ref, ck_ref, cv_ref, ag_ref, qn_ref, cosq_ref, sinq_ref,
                     cosk_ref, sink_ref, o_ref, k_s, v_s, *, seq):
    @pl.when(pl.program_id(2) == 0)
    def _():
        k_s[:seq] = _rope(k_ref[...], cosk_ref[...], sink_ref[...]).astype(BF16)
        k_s[seq:] = ck_ref[...].astype(BF16)
        v_s[:seq] = v_ref[...].astype(BF16)
        v_s[seq:] = cv_ref[...].astype(BF16)

    k = k_s[...]
    v = v_s[...]
    for g in range(KV_GROUP):
        sl = slice(g * HEAD_DIM, (g + 1) * HEAD_DIM)
        q = _norm_head(q_ref[:, sl].astype(F32), qn_ref[...])
        q = _rope(q, cosq_ref[...], sinq_ref[...]) * ATTN_SCALE
        o = _softmax_pv(_qk(q.astype(BF16), k), v)
        o_ref[:, sl] = (o * _silu(ag_ref[:, sl].astype(F32))).astype(BF16)


def _attention_lat(buf, proj, k, v, cache_k, cache_v, layer, q_norm, cos2, sin2, n_ctx, n_seq, seq, tq=512):
    past = cache_k.shape[2]
    ck = cache_k.reshape(cache_k.shape[0], DEPTH, past, KV_W)
    cv = cache_v.reshape(cache_v.shape[0], DEPTH, past, KV_W)
    nq = seq // tq
    row0, seq0 = n_ctx // tq, n_ctx // seq
    in_specs = [
        pl.BlockSpec((tq, GROUP_W), lambda b, h, i: (row0 + b * nq + i, OFF_Q // GROUP_W + h)),
        pl.BlockSpec((seq, HEAD_DIM), lambda b, h, i: (seq0 + b, h)),
        pl.BlockSpec((seq, HEAD_DIM), lambda b, h, i: (seq0 + b, h)),
        pl.BlockSpec((None, None, past, HEAD_DIM), lambda b, h, i: (b, layer, 0, h)),
        pl.BlockSpec((None, None, past, HEAD_DIM), lambda b, h, i: (b, layer, 0, h)),
        pl.BlockSpec((tq, GROUP_W), lambda b, h, i: (row0 + b * nq + i, OFF_AGATE // GROUP_W + h)),
        pl.BlockSpec((1, HEAD_DIM), lambda b, h, i: (0, 0)),
        pl.BlockSpec((tq, HEAD_DIM), lambda b, h, i: (i, 0)),
        pl.BlockSpec((tq, HEAD_DIM), lambda b, h, i: (i, 0)),
        pl.BlockSpec((seq, HEAD_DIM), lambda b, h, i: (0, 0)),
        pl.BlockSpec((seq, HEAD_DIM), lambda b, h, i: (0, 0)),
    ]
    return _buffer_call(
        functools.partial(_attn_lat_kernel, seq=seq), buf, in_specs,
        (proj, k, v, ck, cv, proj, q_norm.reshape(1, HEAD_DIM), cos2, sin2, cos2, sin2),
        out_shape=jax.ShapeDtypeStruct(buf.shape, BF16),
        grid=(n_seq, N_KV_HEADS, nq),
        out_specs=pl.BlockSpec((tq, GROUP_W), lambda b, h, i: (row0 + b * nq + i, h)),
        scratch_shapes=[pltpu.VMEM((seq + past, HEAD_DIM), BF16), pltpu.VMEM((seq + past, HEAD_DIM), BF16)],
        compiler_params=_params("arbitrary", "arbitrary", "arbitrary"),
        name="attn_lat",
    )


def _rope_tables(seq):
    rows = seq // GRID_W
    row = jnp.repeat(jnp.arange(rows, dtype=F32), GRID_W)
    col = jnp.tile(jnp.arange(GRID_W, dtype=F32), rows)
    inv = ROPE_THETA ** (-jnp.arange(ROPE_PAIRS_AXIS, dtype=F32) / ROPE_PAIRS_AXIS)
    ang = jnp.concatenate([row[:, None] * inv, col[:, None] * inv], axis=-1)
    cos, sin = jnp.cos(ang), jnp.sin(ang)
    return jnp.concatenate([cos, cos], axis=-1), jnp.concatenate([-sin, sin], axis=-1)


def _fnet_kernel(u_ref, fg_ref, csc_ref, ml_ref, o_ref, t_s, *, seq, scale):
    @pl.when(pl.program_id(1) == 0)
    def _():
        for g in range(FNET_GROUPS):
            sl = slice(g * FNET_GROUP_W, (g + 1) * FNET_GROUP_W)
            t = _dot(u_ref[:, sl], csc_ref[...])
            t_s[:seq, sl] = t[:, :FNET_GROUP_W].astype(BF16)
            t_s[seq:, sl] = t[:, FNET_GROUP_W:].astype(BF16)

    y = _dot(ml_ref[...], t_s[...]) * scale
    o_ref[...] = (y * _silu(fg_ref[...].astype(F32))).astype(BF16)


def _dft_angle(i, j, n):
    return (2.0 * math.pi / n) * ((i[:, None] * j[None, :]) % n).astype(F32)


def _fnet_tables(seq):
    j = jnp.arange(FNET_GROUP_W, dtype=jnp.int32)
    ang = _dft_angle(j, j, FNET_GROUP_W)
    csc = jnp.concatenate([jnp.cos(ang), jnp.sin(ang)], axis=1).astype(BF16)
    t = jnp.arange(seq, dtype=jnp.int32)
    ang = _dft_angle(t, t, seq)
    ml = jnp.concatenate([jnp.cos(ang), -jnp.sin(ang)], axis=1).astype(BF16)
    return csc, ml


def _fnet(buf, proj, tables, row_start, n_seq, seq, out_rows, tr=256):
    csc, ml = tables
    nr = seq // tr
    s0, r0 = row_start // seq, row_start // tr
    in_specs = [
        pl.BlockSpec((seq, FNET_W), lambda b, r: (s0 + b, OFF_FIN // FNET_W)),
        pl.BlockSpec((tr, FNET_W), lambda b, r: (r0 + b * nr + r, OFF_FGATE // FNET_W)),
        pl.BlockSpec((FNET_GROUP_W, 2 * FNET_GROUP_W), lambda b, r: (0, 0)),
        pl.BlockSpec((tr, 2 * seq), lambda b, r: (r, 0)),
    ]
    return _buffer_call(
        functools.partial(_fnet_kernel, seq=seq, scale=1.0 / math.sqrt(seq * FNET_GROUP_W)), buf, in_specs,
        (proj, proj, csc, ml),
        out_shape=jax.ShapeDtypeStruct((out_rows, FNET_W), BF16),
        grid=(n_seq, nr),
        out_specs=pl.BlockSpec((tr, FNET_W), lambda b, r: (r0 + b * nr + r, 0)),
        scratch_shapes=[pltpu.VMEM((2 * seq, FNET_W), BF16)],
        compiler_params=_params("arbitrary", "arbitrary"),
        name=f"fnet_{seq}",
    )


def _short_kernel(x_ref, w_ref, b_ref, o_ref, *, seq):
    x = x_ref[...].astype(F32)
    n = x.shape[0]
    pos = lax.broadcasted_iota(jnp.int32, x.shape, 0) % seq
    prev = jnp.where(pos == 0, 0.0, pltpu.roll(x, 1, 0))
    nxt = jnp.where(pos == seq - 1, 0.0, pltpu.roll(x, n - 1, 0))
    y = b_ref[...] + prev * w_ref[0:1, :] + x * w_ref[1:2, :] + nxt * w_ref[2:3, :]
    o_ref[...] = y.astype(BF16)


def _short_conv(buf, proj, w, b, row_start, n_rows, seq, out_rows, tr=2048, tw=512):
    width = 3 * HYENA_W
    r0 = row_start // tr
    in_specs = [
        pl.BlockSpec((tr, tw), lambda s, c: (r0 + s, OFF_HV // tw + c)),
        pl.BlockSpec((HYENA_SHORT, tw), lambda s, c: (0, c)),
        pl.BlockSpec((1, tw), lambda s, c: (0, c)),
    ]
    return _buffer_call(
        functools.partial(_short_kernel, seq=seq), buf, in_specs, (proj, w, b.reshape(1, width)),
        out_shape=jax.ShapeDtypeStruct((out_rows, width), BF16),
        grid=(n_rows // tr, width // tw),
        out_specs=pl.BlockSpec((tr, tw), lambda s, c: (r0 + s, c)),
        compiler_params=_params("arbitrary", "arbitrary"),
        name=f"short_conv_{seq}",
    )


def _filt_kernel(feats_ref, t_ref, w1_ref, b1_ref, w2_ref, b2_ref, w3f_ref, w3b_ref, b3f_ref, b3b_ref,
                 fr_ref, dl_ref, of_ref, ob_ref):
    fr = fr_ref[...]
    h = jnp.sin(fr * (_dot_exact(feats_ref[...], w1_ref[...]) + b1_ref[...]))
    h = jnp.sin(fr * (_dot_exact(h, w2_ref[...]) + b2_ref[...]))
    decay = jnp.exp(-t_ref[...] * dl_ref[...])
    hf = (_dot_exact(h, w3f_ref[...]) + b3f_ref[...]) * decay
    hb = (_dot_exact(h, w3b_ref[...]) + b3b_ref[...]) * decay
    den = jnp.sum(jnp.abs(hf), axis=0, keepdims=True) + jnp.sum(jnp.abs(hb), axis=0, keepdims=True) + EPS
    of_ref[...] = (hf / den).astype(BF16)
    ob_ref[...] = (hb / den).astype(BF16)


def _hyena_filters(seq, w1, b1, w2, b2, w3, b3, freq, tw=256):
    t = jnp.arange(seq, dtype=F32)[:, None] / seq
    bands = jnp.arange(1, HYENA_BANDS + 1, dtype=F32)[None, :]
    feats = jnp.concatenate([t, jnp.cos(2 * math.pi * t * bands), jnp.sin(2 * math.pi * t * bands),
                             jnp.zeros((seq, HYENA_POS_PAD - HYENA_POS_DIM), F32)], axis=-1)
    w1p = jnp.concatenate([w1, jnp.zeros((HYENA_POS_PAD - HYENA_POS_DIM, HYENA_FFN_W), F32)], axis=0)
    deltas = jnp.abs(jnp.linspace(HYENA_MIN_DECAY, HYENA_MAX_DECAY, HYENA_W, dtype=F32))[None, :]
    nw = HYENA_W // tw
    small = lambda shape: pl.BlockSpec(shape, lambda o, c: (0, 0))
    out_sds = jax.ShapeDtypeStruct((seq, HYENA_ORDER * HYENA_W), BF16)
    fwd = lambda o, c: (0, (2 * o) * nw + c)
    bwd = lambda o, c: (0, (2 * o + 1) * nw + c)
    b3r = b3.reshape(1, -1)
    return pl.pallas_call(
        _filt_kernel,
        out_shape=(out_sds, out_sds),
        grid=(HYENA_ORDER, nw),
        in_specs=[
            small((seq, HYENA_POS_PAD)), small((seq, 1)),
            small((HYENA_POS_PAD, HYENA_FFN_W)), small((1, HYENA_FFN_W)),
            small((HYENA_FFN_W, HYENA_FFN_W)), small((1, HYENA_FFN_W)),
            pl.BlockSpec((HYENA_FFN_W, tw), fwd), pl.BlockSpec((HYENA_FFN_W, tw), bwd),
            pl.BlockSpec((1, tw), fwd), pl.BlockSpec((1, tw), bwd),
            small((1, HYENA_FFN_W)),
            pl.BlockSpec((1, tw), lambda o, c: (0, c)),
        ],
        out_specs=(pl.BlockSpec((seq, tw), lambda o, c: (0, o * nw + c)),
                   pl.BlockSpec((seq, tw), lambda o, c: (0, o * nw + c))),
        compiler_params=_params("arbitrary", "arbitrary"),
        name=f"hyena_filters_{seq}",
    )(feats, t, w1p, b1.reshape(1, -1), w2, b2.reshape(1, -1), w3, w3, b3r, b3r, freq.reshape(1, -1), deltas)


def _conv_tables(seq):
    n = 2 * seq
    i = jnp.arange(seq, dtype=jnp.int32)
    ang = _dft_angle(i, i, n)
    nyq = (1 - 2 * (i % 2)).astype(F32)
    mc = jnp.cos(ang)
    ms = jnp.where(i[:, None] == 0, nyq[None, :], jnp.sin(ang))
    wgt = jnp.where(i == 0, 1.0, 2.0)[None, :]
    ic = mc * wgt
    is_ = jnp.where(i[None, :] == 0, nyq[:, None], jnp.sin(ang) * wgt)
    return mc.astype(BF16), ms.astype(BF16), ic.astype(BF16), is_.astype(BF16)


def _lconv_kernel(*refs, seq, n_seq_blk, n_chunks, gated):
    refs = list(refs)
    z_ref, xm_ref = refs[:2]
    gate_ref = refs[2] if gated else None
    (bias_ref, mc_ref, ms_ref, ic_ref, is_ref, ucf_ref, ucb_ref, usf_ref, usb_ref, o_ref) = refs[2 + gated:12 + gated]
    acc_ref = refs[12 + gated] if n_chunks > 1 else None
    f = pl.program_id(2)
    usf, usb = usf_ref[...], usb_ref[...]
    gr = ucf_ref[...] + ucb_ref[...]
    gi = usf - usb
    packed = jnp.logical_and(lax.broadcasted_iota(jnp.int32, gr.shape, 0) == 0, f == 0)
    gi = jnp.where(packed, 0.0, gi)
    gn = jnp.where(packed, usf + usb, gr)

    def finish(acc, rows):
        z = z_ref[rows, :].astype(F32)
        y = (acc * (1.0 / (2 * seq)) + z * bias_ref[...]) * xm_ref[rows, :].astype(F32)
        if gated:
            y = y * _silu(gate_ref[rows, :].astype(F32))
        o_ref[rows, :] = y.astype(BF16)

    for s in range(n_seq_blk):
        rows = slice(s * seq, (s + 1) * seq)
        z = z_ref[rows, :]
        zr = _dot(mc_ref[...], z)
        zi = _dot(ms_ref[...], z)
        pr = (zr * gr - zi * gi).astype(BF16)
        pi = (zr * gi + zi * gn).astype(BF16)
        contrib = _dot(ic_ref[...], pr) + _dot(is_ref[...], pi)
        if n_chunks == 1:
            finish(contrib, rows)
        else:
            @pl.when(f == 0)
            def _():
                acc_ref[rows, :] = contrib

            @pl.when(f > 0)
            def _():
                acc_ref[rows, :] += contrib

            @pl.when(f == n_chunks - 1)
            def _():
                finish(acc_ref[rows, :], rows)


def _long_conv(buf, z, z_col, xm, xm_col, gate, bias, idx, order, tables, spec_f, spec_b,
               row_start, n_seq, seq, out_rows, tw, fc, seq_blk):
    mc, ms, ic, is_ = tables
    nw, nf = HYENA_W // tw, seq // fc
    tr = seq_blk * seq
    r0 = row_start // tr
    gated = gate is not None
    tok = lambda col: pl.BlockSpec((tr, tw), lambda s, w, f: (r0 + s, col * nw + w))
    spec_c = pl.BlockSpec((fc, tw), lambda s, w, f: (f, order * nw + w))
    spec_s = pl.BlockSpec((fc, tw), lambda s, w, f: (nf + f, order * nw + w))
    in_specs = [tok(z_col), tok(xm_col)]
    args = [z, xm]
    if gated:
        in_specs.append(tok(OFF_HGATE // HYENA_W))
        args.append(gate)
    in_specs += [
        pl.BlockSpec((None, 1, tw), lambda s, w, f: (idx, 0, w)),
        pl.BlockSpec((fc, seq), lambda s, w, f: (f, 0)),
        pl.BlockSpec((fc, seq), lambda s, w, f: (f, 0)),
        pl.BlockSpec((seq, fc), lambda s, w, f: (0, f)),
        pl.BlockSpec((seq, fc), lambda s, w, f: (0, f)),
        spec_c, spec_c, spec_s, spec_s,
    ]
    args += [bias, mc, ms, ic, is_, spec_f, spec_b, spec_f, spec_b]
    return _buffer_call(
        functools.partial(_lconv_kernel, seq=seq, n_seq_blk=seq_blk, n_chunks=nf, gated=gated), buf, in_specs, args,
        out_shape=jax.ShapeDtypeStruct((out_rows, HYENA_W), BF16),
        grid=(n_seq // seq_blk, nw, nf),
        out_specs=pl.BlockSpec((tr, tw), lambda s, w, f: (r0 + s, w)),
        scratch_shapes=[pltpu.VMEM((tr, tw), F32)] if nf > 1 else [],
        compiler_params=_params("arbitrary", "arbitrary", "arbitrary"),
        name=f"long_conv_{seq}_{order}",
    )


def _merge_kernel(a_ref, f_ref, h_ref, wa_ref, wf_ref, wh_ref, ga_ref, gf_ref, gh_ref, o_ref):
    m = jax.nn.sigmoid(ga_ref[...].astype(F32)) * _dot(a_ref[...], wa_ref[...])
    m = m + jax.nn.sigmoid(gf_ref[...].astype(F32)) * _dot(f_ref[...], wf_ref[...])
    m = m + jax.nn.sigmoid(gh_ref[...].astype(F32)) * _dot(h_ref[...], wh_ref[...])
    o_ref[...] = m.astype(BF16)


def _merge(attn, fnet, hy, w_a, w_f, w_h, layer, proj, tm=1024, tn=512):
    t = attn.shape[0]
    act = lambda w: pl.BlockSpec((tm, w), lambda i, j: (i, 0))
    wgt = lambda k: pl.BlockSpec((None, k, tn), lambda i, j: (layer, 0, j))
    gate = lambda off: pl.BlockSpec((tm, tn), lambda i, j: (i, off // tn + j))
    return pl.pallas_call(
        _merge_kernel,
        out_shape=jax.ShapeDtypeStruct((t, D_MODEL), BF16),
        grid=(t // tm, D_MODEL // tn),
        in_specs=[act(ATTN_W), act(FNET_W), act(HYENA_W), wgt(ATTN_W), wgt(FNET_W), wgt(HYENA_W),
                  gate(OFF_GA), gate(OFF_GF), gate(OFF_GH)],
        out_specs=pl.BlockSpec((tm, tn), lambda i, j: (i, j)),
        compiler_params=_params("arbitrary", "arbitrary"),
        name="merge",
    )(attn, fnet, hy, w_a, w_f, w_h, proj, proj, proj)


def kernel(x_prompt, x_sample, cache_k, cache_v, c, c_ctx, w_mod, b_mod, g_pre, w_in, q_norm, k_norm, hy_short_w, hy_short_b, hy_ffn_w1, hy_ffn_b1, hy_ffn_w2, hy_ffn_b2, hy_ffn_w3, hy_ffn_b3, hy_sin_freq, hy_bias, w_attn_o, w_fnet_o, w_hy_o, w_out, g_post):
    nb, seq, _ = x_prompt.shape
    db, dseq, _ = x_sample.shape
    assert 1 + db <= MOD_ROWS
    n_ctx, n_lat = nb * seq, db * dseq
    t = n_ctx + n_lat
    ctx_row = lambda i: 0
    lat_row = lambda i: 1 + i // (dseq // ROW_TILE)

    cond = jnp.concatenate([c_ctx[None, :], c, jnp.zeros((MOD_ROWS - 1 - db, D_MODEL), F32)], axis=0)
    mod = _modulation(cond, w_mod, b_mod)

    w_in_b, w_out_b = w_in.astype(BF16), w_out.astype(BF16)
    w_a_b, w_f_b, w_h_b = w_attn_o.astype(BF16), w_fnet_o.astype(BF16), w_hy_o.astype(BF16)
    cos2, sin2 = _rope_tables(dseq)
    fnet_tables = {s: _fnet_tables(s) for s in (seq, dseq)}
    conv_tables = {s: _conv_tables(s) for s in (seq, dseq)}
    bias = hy_bias.reshape(DEPTH * HYENA_ORDER, 1, HYENA_W)
    conv_tiles = {seq: (HYENA_W, seq, 4), dseq: (512, 256, 1)}
    skip_kv = lambda j, tn=512: jnp.where(j >= W_OFF_K // tn, j + W_KV // tn, j)

    xs = ((x_prompt.reshape(n_ctx, D_MODEL), 0), (x_sample.reshape(n_lat, D_MODEL), 0))
    new_k, new_v = [], []
    for l in range(DEPTH):
        mod3 = mod[l].reshape(MOD_ROWS, 1, 3 * D_MODEL)
        (x_c, c0), (x_l, l0) = xs
        h = _prenorm(None, x_c, c0, g_pre[l], mod3, ctx_row, 0, n_ctx, t)
        h = _prenorm(h, x_l, l0, g_pre[l], mod3, lat_row, n_ctx, n_lat, t)
        proj = _matmul(h, w_in_b, BF16, 1024, 512, "in_proj", layer=l, n_out=PROJ_W, col_map=skip_kv)
        k, v = _kv_proj(h, w_in_b, l, k_norm[l])
        new_k.append(k[:n_ctx].reshape(nb, seq, N_KV_HEADS, HEAD_DIM))
        new_v.append(v[:n_ctx].reshape(nb, seq, N_KV_HEADS, HEAD_DIM))

        attn = _attention_ctx(proj, k, v, q_norm[l], nb, seq, t)
        attn = _attention_lat(attn, proj, k, v, cache_k, cache_v, l, q_norm[l], cos2, sin2, n_ctx, db, dseq)

        fnet = u = z1 = hy = None
        for row_start, n_seq, s in ((0, nb, seq), (n_ctx, db, dseq)):
            fnet = _fnet(fnet, proj, fnet_tables[s], row_start, n_seq, s, t)
            u = _short_conv(u, proj, hy_short_w[l], hy_short_b[l], row_start, n_seq * s, s, t)
            filt_f, filt_b = _hyena_filters(s, hy_ffn_w1[l], hy_ffn_b1[l], hy_ffn_w2[l], hy_ffn_b2[l],
                                            hy_ffn_w3[l], hy_ffn_b3[l], hy_sin_freq[l])
            fwd = jnp.concatenate(conv_tables[s][:2], axis=0)
            spec_f = _matmul(fwd, filt_f, F32, 512, 512, f"filter_spectrum_fwd_{s}")
            spec_b = _matmul(fwd, filt_b, F32, 512, 512, f"filter_spectrum_bwd_{s}")
            tw, fc, blk = conv_tiles[s]
            z1 = _long_conv(z1, u, 0, u, 1, None, bias, l * HYENA_ORDER, 0, conv_tables[s],
                            spec_f, spec_b, row_start, n_seq, s, t, tw, fc, blk)
            hy = _long_conv(hy, z1, 0, u, 2, proj, bias, l * HYENA_ORDER + 1, 1, conv_tables[s],
                            spec_f, spec_b, row_start, n_seq, s, t, tw, fc, blk)

        merged = _merge(attn, fnet, hy, w_a_b, w_f_b, w_h_b, l, proj)
        r = _matmul(merged, w_out_b, BF16, 1024, 512, "out_proj", layer=l)
        if l < DEPTH - 1:
            x_new = _post(None, x_c, c0, r, 0, g_post[l], mod3, ctx_row, 0, n_ctx, t)
            x_new = _post(x_new, x_l, l0, r, n_ctx, g_post[l], mod3, lat_row, n_ctx, n_lat, t)
            xs = ((x_new, 0), (x_new, n_ctx))
        else:
            y_prompt = _post(None, x_c, c0, r, 0, g_post[l], mod3, ctx_row, 0, n_ctx, n_ctx)
            y_sample = _post(None, x_l, l0, r, n_ctx, g_post[l], mod3, lat_row, 0, n_lat, n_lat)

    return (y_prompt.reshape(nb, seq, D_MODEL), y_sample.reshape(db, dseq, D_MODEL),
            jnp.stack(new_k, axis=1), jnp.stack(new_v, axis=1))
```

```python
import functools
import math

import jax
import jax.numpy as jnp
from jax import lax
from jax.experimental import pallas as pl
from jax.experimental.pallas import tpu as pltpu

F32 = jnp.float32
BF16 = jnp.bfloat16

D_MODEL = 4096
DEPTH = 2
GRID_W = 64
HEAD_DIM = 128
N_HEADS = 16
N_KV_HEADS = 4
KV_GROUP = N_HEADS // N_KV_HEADS
ATTN_W = N_HEADS * HEAD_DIM
KV_W = N_KV_HEADS * HEAD_DIM
GROUP_W = KV_GROUP * HEAD_DIM
ATTN_SCALE = HEAD_DIM ** -0.5
ROPE_THETA = 10000.0
ROPE_PAIRS_AXIS = HEAD_DIM // 4
FNET_W = D_MODEL // 4
FNET_GROUPS = 4
FNET_GROUP_W = FNET_W // FNET_GROUPS
HYENA_W = D_MODEL // 4
HYENA_ORDER = 2
HYENA_SHORT = 3
HYENA_BANDS = 16
HYENA_POS_DIM = 1 + 2 * HYENA_BANDS
HYENA_POS_PAD = 128
HYENA_FFN_W = 64
HYENA_MIN_DECAY = math.log(1e-2) / 0.3
HYENA_MAX_DECAY = math.log(1e-2) / 1.5
EPS = 1e-6

W_OFF_K = ATTN_W
W_KV = 2 * KV_W
OFF_Q = 0
OFF_AGATE = OFF_Q + ATTN_W
OFF_FIN = OFF_AGATE + ATTN_W
OFF_FGATE = OFF_FIN + FNET_W
OFF_HV = OFF_FGATE + FNET_W
OFF_HX1 = OFF_HV + HYENA_W
OFF_HX2 = OFF_HX1 + HYENA_W
OFF_HGATE = OFF_HX2 + HYENA_W
OFF_GA = OFF_HGATE + HYENA_W
OFF_GF = OFF_GA + D_MODEL
OFF_GH = OFF_GF + D_MODEL
PROJ_W = OFF_GH + D_MODEL

MOD_ROWS = 8
ROW_TILE = 256
VMEM_LIMIT = 56 * 1024 * 1024


def _params(*sem):
    return pltpu.CompilerParams(dimension_semantics=sem, vmem_limit_bytes=VMEM_LIMIT)


def _silu(x):
    return x * jax.nn.sigmoid(x)


def _dot(a, b):
    return jnp.dot(a, b, preferred_element_type=F32)


def _dot_exact(a, b):
    return jnp.dot(a, b, preferred_element_type=F32, precision=lax.Precision.HIGHEST)


def _completing(kern):
    def wrapped(_buf_ref, *refs):
        kern(*refs)
    return wrapped


def _buffer_call(kern, buf, in_specs, args, **kw):
    if buf is None:
        return pl.pallas_call(kern, in_specs=in_specs, **kw)(*args)
    return pl.pallas_call(_completing(kern), in_specs=[pl.BlockSpec(memory_space=pl.ANY)] + in_specs,
                          input_output_aliases={0: 0}, **kw)(buf, *args)


def _mod_kernel(c_ref, w_ref, b_ref, o_ref):
    s = _silu(c_ref[...]).astype(BF16)
    o_ref[0] = _dot(s, w_ref[0].astype(BF16)) + b_ref[0]


def _modulation(cond, w_mod, b_mod, tn=512):
    n = w_mod.shape[2]
    return pl.pallas_call(
        _mod_kernel,
        out_shape=jax.ShapeDtypeStruct((DEPTH, MOD_ROWS, n), F32),
        grid=(DEPTH, n // tn),
        in_specs=[
            pl.BlockSpec((MOD_ROWS, D_MODEL), lambda l, j: (0, 0)),
            pl.BlockSpec((1, D_MODEL, tn), lambda l, j: (l, 0, j)),
            pl.BlockSpec((1, 1, tn), lambda l, j: (l, 0, j)),
        ],
        out_specs=pl.BlockSpec((1, MOD_ROWS, tn), lambda l, j: (l, 0, j)),
        compiler_params=_params("arbitrary", "arbitrary"),
        name="modulation",
    )(cond, w_mod, b_mod.reshape(DEPTH, 1, n))


def _prenorm_kernel(x_ref, g_ref, shift_ref, scale_ref, o_ref):
    x = x_ref[...]
    y = x * lax.rsqrt(jnp.mean(x * x, axis=-1, keepdims=True) + EPS) * g_ref[...]
    o_ref[...] = (y * (1.0 + scale_ref[0]) + shift_ref[0]).astype(BF16)


def _prenorm(buf, x, in_row0, g, mod3, mod_row, out_row0, n_rows, out_rows):
    tm = ROW_TILE
    i0, o0 = in_row0 // tm, out_row0 // tm
    in_specs = [
        pl.BlockSpec((tm, D_MODEL), lambda i: (i0 + i, 0)),
        pl.BlockSpec((1, D_MODEL), lambda i: (0, 0)),
        pl.BlockSpec((1, 1, D_MODEL), lambda i: (mod_row(i), 0, 0)),
        pl.BlockSpec((1, 1, D_MODEL), lambda i: (mod_row(i), 0, 1)),
    ]
    return _buffer_call(
        _prenorm_kernel, buf, in_specs, (x, g.reshape(1, D_MODEL), mod3, mod3),
        out_shape=jax.ShapeDtypeStruct((out_rows, D_MODEL), BF16),
        grid=(n_rows // tm,),
        out_specs=pl.BlockSpec((tm, D_MODEL), lambda i: (o0 + i, 0)),
        compiler_params=_params("arbitrary"),
        name="prenorm",
    )


def _post_kernel(x_ref, r_ref, g_ref, gate_ref, o_ref):
    r = r_ref[...].astype(F32)
    y = r * lax.rsqrt(jnp.mean(r * r, axis=-1, keepdims=True) + EPS) * g_ref[...]
    o_ref[...] = x_ref[...] + gate_ref[0] * y


def _post(buf, x, in_row0, r, r_row0, g, mod3, mod_row, out_row0, n_rows, out_rows):
    tm = ROW_TILE
    i0, r0, o0 = in_row0 // tm, r_row0 // tm, out_row0 // tm
    in_specs = [
        pl.BlockSpec((tm, D_MODEL), lambda i: (i0 + i, 0)),
        pl.BlockSpec((tm, D_MODEL), lambda i: (r0 + i, 0)),
        pl.BlockSpec((1, D_MODEL), lambda i: (0, 0)),
        pl.BlockSpec((1, 1, D_MODEL), lambda i: (mod_row(i), 0, 2)),
    ]
    return _buffer_call(
        _post_kernel, buf, in_specs, (x, r, g.reshape(1, D_MODEL), mod3),
        out_shape=jax.ShapeDtypeStruct((out_rows, D_MODEL), F32),
        grid=(n_rows // tm,),
        out_specs=pl.BlockSpec((tm, D_MODEL), lambda i: (o0 + i, 0)),
        compiler_params=_params("arbitrary"),
        name="post",
    )


def _mm_kernel(a_ref, b_ref, o_ref):
    o_ref[...] = _dot(a_ref[...], b_ref[...]).astype(o_ref.dtype)


def _matmul(a, b, out_dtype, tm, tn, name):
    m, k = a.shape
    n = b.shape[1]
    return pl.pallas_call(
        _mm_kernel,
        out_shape=jax.ShapeDtypeStruct((m, n), out_dtype),
        grid=(m // tm, n // tn),
        in_specs=[pl.BlockSpec((tm, k), lambda i, j: (i, 0)), pl.BlockSpec((k, tn), lambda i, j: (0, j))],
        out_specs=pl.BlockSpec((tm, tn), lambda i, j: (i, j)),
        compiler_params=_params("arbitrary", "arbitrary"),
        name=name,
    )(a, b)


def _mm_wcast_kernel(a_ref, w_ref, o_ref, wb_ref):
    @pl.when(pl.program_id(1) == 0)
    def _():
        wb_ref[...] = w_ref[...].astype(BF16)

    o_ref[...] = _dot(a_ref[...], wb_ref[...]).astype(o_ref.dtype)


def _matmul_wcast(a, w, layer, out_dtype, tm, tn, name, n_out=None, col_map=None):
    m, k = a.shape
    n = n_out or w.shape[-1]
    col = col_map or (lambda j: j)
    return pl.pallas_call(
        _mm_wcast_kernel,
        out_shape=jax.ShapeDtypeStruct((m, n), out_dtype),
        grid=(n // tn, m // tm),
        in_specs=[
            pl.BlockSpec((tm, k), lambda j, i: (i, 0)),
            pl.BlockSpec((None, k, tn), lambda j, i: (layer, 0, col(j))),
        ],
        out_specs=pl.BlockSpec((tm, tn), lambda j, i: (i, j)),
        scratch_shapes=[pltpu.VMEM((k, tn), BF16)],
        compiler_params=_params("arbitrary", "arbitrary"),
        name=name,
    )(a, w)


def _kv_kernel(h_ref, w_ref, kn_ref, k_ref, v_ref):
    acc = _dot(h_ref[...], w_ref[...])
    for hd in range(N_KV_HEADS):
        sl = slice(hd * HEAD_DIM, (hd + 1) * HEAD_DIM)
        k_ref[:, sl] = _norm_head(acc[:, sl], kn_ref[...])
    v_ref[...] = acc[:, KV_W:]


def _kv_proj(h, w_kv, layer, k_norm, tm=512):
    t = h.shape[0]
    return pl.pallas_call(
        _kv_kernel,
        out_shape=(jax.ShapeDtypeStruct((t, KV_W), F32), jax.ShapeDtypeStruct((t, KV_W), F32)),
        grid=(t // tm,),
        in_specs=[
            pl.BlockSpec((tm, D_MODEL), lambda i: (i, 0)),
            pl.BlockSpec((None, D_MODEL, W_KV), lambda i: (layer, 0, 0)),
            pl.BlockSpec((1, HEAD_DIM), lambda i: (0, 0)),
        ],
        out_specs=(pl.BlockSpec((tm, KV_W), lambda i: (i, 0)), pl.BlockSpec((tm, KV_W), lambda i: (i, 0))),
        compiler_params=_params("arbitrary"),
        name="kv_proj",
    )(h, w_kv, k_norm.reshape(1, HEAD_DIM))


def _softmax_pv(s, v):
    m = jnp.max(s, axis=-1, keepdims=True)
    p = jnp.exp(s - m)
    l = jnp.sum(p, axis=-1, keepdims=True)
    return _dot(p.astype(BF16), v) / l


def _qk(q, k):
    return lax.dot_general(q, k, (((1,), (1,)), ((), ())), preferred_element_type=F32)


def _norm_head(q, g):
    return q * lax.rsqrt(jnp.mean(q * q, axis=-1, keepdims=True) + EPS) * g


def _rope(x, cos2, sin2):
    return x * cos2 + pltpu.roll(x, HEAD_DIM // 2, 1) * sin2


def _attn_ctx_kernel(q_ref, k_ref, v_ref, ag_ref, qn_ref, o_ref):
    seq = q_ref.shape[0]
    for hd in range(N_KV_HEADS):
        kv = slice(hd * HEAD_DIM, (hd + 1) * HEAD_DIM)
        cols = [slice((hd * KV_GROUP + g) * HEAD_DIM, (hd * KV_GROUP + g + 1) * HEAD_DIM) for g in range(KV_GROUP)]
        q = jnp.concatenate([(_norm_head(q_ref[:, c].astype(F32), qn_ref[...]) * ATTN_SCALE).astype(BF16)
                             for c in cols], axis=0)
        o = _softmax_pv(_qk(q, k_ref[:, kv].astype(BF16)), v_ref[:, kv].astype(BF16))
        for g, c in enumerate(cols):
            o_ref[:, c] = (o[g * seq:(g + 1) * seq] * _silu(ag_ref[:, c].astype(F32))).astype(BF16)


def _attention_ctx(proj, k, v, q_norm, n_seq, seq, out_rows):
    return pl.pallas_call(
        _attn_ctx_kernel,
        out_shape=jax.ShapeDtypeStruct((out_rows, ATTN_W), BF16),
        grid=(n_seq,),
        in_specs=[
            pl.BlockSpec((seq, ATTN_W), lambda b: (b, OFF_Q // ATTN_W)),
            pl.BlockSpec((seq, KV_W), lambda b: (b, 0)),
            pl.BlockSpec((seq, KV_W), lambda b: (b, 0)),
            pl.BlockSpec((seq, ATTN_W), lambda b: (b, OFF_AGATE // ATTN_W)),
            pl.BlockSpec((1, HEAD_DIM), lambda b: (0, 0)),
        ],
        out_specs=pl.BlockSpec((seq, ATTN_W), lambda b: (b, 0)),
        compiler_params=_params("arbitrary"),
        name="attn_ctx",
    )(proj, k, v, proj, q_norm.reshape(1, HEAD_DIM))


def _attn_lat_kernel(q_ref, k_ref, v_ref, ck_ref, cv_ref, ag_ref, qn_ref, cosq_ref, sinq_ref,
                     cosk_ref, sink_ref, o_ref, k_s, v_s, *, seq):
    @pl.when(pl.program_id(2) == 0)
    def _():
        k_s[:seq] = _rope(k_ref[...], cosk_ref[...], sink_ref[...]).astype(BF16)
        k_s[seq:] = ck_ref[...].astype(BF16)
        v_s[:seq] = v_ref[...].astype(BF16)
        v_s[seq:] = cv_ref[...].astype(BF16)

    k = k_s[...]
    v = v_s[...]
    for g in range(KV_GROUP):
        sl = slice(g * HEAD_DIM, (g + 1) * HEAD_DIM)
        q = _norm_head(q_ref[:, sl].astype(F32), qn_ref[...])
        q = _rope(q, cosq_ref[...], sinq_ref[...]) * ATTN_SCALE
        o = _softmax_pv(_qk(q.astype(BF16), k), v)
        o_ref[:, sl] = (o * _silu(ag_ref[:, sl].astype(F32))).astype(BF16)


def _attention_lat(buf, proj, k, v, cache_k, cache_v, layer, q_norm, cos2, sin2, n_ctx, n_seq, seq, tq=512):
    past = cache_k.shape[2]
    ck = cache_k.reshape(cache_k.shape[0], DEPTH, past, KV_W)
    cv = cache_v.reshape(cache_v.shape[0], DEPTH, past, KV_W)
    nq = seq // tq
    row0, seq0 = n_ctx // tq, n_ctx // seq
    in_specs = [
        pl.BlockSpec((tq, GROUP_W), lambda b, h, i: (row0 + b * nq + i, OFF_Q // GROUP_W + h)),
        pl.BlockSpec((seq, HEAD_DIM), lambda b, h, i: (seq0 + b, h)),
        pl.BlockSpec((seq, HEAD_DIM), lambda b, h, i: (seq0 + b, h)),
        pl.BlockSpec((None, None, past, HEAD_DIM), lambda b, h, i: (b, layer, 0, h)),
        pl.BlockSpec((None, None, past, HEAD_DIM), lambda b, h, i: (b, layer, 0, h)),
        pl.BlockSpec((tq, GROUP_W), lambda b, h, i: (row0 + b * nq + i, OFF_AGATE // GROUP_W + h)),
        pl.BlockSpec((1, HEAD_DIM), lambda b, h, i: (0, 0)),
        pl.BlockSpec((tq, HEAD_DIM), lambda b, h, i: (i, 0)),
        pl.BlockSpec((tq, HEAD_DIM), lambda b, h, i: (i, 0)),
        pl.BlockSpec((seq, HEAD_DIM), lambda b, h, i: (0, 0)),
        pl.BlockSpec((seq, HEAD_DIM), lambda b, h, i: (0, 0)),
    ]
    return _buffer_call(
        functools.partial(_attn_lat_kernel, seq=seq), buf, in_specs,
        (proj, k, v, ck, cv, proj, q_norm.reshape(1, HEAD_DIM), cos2, sin2, cos2, sin2),
        out_shape=jax.ShapeDtypeStruct(buf.shape, BF16),
        grid=(n_seq, N_KV_HEADS, nq),
        out_specs=pl.BlockSpec((tq, GROUP_W), lambda b, h, i: (row0 + b * nq + i, h)),
        scratch_shapes=[pltpu.VMEM((seq + past, HEAD_DIM), BF16), pltpu.VMEM((seq + past, HEAD_DIM), BF16)],
        compiler_params=_params("arbitrary", "arbitrary", "arbitrary"),
        name="attn_lat",
    )


def _rope_tables(seq):
    rows = seq // GRID_W
    row = jnp.repeat(jnp.arange(rows, dtype=F32), GRID_W)
    col = jnp.tile(jnp.arange(GRID_W, dtype=F32), rows)
    inv = ROPE_THETA ** (-jnp.arange(ROPE_PAIRS_AXIS, dtype=F32) / ROPE_PAIRS_AXIS)
    ang = jnp.concatenate([row[:, None] * inv, col[:, None] * inv], axis=-1)
    cos, sin = jnp.cos(ang), jnp.sin(ang)
    return jnp.concatenate([cos, cos], axis=-1), jnp.concatenate([-sin, sin], axis=-1)


def _fnet_kernel(u_ref, fg_ref, csc_ref, ml_ref, o_ref, t_s, *, seq, scale):
    @pl.when(pl.program_id(1) == 0)
    def _():
        for g in range(FNET_GROUPS):
            sl = slice(g * FNET_GROUP_W, (g + 1) * FNET_GROUP_W)
            t = _dot(u_ref[:, sl], csc_ref[...])
            t_s[:seq, sl] = t[:, :FNET_GROUP_W].astype(BF16)
            t_s[seq:, sl] = t[:, FNET_GROUP_W:].astype(BF16)

    y = _dot(ml_ref[...], t_s[...]) * scale
    o_ref[...] = (y * _silu(fg_ref[...].astype(F32))).astype(BF16)


def _dft_angle(i, j, n):
    return (2.0 * math.pi / n) * ((i[:, None] * j[None, :]) % n).astype(F32)


def _fnet_tables(seq):
    j = jnp.arange(FNET_GROUP_W, dtype=jnp.int32)
    ang = _dft_angle(j, j, FNET_GROUP_W)
    csc = jnp.concatenate([jnp.cos(ang), jnp.sin(ang)], axis=1).astype(BF16)
    t = jnp.arange(seq, dtype=jnp.int32)
    ang = _dft_angle(t, t, seq)
    ml = jnp.concatenate([jnp.cos(ang), -jnp.sin(ang)], axis=1).astype(BF16)
    return csc, ml


def _fnet(buf, proj, tables, row_start, n_seq, seq, out_rows, tr=256):
    csc, ml = tables
    nr = seq // tr
    s0, r0 = row_start // seq, row_start // tr
    in_specs = [
        pl.BlockSpec((seq, FNET_W), lambda b, r: (s0 + b, OFF_FIN // FNET_W)),
        pl.BlockSpec((tr, FNET_W), lambda b, r: (r0 + b * nr + r, OFF_FGATE // FNET_W)),
        pl.BlockSpec((FNET_GROUP_W, 2 * FNET_GROUP_W), lambda b, r: (0, 0)),
        pl.BlockSpec((tr, 2 * seq), lambda b, r: (r, 0)),
    ]
    return _buffer_call(
        functools.partial(_fnet_kernel, seq=seq, scale=1.0 / math.sqrt(seq * FNET_GROUP_W)), buf, in_specs,
        (proj, proj, csc, ml),
        out_shape=jax.ShapeDtypeStruct((out_rows, FNET_W), BF16),
        grid=(n_seq, nr),
        out_specs=pl.BlockSpec((tr, FNET_W), lambda b, r: (r0 + b * nr + r, 0)),
        scratch_shapes=[pltpu.VMEM((2 * seq, FNET_W), BF16)],
        compiler_params=_params("arbitrary", "arbitrary"),
        name=f"fnet_{seq}",
    )


def _short_kernel(x_ref, w_ref, b_ref, o_ref, *, seq):
    x = x_ref[...].astype(F32)
    n = x.shape[0]
    pos = lax.broadcasted_iota(jnp.int32, x.shape, 0) % seq
    prev = jnp.where(pos == 0, 0.0, pltpu.roll(x, 1, 0))
    nxt = jnp.where(pos == seq - 1, 0.0, pltpu.roll(x, n - 1, 0))
    y = b_ref[...] + prev * w_ref[0:1, :] + x * w_ref[1:2, :] + nxt * w_ref[2:3, :]
    o_ref[...] = y.astype(BF16)


def _short_conv(buf, proj, w, b, row_start, n_rows, seq, out_rows, tr=2048, tw=512):
    width = 3 * HYENA_W
    r0 = row_start // tr
    in_specs = [
        pl.BlockSpec((tr, tw), lambda s, c: (r0 + s, OFF_HV // tw + c)),
        pl.BlockSpec((HYENA_SHORT, tw), lambda s, c: (0, c)),
        pl.BlockSpec((1, tw), lambda s, c: (0, c)),
    ]
    return _buffer_call(
        functools.partial(_short_kernel, seq=seq), buf, in_specs, (proj, w, b.reshape(1, width)),
        out_shape=jax.ShapeDtypeStruct((out_rows, width), BF16),
        grid=(n_rows // tr, width // tw),
        out_specs=pl.BlockSpec((tr, tw), lambda s, c: (r0 + s, c)),
        compiler_params=_params("arbitrary", "arbitrary"),
        name=f"short_conv_{seq}",
    )


def _filt_mlp_kernel(feats_ref, w1_ref, b1_ref, w2_ref, b2_ref, fr_ref, o_ref):
    fr = fr_ref[...]
    h = jnp.sin(fr * (_dot_exact(feats_ref[...], w1_ref[...]) + b1_ref[...]))
    o_ref[...] = jnp.sin(fr * (_dot_exact(h, w2_ref[...]) + b2_ref[...]))


def _filt_kernel(h_ref, t_ref, w3f_ref, w3b_ref, b3f_ref, b3b_ref, dl_ref, of_ref, ob_ref):
    h = h_ref[...]
    decay = jnp.exp(-t_ref[...] * dl_ref[...])
    hf = (_dot_exact(h, w3f_ref[...]) + b3f_ref[...]) * decay
    hb = (_dot_exact(h, w3b_ref[...]) + b3b_ref[...]) * decay
    den = jnp.sum(jnp.abs(hf), axis=0, keepdims=True) + jnp.sum(jnp.abs(hb), axis=0, keepdims=True) + EPS
    of_ref[...] = (hf / den).astype(BF16)
    ob_ref[...] = (hb / den).astype(BF16)


def _hyena_filters(seq, w1, b1, w2, b2, w3, b3, freq, tw=512):
    t = jnp.arange(seq, dtype=F32)[:, None] / seq
    bands = jnp.arange(1, HYENA_BANDS + 1, dtype=F32)[None, :]
    feats = jnp.concatenate([t, jnp.cos(2 * math.pi * t * bands), jnp.sin(2 * math.pi * t * bands),
                             jnp.zeros((seq, HYENA_POS_PAD - HYENA_POS_DIM), F32)], axis=-1)
    w1p = jnp.concatenate([w1, jnp.zeros((HYENA_POS_PAD - HYENA_POS_DIM, HYENA_FFN_W), F32)], axis=0)
    deltas = jnp.abs(jnp.linspace(HYENA_MIN_DECAY, HYENA_MAX_DECAY, HYENA_W, dtype=F32))[None, :]
    hidden = pl.pallas_call(
        _filt_mlp_kernel,
        out_shape=jax.ShapeDtypeStruct((seq, HYENA_FFN_W), F32),
        name=f"hyena_filter_mlp_{seq}",
    )(feats, w1p, b1.reshape(1, -1), w2, b2.reshape(1, -1), freq.reshape(1, -1))
    nw = HYENA_W // tw
    small = lambda shape: pl.BlockSpec(shape, lambda o, c: (0, 0))
    out_sds = jax.ShapeDtypeStruct((seq, HYENA_ORDER * HYENA_W), BF16)
    fwd = lambda o, c: (0, (2 * o) * nw + c)
    bwd = lambda o, c: (0, (2 * o + 1) * nw + c)
    b3r = b3.reshape(1, -1)
    return pl.pallas_call(
        _filt_kernel,
        out_shape=(out_sds, out_sds),
        grid=(HYENA_ORDER, nw),
        in_specs=[
            small((seq, HYENA_FFN_W)), small((seq, 1)),
            pl.BlockSpec((HYENA_FFN_W, tw), fwd), pl.BlockSpec((HYENA_FFN_W, tw), bwd),
            pl.BlockSpec((1, tw), fwd), pl.BlockSpec((1, tw), bwd),
            pl.BlockSpec((1, tw), lambda o, c: (0, c)),
        ],
        out_specs=(pl.BlockSpec((seq, tw), lambda o, c: (0, o * nw + c)),
                   pl.BlockSpec((seq, tw), lambda o, c: (0, o * nw + c))),
        compiler_params=_params("arbitrary", "arbitrary"),
        name=f"hyena_filters_{seq}",
    )(hidden, t, w3, w3, b3r, b3r, deltas)


def _conv_tables(seq):
    n = 2 * seq
    i = jnp.arange(seq, dtype=jnp.int32)
    r = jnp.arange(n, dtype=jnp.int32)
    nyq = (1 - 2 * (i % 2)).astype(F32)
    ang = _dft_angle(r % seq, i, n) - (0.5 * math.pi) * (r >= seq).astype(F32)[:, None]
    fwd = jnp.where(r[:, None] == seq, nyq[None, :], jnp.cos(ang))
    sin_t = jnp.where(i[None, :] == 0, nyq[:, None], jnp.sin(_dft_angle(i, i, n)))
    return fwd.astype(BF16), sin_t.astype(BF16)


def _lconv_kernel(*refs, seq, n_seq_blk, n_chunks, gated):
    refs = list(refs)
    z_ref, xm_ref = refs[:2]
    gate_ref = refs[2] if gated else None
    (bias_ref, mc_ref, ms_ref, ic_ref, is_ref, ucf_ref, ucb_ref, usf_ref, usb_ref, o_ref) = refs[2 + gated:12 + gated]
    acc_ref = refs[12 + gated] if n_chunks > 1 else None
    f = pl.program_id(2)
    usf, usb = usf_ref[...], usb_ref[...]
    gr = ucf_ref[...] + ucb_ref[...]
    gi = usf - usb
    packed = jnp.logical_and(lax.broadcasted_iota(jnp.int32, gr.shape, 0) == 0, f == 0)
    gi = jnp.where(packed, 0.0, gi)
    gn = jnp.where(packed, usf + usb, gr)
    wgt = jnp.where(packed, 1.0, 2.0)
    gr, gi, gn = gr * wgt, gi * wgt, gn * wgt

    def finish(acc, rows):
        z = z_ref[rows, :].astype(F32)
        y = (acc * (1.0 / (2 * seq)) + z * bias_ref[...]) * xm_ref[rows, :].astype(F32)
        if gated:
            y = y * _silu(gate_ref[rows, :].astype(F32))
        o_ref[rows, :] = y.astype(BF16)

    for s in range(n_seq_blk):
        rows = slice(s * seq, (s + 1) * seq)
        z = z_ref[rows, :]
        zr = _dot(mc_ref[...], z)
        zi = _dot(ms_ref[...], z)
        pr = (zr * gr - zi * gi).astype(BF16)
        pi = (zr * gi + zi * gn).astype(BF16)
        contrib = _dot(ic_ref[...], pr) + _dot(is_ref[...], pi)
        if n_chunks == 1:
            finish(contrib, rows)
        else:
            @pl.when(f == 0)
            def _():
                acc_ref[rows, :] = contrib

            @pl.when(f > 0)
            def _():
                acc_ref[rows, :] += contrib

            @pl.when(f == n_chunks - 1)
            def _():
                finish(acc_ref[rows, :], rows)


def _long_conv(buf, z, z_col, xm, xm_col, gate, bias, idx, order, tables, spec_f, spec_b,
               row_start, n_seq, seq, out_rows, tw, fc, seq_blk):
    fwd, sin_t = tables
    nw, nf = HYENA_W // tw, seq // fc
    tr = seq_blk * seq
    r0 = row_start // tr
    gated = gate is not None
    tok = lambda col: pl.BlockSpec((tr, tw), lambda s, w, f: (r0 + s, col * nw + w))
    spec_c = pl.BlockSpec((fc, tw), lambda s, w, f: (f, order * nw + w))
    spec_s = pl.BlockSpec((fc, tw), lambda s, w, f: (nf + f, order * nw + w))
    in_specs = [tok(z_col), tok(xm_col)]
    args = [z, xm]
    if gated:
        in_specs.append(tok(OFF_HGATE // HYENA_W))
        args.append(gate)
    in_specs += [
        pl.BlockSpec((None, 1, tw), lambda s, w, f: (idx, 0, w)),
        pl.BlockSpec((fc, seq), lambda s, w, f: (f, 0)),
        pl.BlockSpec((fc, seq), lambda s, w, f: (nf + f, 0)),
        pl.BlockSpec((seq, fc), lambda s, w, f: (0, f)),
        pl.BlockSpec((seq, fc), lambda s, w, f: (0, f)),
        spec_c, spec_c, spec_s, spec_s,
    ]
    args += [bias, fwd, fwd, fwd, sin_t, spec_f, spec_b, spec_f, spec_b]
    return _buffer_call(
        functools.partial(_lconv_kernel, seq=seq, n_seq_blk=seq_blk, n_chunks=nf, gated=gated), buf, in_specs, args,
        out_shape=jax.ShapeDtypeStruct((out_rows, HYENA_W), BF16),
        grid=(n_seq // seq_blk, nw, nf),
        out_specs=pl.BlockSpec((tr, tw), lambda s, w, f: (r0 + s, w)),
        scratch_shapes=[pltpu.VMEM((tr, tw), F32)] if nf > 1 else [],
        compiler_params=_params("arbitrary", "arbitrary", "arbitrary"),
        name=f"long_conv_{seq}_{order}",
    )


def _merge_kernel(a_ref, f_ref, h_ref, wa_ref, wf_ref, wh_ref, ga_ref, gf_ref, gh_ref, o_ref):
    m = jax.nn.sigmoid(ga_ref[...].astype(F32)) * _dot(a_ref[...], wa_ref[...])
    m = m + jax.nn.sigmoid(gf_ref[...].astype(F32)) * _dot(f_ref[...], wf_ref[...])
    m = m + jax.nn.sigmoid(gh_ref[...].astype(F32)) * _dot(h_ref[...], wh_ref[...])
    o_ref[...] = m.astype(BF16)


def _merge(attn, fnet, hy, w_a, w_f, w_h, layer, proj, tm=1024, tn=512):
    t = attn.shape[0]
    act = lambda w: pl.BlockSpec((tm, w), lambda i, j: (i, 0))
    wgt = lambda k: pl.BlockSpec((None, k, tn), lambda i, j: (layer, 0, j))
    gate = lambda off: pl.BlockSpec((tm, tn), lambda i, j: (i, off // tn + j))
    return pl.pallas_call(
        _merge_kernel,
        out_shape=jax.ShapeDtypeStruct((t, D_MODEL), BF16),
        grid=(t // tm, D_MODEL // tn),
        in_specs=[act(ATTN_W), act(FNET_W), act(HYENA_W), wgt(ATTN_W), wgt(FNET_W), wgt(HYENA_W),
                  gate(OFF_GA), gate(OFF_GF), gate(OFF_GH)],
        out_specs=pl.BlockSpec((tm, tn), lambda i, j: (i, j)),
        compiler_params=_params("arbitrary", "arbitrary"),
        name="merge",
    )(attn, fnet, hy, w_a, w_f, w_h, proj, proj, proj)


def kernel(x_prompt, x_sample, cache_k, cache_v, c, c_ctx, w_mod, b_mod, g_pre, w_in, q_norm, k_norm, hy_short_w, hy_short_b, hy_ffn_w1, hy_ffn_b1, hy_ffn_w2, hy_ffn_b2, hy_ffn_w3, hy_ffn_b3, hy_sin_freq, hy_bias, w_attn_o, w_fnet_o, w_hy_o, w_out, g_post):
    nb, seq, _ = x_prompt.shape
    db, dseq, _ = x_sample.shape
    assert 1 + db <= MOD_ROWS
    n_ctx, n_lat = nb * seq, db * dseq
    t = n_ctx + n_lat
    ctx_row = lambda i: 0
    lat_row = lambda i: 1 + i // (dseq // ROW_TILE)

    cond = jnp.concatenate([c_ctx[None, :], c, jnp.zeros((MOD_ROWS - 1 - db, D_MODEL), F32)], axis=0)
    mod = _modulation(cond, w_mod, b_mod)

    w_kv_b = w_in[:, :, W_OFF_K:W_OFF_K + W_KV].astype(BF16)
    w_a_b, w_f_b, w_h_b = w_attn_o.astype(BF16), w_fnet_o.astype(BF16), w_hy_o.astype(BF16)
    cos2, sin2 = _rope_tables(dseq)
    fnet_tables = {s: _fnet_tables(s) for s in (seq, dseq)}
    conv_tables = {s: _conv_tables(s) for s in (seq, dseq)}
    bias = hy_bias.reshape(DEPTH * HYENA_ORDER, 1, HYENA_W)
    conv_tiles = {seq: (HYENA_W, seq, 4), dseq: (512, 256, 1)}
    skip_kv = lambda j, tn=512: jnp.where(j >= W_OFF_K // tn, j + W_KV // tn, j)

    xs = ((x_prompt.reshape(n_ctx, D_MODEL), 0), (x_sample.reshape(n_lat, D_MODEL), 0))
    new_k, new_v = [], []
    for l in range(DEPTH):
        mod3 = mod[l].reshape(MOD_ROWS, 1, 3 * D_MODEL)
        (x_c, c0), (x_l, l0) = xs
        h = _prenorm(None, x_c, c0, g_pre[l], mod3, ctx_row, 0, n_ctx, t)
        h = _prenorm(h, x_l, l0, g_pre[l], mod3, lat_row, n_ctx, n_lat, t)
        proj = _matmul_wcast(h, w_in, l, BF16, 1024, 512, "in_proj", n_out=PROJ_W, col_map=skip_kv)
        k, v = _kv_proj(h, w_kv_b, l, k_norm[l])
        new_k.append(k[:n_ctx].reshape(nb, seq, N_KV_HEADS, HEAD_DIM))
        new_v.append(v[:n_ctx].reshape(nb, seq, N_KV_HEADS, HEAD_DIM))

        attn = _attention_ctx(proj, k, v, q_norm[l], nb, seq, t)
        attn = _attention_lat(attn, proj, k, v, cache_k, cache_v, l, q_norm[l], cos2, sin2, n_ctx, db, dseq)

        fnet = u = z1 = hy = None
        for row_start, n_seq, s in ((0, nb, seq), (n_ctx, db, dseq)):
            fnet = _fnet(fnet, proj, fnet_tables[s], row_start, n_seq, s, t)
            u = _short_conv(u, proj, hy_short_w[l], hy_short_b[l], row_start, n_seq * s, s, t)
            filt_f, filt_b = _hyena_filters(s, hy_ffn_w1[l], hy_ffn_b1[l], hy_ffn_w2[l], hy_ffn_b2[l],
                                            hy_ffn_w3[l], hy_ffn_b3[l], hy_sin_freq[l])
            fwd = conv_tables[s][0]
            spec_f = _matmul(fwd, filt_f, F32, 512, 512, f"filter_spectrum_fwd_{s}")
            spec_b = _matmul(fwd, filt_b, F32, 512, 512, f"filter_spectrum_bwd_{s}")
            tw, fc, blk = conv_tiles[s]
            z1 = _long_conv(z1, u, 0, u, 1, None, bias, l * HYENA_ORDER, 0, conv_tables[s],
                            spec_f, spec_b, row_start, n_seq, s, t, tw, fc, blk)
            hy = _long_conv(hy, z1, 0, u, 2, proj, bias, l * HYENA_ORDER + 1, 1, conv_tables[s],
                            spec_f, spec_b, row_start, n_seq, s, t, tw, fc, blk)

        merged = _merge(attn, fnet, hy, w_a_b, w_f_b, w_h_b, l, proj)
        r = _matmul_wcast(merged, w_out, l, BF16, 1024, 512, "out_proj")
        if l < DEPTH - 1:
            x_new = _post(None, x_c, c0, r, 0, g_post[l], mod3, ctx_row, 0, n_ctx, t)
            x_new = _post(x_new, x_l, l0, r, n_ctx, g_post[l], mod3, lat_row, n_ctx, n_lat, t)
            xs = ((x_new, 0), (x_new, n_ctx))
        else:
            y_prompt = _post(None, x_c, c0, r, 0, g_post[l], mod3, ctx_row, 0, n_ctx, n_ctx)
            y_sample = _post(None, x_l, l0, r, n_ctx, g_post[l], mod3, lat_row, 0, n_lat, n_lat)

    return (y_prompt.reshape(nb, seq, D_MODEL), y_sample.reshape(db, dseq, D_MODEL),
            jnp.stack(new_k, axis=1), jnp.stack(new_v, axis=1))
```

```python
import functools
import math

import jax
import jax.numpy as jnp
from jax import lax
from jax.experimental import pallas as pl
from jax.experimental.pallas import tpu as pltpu

F32 = jnp.float32
BF16 = jnp.bfloat16

D_MODEL = 4096
DEPTH = 2
GRID_W = 64
HEAD_DIM = 128
N_HEADS = 16
N_KV_HEADS = 4
KV_GROUP = N_HEADS // N_KV_HEADS
ATTN_W = N_HEADS * HEAD_DIM
KV_W = N_KV_HEADS * HEAD_DIM
GROUP_W = KV_GROUP * HEAD_DIM
ATTN_SCALE = HEAD_DIM ** -0.5
ROPE_THETA = 10000.0
ROPE_PAIRS_AXIS = HEAD_DIM // 4
FNET_W = D_MODEL // 4
FNET_GROUPS = 4
FNET_GROUP_W = FNET_W // FNET_GROUPS
HYENA_W = D_MODEL // 4
HYENA_ORDER = 2
HYENA_SHORT = 3
HYENA_BANDS = 16
HYENA_POS_DIM = 1 + 2 * HYENA_BANDS
HYENA_POS_PAD = 128
HYENA_FFN_W = 64
HYENA_MIN_DECAY = math.log(1e-2) / 0.3
HYENA_MAX_DECAY = math.log(1e-2) / 1.5
EPS = 1e-6

W_OFF_K = ATTN_W
W_KV = 2 * KV_W
OFF_Q = 0
OFF_AGATE = OFF_Q + ATTN_W
OFF_FIN = OFF_AGATE + ATTN_W
OFF_FGATE = OFF_FIN + FNET_W
OFF_HV = OFF_FGATE + FNET_W
OFF_HX1 = OFF_HV + HYENA_W
OFF_HX2 = OFF_HX1 + HYENA_W
OFF_HGATE = OFF_HX2 + HYENA_W
OFF_GA = OFF_HGATE + HYENA_W
OFF_GF = OFF_GA + D_MODEL
OFF_GH = OFF_GF + D_MODEL
PROJ_W = OFF_GH + D_MODEL

MOD_ROWS = 8
ROW_TILE = 256
TABLE_SPLIT = 64
VMEM_LIMIT = 56 * 1024 * 1024


def _params(*sem):
    return pltpu.CompilerParams(dimension_semantics=sem, vmem_limit_bytes=VMEM_LIMIT)


def _silu(x):
    return x * jax.nn.sigmoid(x)


def _dot(a, b):
    return jnp.dot(a, b, preferred_element_type=F32)


def _dot_exact(a, b):
    return jnp.dot(a, b, preferred_element_type=F32, precision=lax.Precision.HIGHEST)


def _pair_specs(block, n_ctx_tiles, ctx_tile0=0, lat_tile0=0, col=0, inner=False):
    def ctx(i):
        return (ctx_tile0 + jnp.clip(i, 0, max(n_ctx_tiles - 1, 0)), col)

    def lat(i):
        return (lat_tile0 + jnp.maximum(i - n_ctx_tiles, 0), col)

    if inner:
        return [pl.BlockSpec(block, lambda j, i: ctx(i)), pl.BlockSpec(block, lambda j, i: lat(i))]
    return [pl.BlockSpec(block, lambda i: ctx(i)), pl.BlockSpec(block, lambda i: lat(i))]


def _by_group(i, n_ctx_tiles, body, ctx_refs, lat_refs):
    @pl.when(i < n_ctx_tiles)
    def _():
        body(*ctx_refs)

    @pl.when(i >= n_ctx_tiles)
    def _():
        body(*lat_refs)


def _mod_kernel(c_ref, w_ref, b_ref, o_ref):
    s = _silu(c_ref[...]).astype(BF16)
    o_ref[0] = _dot(s, w_ref[0].astype(BF16)) + b_ref[0]


def _modulation(cond, w_mod, b_mod, tn=512):
    n = w_mod.shape[2]
    return pl.pallas_call(
        _mod_kernel,
        out_shape=jax.ShapeDtypeStruct((DEPTH, MOD_ROWS, n), F32),
        grid=(DEPTH, n // tn),
        in_specs=[
            pl.BlockSpec((MOD_ROWS, D_MODEL), lambda l, j: (0, 0)),
            pl.BlockSpec((1, D_MODEL, tn), lambda l, j: (l, 0, j)),
            pl.BlockSpec((1, 1, tn), lambda l, j: (l, 0, j)),
        ],
        out_specs=pl.BlockSpec((1, MOD_ROWS, tn), lambda l, j: (l, 0, j)),
        compiler_params=_params("arbitrary", "arbitrary"),
        name="modulation",
    )(cond, w_mod, b_mod.reshape(DEPTH, 1, n))


def _mod_spec(chunk, n_ctx_tiles, tiles_per_latent_seq):
    def row(i):
        return jnp.where(i < n_ctx_tiles, 0, 1 + (i - n_ctx_tiles) // tiles_per_latent_seq)
    return pl.BlockSpec((1, 1, D_MODEL), lambda i: (row(i), 0, chunk))


def _prenorm_kernel(xc_ref, xl_ref, g_ref, shift_ref, scale_ref, o_ref, *, n_ctx_tiles):
    def body(x_ref):
        x = x_ref[...]
        y = x * lax.rsqrt(jnp.mean(x * x, axis=-1, keepdims=True) + EPS) * g_ref[...]
        o_ref[...] = (y * (1.0 + scale_ref[0]) + shift_ref[0]).astype(BF16)

    _by_group(pl.program_id(0), n_ctx_tiles, body, (xc_ref,), (xl_ref,))


def _prenorm(x_ctx, ctx_row0, x_lat, lat_row0, g, mod3, n_ctx, n_lat, lat_seq):
    tm = ROW_TILE
    nct = n_ctx // tm
    mod = functools.partial(_mod_spec, n_ctx_tiles=nct, tiles_per_latent_seq=lat_seq // tm)
    return pl.pallas_call(
        functools.partial(_prenorm_kernel, n_ctx_tiles=nct),
        out_shape=jax.ShapeDtypeStruct((n_ctx + n_lat, D_MODEL), BF16),
        grid=((n_ctx + n_lat) // tm,),
        in_specs=_pair_specs((tm, D_MODEL), nct, ctx_row0 // tm, lat_row0 // tm)
        + [pl.BlockSpec((1, D_MODEL), lambda i: (0, 0)), mod(0), mod(1)],
        out_specs=pl.BlockSpec((tm, D_MODEL), lambda i: (i, 0)),
        compiler_params=_params("arbitrary"),
        name="prenorm",
    )(x_ctx, x_lat, g.reshape(1, D_MODEL), mod3, mod3)


def _post_kernel(xc_ref, xl_ref, r_ref, g_ref, gate_ref, o_ref, *, n_ctx_tiles):
    def body(x_ref):
        r = r_ref[...].astype(F32)
        y = r * lax.rsqrt(jnp.mean(r * r, axis=-1, keepdims=True) + EPS) * g_ref[...]
        o_ref[...] = x_ref[...] + gate_ref[0] * y

    _by_group(pl.program_id(0), n_ctx_tiles, body, (xc_ref,), (xl_ref,))


def _post(x_ctx, ctx_row0, x_lat, lat_row0, r, r_row0, g, mod3, n_ctx, n_lat, lat_seq):
    tm = ROW_TILE
    nct = n_ctx // tm
    r0 = r_row0 // tm
    return pl.pallas_call(
        functools.partial(_post_kernel, n_ctx_tiles=nct),
        out_shape=jax.ShapeDtypeStruct((n_ctx + n_lat, D_MODEL), F32),
        grid=((n_ctx + n_lat) // tm,),
        in_specs=_pair_specs((tm, D_MODEL), nct, ctx_row0 // tm, lat_row0 // tm)
        + [pl.BlockSpec((tm, D_MODEL), lambda i: (r0 + i, 0)),
           pl.BlockSpec((1, D_MODEL), lambda i: (0, 0)),
           _mod_spec(2, nct, lat_seq // tm)],
        out_specs=pl.BlockSpec((tm, D_MODEL), lambda i: (i, 0)),
        compiler_params=_params("arbitrary"),
        name="post",
    )(x_ctx, x_lat, r, g.reshape(1, D_MODEL), mod3)


def _mm_kernel(a_ref, b_ref, o_ref):
    o_ref[...] = _dot(a_ref[...], b_ref[...]).astype(o_ref.dtype)


def _matmul(a, b, out_dtype, tm, tn, name):
    m, k = a.shape
    n = b.shape[1]
    return pl.pallas_call(
        _mm_kernel,
        out_shape=jax.ShapeDtypeStruct((m, n), out_dtype),
        grid=(m // tm, n // tn),
        in_specs=[pl.BlockSpec((tm, k), lambda i, j: (i, 0)), pl.BlockSpec((k, tn), lambda i, j: (0, j))],
        out_specs=pl.BlockSpec((tm, tn), lambda i, j: (i, j)),
        compiler_params=_params("arbitrary", "arbitrary"),
        name=name,
    )(a, b)


def _mm_wcast_kernel(a_ref, w_ref, o_ref, wb_ref):
    @pl.when(pl.program_id(1) == 0)
    def _():
        wb_ref[...] = w_ref[...].astype(BF16)

    o_ref[...] = _dot(a_ref[...], wb_ref[...]).astype(o_ref.dtype)


def _matmul_wcast(a, w, layer, out_dtype, tm, tn, name, n_out=None, col_map=None):
    m, k = a.shape
    n = n_out or w.shape[-1]
    col = col_map or (lambda j: j)
    return pl.pallas_call(
        _mm_wcast_kernel,
        out_shape=jax.ShapeDtypeStruct((m, n), out_dtype),
        grid=(n // tn, m // tm),
        in_specs=[
            pl.BlockSpec((tm, k), lambda j, i: (i, 0)),
            pl.BlockSpec((None, k, tn), lambda j, i: (layer, 0, col(j))),
        ],
        out_specs=pl.BlockSpec((tm, tn), lambda j, i: (i, j)),
        scratch_shapes=[pltpu.VMEM((k, tn), BF16)],
        compiler_params=_params("arbitrary", "arbitrary"),
        name=name,
    )(a, w)


def _kv_kernel(h_ref, w_ref, kn_ref, k_ref, v_ref):
    acc = _dot(h_ref[...], w_ref[...])
    for hd in range(N_KV_HEADS):
        sl = slice(hd * HEAD_DIM, (hd + 1) * HEAD_DIM)
        k_ref[:, sl] = _norm_head(acc[:, sl], kn_ref[...])
    v_ref[...] = acc[:, KV_W:]


def _kv_proj(h, w_kv, layer, k_norm, tm=512):
    t = h.shape[0]
    return pl.pallas_call(
        _kv_kernel,
        out_shape=(jax.ShapeDtypeStruct((t, KV_W), F32), jax.ShapeDtypeStruct((t, KV_W), F32)),
        grid=(t // tm,),
        in_specs=[
            pl.BlockSpec((tm, D_MODEL), lambda i: (i, 0)),
            pl.BlockSpec((None, D_MODEL, W_KV), lambda i: (layer, 0, 0)),
            pl.BlockSpec((1, HEAD_DIM), lambda i: (0, 0)),
        ],
        out_specs=(pl.BlockSpec((tm, KV_W), lambda i: (i, 0)), pl.BlockSpec((tm, KV_W), lambda i: (i, 0))),
        compiler_params=_params("arbitrary"),
        name="kv_proj",
    )(h, w_kv, k_norm.reshape(1, HEAD_DIM))


def _softmax_pv(s, v):
    m = jnp.max(s, axis=-1, keepdims=True)
    p = jnp.exp(s - m)
    l = jnp.sum(p, axis=-1, keepdims=True)
    return _dot(p.astype(BF16), v) / l


def _qk(q, k):
    return lax.dot_general(q, k, (((1,), (1,)), ((), ())), preferred_element_type=F32)


def _norm_head(q, g):
    return q * lax.rsqrt(jnp.mean(q * q, axis=-1, keepdims=True) + EPS) * g


def _rope(x, cos2, sin2):
    return x * cos2 + pltpu.roll(x, HEAD_DIM // 2, 1) * sin2


def _attn_ctx_kernel(q_ref, k_ref, v_ref, ag_ref, qn_ref, o_ref):
    seq = q_ref.shape[0]
    for hd in range(N_KV_HEADS):
        kv = slice(hd * HEAD_DIM, (hd + 1) * HEAD_DIM)
        cols = [slice((hd * KV_GROUP + g) * HEAD_DIM, (hd * KV_GROUP + g + 1) * HEAD_DIM) for g in range(KV_GROUP)]
        q = jnp.concatenate([(_norm_head(q_ref[:, c].astype(F32), qn_ref[...]) * ATTN_SCALE).astype(BF16)
                             for c in cols], axis=0)
        o = _softmax_pv(_qk(q, k_ref[:, kv].astype(BF16)), v_ref[:, kv].astype(BF16))
        for g, c in enumerate(cols):
            o_ref[:, c] = (o[g * seq:(g + 1) * seq] * _silu(ag_ref[:, c].astype(F32))).astype(BF16)


def _attention_ctx(proj, k, v, q_norm, n_seq, seq):
    return pl.pallas_call(
        _attn_ctx_kernel,
        out_shape=jax.ShapeDtypeStruct((n_seq * seq, ATTN_W), BF16),
        grid=(n_seq,),
        in_specs=[
            pl.BlockSpec((seq, ATTN_W), lambda b: (b, OFF_Q // ATTN_W)),
            pl.BlockSpec((seq, KV_W), lambda b: (b, 0)),
            pl.BlockSpec((seq, KV_W), lambda b: (b, 0)),
            pl.BlockSpec((seq, ATTN_W), lambda b: (b, OFF_AGATE // ATTN_W)),
            pl.BlockSpec((1, HEAD_DIM), lambda b: (0, 0)),
        ],
        out_specs=pl.BlockSpec((seq, ATTN_W), lambda b: (b, 0)),
        compiler_params=_params("arbitrary"),
        name="attn_ctx",
    )(proj, k, v, proj, q_norm.reshape(1, HEAD_DIM))


def _attn_lat_kernel(q_ref, k_ref, v_ref, ck_ref, cv_ref, ag_ref, qn_ref, cosq_ref, sinq_ref,
                     cosk_ref, sink_ref, o_ref, k_s, v_s, *, seq):
    @pl.when(pl.program_id(2) == 0)
    def _():
        k_s[:seq] = _rope(k_ref[...], cosk_ref[...], sink_ref[...]).astype(BF16)
        k_s[seq:] = ck_ref[...].astype(BF16)
        v_s[:seq] = v_ref[...].astype(BF16)
        v_s[seq:] = cv_ref[...].astype(BF16)

    k = k_s[...]
    v = v_s[...]
    for g in range(KV_GROUP):
        sl = slice(g * HEAD_DIM, (g + 1) * HEAD_DIM)
        q = _norm_head(q_ref[:, sl].astype(F32), qn_ref[...])
        q = _rope(q, cosq_ref[...], sinq_ref[...]) * ATTN_SCALE
        o = _softmax_pv(_qk(q.astype(BF16), k), v)
        o_ref[:, sl] = (o * _silu(ag_ref[:, sl].astype(F32))).astype(BF16)


def _attention_lat(proj, k, v, cache_k, cache_v, layer, q_norm, cos2, sin2, n_ctx, n_seq, seq, tq=512):
    past = cache_k.shape[2]
    ck = cache_k.reshape(cache_k.shape[0], DEPTH, past, KV_W)
    cv = cache_v.reshape(cache_v.shape[0], DEPTH, past, KV_W)
    nq = seq // tq
    row0, seq0 = n_ctx // tq, n_ctx // seq
    return pl.pallas_call(
        functools.partial(_attn_lat_kernel, seq=seq),
        out_shape=jax.ShapeDtypeStruct((n_seq * seq, ATTN_W), BF16),
        grid=(n_seq, N_KV_HEADS, nq),
        in_specs=[
            pl.BlockSpec((tq, GROUP_W), lambda b, h, i: (row0 + b * nq + i, OFF_Q // GROUP_W + h)),
            pl.BlockSpec((seq, HEAD_DIM), lambda b, h, i: (seq0 + b, h)),
            pl.BlockSpec((seq, HEAD_DIM), lambda b, h, i: (seq0 + b, h)),
            pl.BlockSpec((None, None, past, HEAD_DIM), lambda b, h, i: (b, layer, 0, h)),
            pl.BlockSpec((None, None, past, HEAD_DIM), lambda b, h, i: (b, layer, 0, h)),
            pl.BlockSpec((tq, GROUP_W), lambda b, h, i: (row0 + b * nq + i, OFF_AGATE // GROUP_W + h)),
            pl.BlockSpec((1, HEAD_DIM), lambda b, h, i: (0, 0)),
            pl.BlockSpec((tq, HEAD_DIM), lambda b, h, i: (i, 0)),
            pl.BlockSpec((tq, HEAD_DIM), lambda b, h, i: (i, 0)),
            pl.BlockSpec((seq, HEAD_DIM), lambda b, h, i: (0, 0)),
            pl.BlockSpec((seq, HEAD_DIM), lambda b, h, i: (0, 0)),
        ],
        out_specs=pl.BlockSpec((tq, GROUP_W), lambda b, h, i: (b * nq + i, h)),
        scratch_shapes=[pltpu.VMEM((seq + past, HEAD_DIM), BF16), pltpu.VMEM((seq + past, HEAD_DIM), BF16)],
        compiler_params=_params("arbitrary", "arbitrary", "arbitrary"),
        name="attn_lat",
    )(proj, k, v, ck, cv, proj, q_norm.reshape(1, HEAD_DIM), cos2, sin2, cos2, sin2)


def _rope_tables(seq):
    rows = seq // GRID_W
    row = jnp.repeat(jnp.arange(rows, dtype=F32), GRID_W)
    col = jnp.tile(jnp.arange(GRID_W, dtype=F32), rows)
    inv = ROPE_THETA ** (-jnp.arange(ROPE_PAIRS_AXIS, dtype=F32) / ROPE_PAIRS_AXIS)
    ang = jnp.concatenate([row[:, None] * inv, col[:, None] * inv], axis=-1)
    cos, sin = jnp.cos(ang), jnp.sin(ang)
    return jnp.concatenate([cos, cos], axis=-1), jnp.concatenate([-sin, sin], axis=-1)


def _dft_angle(i, j, n):
    return (2.0 * math.pi / n) * ((i[:, None] * j[None, :]) % n).astype(F32)


def _cos_sin_table(rows, cols, n):
    c = jnp.arange(cols, dtype=jnp.int32)
    hi = _dft_angle(TABLE_SPLIT * jnp.arange(rows // TABLE_SPLIT, dtype=jnp.int32), c, n)
    lo = _dft_angle(jnp.arange(TABLE_SPLIT, dtype=jnp.int32), c, n)
    ch, sh = jnp.cos(hi)[:, None, :], jnp.sin(hi)[:, None, :]
    cl, sl = jnp.cos(lo)[None, :, :], jnp.sin(lo)[None, :, :]
    return (ch * cl - sh * sl).reshape(rows, cols), (sh * cl + ch * sl).reshape(rows, cols)


def _fnet_kernel(u_ref, fg_ref, csc_ref, ml_ref, o_ref, t_s, *, seq, scale):
    @pl.when(pl.program_id(1) == 0)
    def _():
        for g in range(FNET_GROUPS):
            sl = slice(g * FNET_GROUP_W, (g + 1) * FNET_GROUP_W)
            t = _dot(u_ref[:, sl], csc_ref[...])
            t_s[:seq, sl] = t[:, :FNET_GROUP_W].astype(BF16)
            t_s[seq:, sl] = t[:, FNET_GROUP_W:].astype(BF16)

    y = _dot(ml_ref[...], t_s[...]) * scale
    o_ref[...] = (y * _silu(fg_ref[...].astype(F32))).astype(BF16)


def _fnet_tables(seq):
    j = jnp.arange(FNET_GROUP_W, dtype=jnp.int32)
    ang = _dft_angle(j, j, FNET_GROUP_W)
    csc = jnp.concatenate([jnp.cos(ang), jnp.sin(ang)], axis=1).astype(BF16)
    cos, sin = _cos_sin_table(seq, seq, seq)
    ml = jnp.concatenate([cos, -sin], axis=1).astype(BF16)
    return csc, ml


def _fnet(proj, tables, row_start, n_seq, seq, tr=256):
    csc, ml = tables
    nr = seq // tr
    s0, r0 = row_start // seq, row_start // tr
    return pl.pallas_call(
        functools.partial(_fnet_kernel, seq=seq, scale=1.0 / math.sqrt(seq * FNET_GROUP_W)),
        out_shape=jax.ShapeDtypeStruct((n_seq * seq, FNET_W), BF16),
        grid=(n_seq, nr),
        in_specs=[
            pl.BlockSpec((seq, FNET_W), lambda b, r: (s0 + b, OFF_FIN // FNET_W)),
            pl.BlockSpec((tr, FNET_W), lambda b, r: (r0 + b * nr + r, OFF_FGATE // FNET_W)),
            pl.BlockSpec((FNET_GROUP_W, 2 * FNET_GROUP_W), lambda b, r: (0, 0)),
            pl.BlockSpec((tr, 2 * seq), lambda b, r: (r, 0)),
        ],
        out_specs=pl.BlockSpec((tr, FNET_W), lambda b, r: (b * nr + r, 0)),
        scratch_shapes=[pltpu.VMEM((2 * seq, FNET_W), BF16)],
        compiler_params=_params("arbitrary", "arbitrary"),
        name=f"fnet_{seq}",
    )(proj, proj, csc, ml)


def _short_kernel(x_ref, w_ref, b_ref, o_ref, *, seq):
    x = x_ref[...].astype(F32)
    n = x.shape[0]
    pos = lax.broadcasted_iota(jnp.int32, x.shape, 0) % seq
    prev = jnp.where(pos == 0, 0.0, pltpu.roll(x, 1, 0))
    nxt = jnp.where(pos == seq - 1, 0.0, pltpu.roll(x, n - 1, 0))
    y = b_ref[...] + prev * w_ref[0:1, :] + x * w_ref[1:2, :] + nxt * w_ref[2:3, :]
    o_ref[...] = y.astype(BF16)


def _short_conv(proj, w, b, row_start, n_rows, seq, tr=2048, tw=512):
    width = 3 * HYENA_W
    r0 = row_start // tr
    return pl.pallas_call(
        functools.partial(_short_kernel, seq=seq),
        out_shape=jax.ShapeDtypeStruct((n_rows, width), BF16),
        grid=(n_rows // tr, width // tw),
        in_specs=[
            pl.BlockSpec((tr, tw), lambda s, c: (r0 + s, OFF_HV // tw + c)),
            pl.BlockSpec((HYENA_SHORT, tw), lambda s, c: (0, c)),
            pl.BlockSpec((1, tw), lambda s, c: (0, c)),
        ],
        out_specs=pl.BlockSpec((tr, tw), lambda s, c: (s, c)),
        compiler_params=_params("arbitrary", "arbitrary"),
        name=f"short_conv_{seq}",
    )(proj, w, b.reshape(1, width))


def _filt_mlp_kernel(feats_ref, w1_ref, b1_ref, w2_ref, b2_ref, fr_ref, o_ref):
    fr = fr_ref[...]
    h = jnp.sin(fr * (_dot_exact(feats_ref[...], w1_ref[...]) + b1_ref[...]))
    o_ref[...] = jnp.sin(fr * (_dot_exact(h, w2_ref[...]) + b2_ref[...]))


def _filt_kernel(h_ref, t_ref, w3f_ref, w3b_ref, b3f_ref, b3b_ref, dl_ref, of_ref, ob_ref):
    h = h_ref[...]
    decay = jnp.exp(-t_ref[...] * dl_ref[...])
    hf = (_dot_exact(h, w3f_ref[...]) + b3f_ref[...]) * decay
    hb = (_dot_exact(h, w3b_ref[...]) + b3b_ref[...]) * decay
    den = jnp.sum(jnp.abs(hf), axis=0, keepdims=True) + jnp.sum(jnp.abs(hb), axis=0, keepdims=True) + EPS
    of_ref[...] = (hf / den).astype(BF16)
    ob_ref[...] = (hb / den).astype(BF16)


def _hyena_filters(seq, w1, b1, w2, b2, w3, b3, freq, tw=512):
    t = jnp.arange(seq, dtype=F32)[:, None] / seq
    bands = jnp.arange(1, HYENA_BANDS + 1, dtype=F32)[None, :]
    feats = jnp.concatenate([t, jnp.cos(2 * math.pi * t * bands), jnp.sin(2 * math.pi * t * bands),
                             jnp.zeros((seq, HYENA_POS_PAD - HYENA_POS_DIM), F32)], axis=-1)
    w1p = jnp.concatenate([w1, jnp.zeros((HYENA_POS_PAD - HYENA_POS_DIM, HYENA_FFN_W), F32)], axis=0)
    deltas = jnp.abs(jnp.linspace(HYENA_MIN_DECAY, HYENA_MAX_DECAY, HYENA_W, dtype=F32))[None, :]
    hidden = pl.pallas_call(
        _filt_mlp_kernel,
        out_shape=jax.ShapeDtypeStruct((seq, HYENA_FFN_W), F32),
        name=f"hyena_filter_mlp_{seq}",
    )(feats, w1p, b1.reshape(1, -1), w2, b2.reshape(1, -1), freq.reshape(1, -1))
    nw = HYENA_W // tw
    small = lambda shape: pl.BlockSpec(shape, lambda o, c: (0, 0))
    out_sds = jax.ShapeDtypeStruct((seq, HYENA_ORDER * HYENA_W), BF16)
    fwd = lambda o, c: (0, (2 * o) * nw + c)
    bwd = lambda o, c: (0, (2 * o + 1) * nw + c)
    b3r = b3.reshape(1, -1)
    return pl.pallas_call(
        _filt_kernel,
        out_shape=(out_sds, out_sds),
        grid=(HYENA_ORDER, nw),
        in_specs=[
            small((seq, HYENA_FFN_W)), small((seq, 1)),
            pl.BlockSpec((HYENA_FFN_W, tw), fwd), pl.BlockSpec((HYENA_FFN_W, tw), bwd),
            pl.BlockSpec((1, tw), fwd), pl.BlockSpec((1, tw), bwd),
            pl.BlockSpec((1, tw), lambda o, c: (0, c)),
        ],
        out_specs=(pl.BlockSpec((seq, tw), lambda o, c: (0, o * nw + c)),
                   pl.BlockSpec((seq, tw), lambda o, c: (0, o * nw + c))),
        compiler_params=_params("arbitrary", "arbitrary"),
        name=f"hyena_filters_{seq}",
    )(hidden, t, w3, w3, b3r, b3r, deltas)


def _conv_tables(seq):
    i = jnp.arange(seq, dtype=jnp.int32)
    nyq = (1 - 2 * (i % 2)).astype(F32)
    cos, sin = _cos_sin_table(seq, seq, 2 * seq)
    fwd = jnp.concatenate([cos, jnp.where(i[:, None] == 0, nyq[None, :], sin)], axis=0)
    sin_t = jnp.where(i[None, :] == 0, nyq[:, None], sin)
    return fwd.astype(BF16), sin_t.astype(BF16)


def _lconv_kernel(*refs, seq, n_seq_blk, n_chunks, gated):
    refs = list(refs)
    z_ref, xm_ref = refs[:2]
    gate_ref = refs[2] if gated else None
    (bias_ref, mc_ref, ms_ref, ic_ref, is_ref, ucf_ref, ucb_ref, usf_ref, usb_ref, o_ref) = refs[2 + gated:12 + gated]
    acc_ref = refs[12 + gated] if n_chunks > 1 else None
    f = pl.program_id(2)
    usf, usb = usf_ref[...], usb_ref[...]
    gr = ucf_ref[...] + ucb_ref[...]
    gi = usf - usb
    packed = jnp.logical_and(lax.broadcasted_iota(jnp.int32, gr.shape, 0) == 0, f == 0)
    gi = jnp.where(packed, 0.0, gi)
    gn = jnp.where(packed, usf + usb, gr)
    wgt = jnp.where(packed, 1.0, 2.0)
    gr, gi, gn = gr * wgt, gi * wgt, gn * wgt

    def finish(acc, rows):
        z = z_ref[rows, :].astype(F32)
        y = (acc * (1.0 / (2 * seq)) + z * bias_ref[...]) * xm_ref[rows, :].astype(F32)
        if gated:
            y = y * _silu(gate_ref[rows, :].astype(F32))
        o_ref[rows, :] = y.astype(BF16)

    for s in range(n_seq_blk):
        rows = slice(s * seq, (s + 1) * seq)
        z = z_ref[rows, :]
        zr = _dot(mc_ref[...], z)
        zi = _dot(ms_ref[...], z)
        pr = (zr * gr - zi * gi).astype(BF16)
        pi = (zr * gi + zi * gn).astype(BF16)
        contrib = _dot(ic_ref[...], pr) + _dot(is_ref[...], pi)
        if n_chunks == 1:
            finish(contrib, rows)
        else:
            @pl.when(f == 0)
            def _():
                acc_ref[rows, :] = contrib

            @pl.when(f > 0)
            def _():
                acc_ref[rows, :] += contrib

            @pl.when(f == n_chunks - 1)
            def _():
                finish(acc_ref[rows, :], rows)


def _long_conv(z, z_col, xm, xm_col, gate, gate_row0, bias, idx, order, tables, spec_f, spec_b,
               n_seq, seq, tw, fc, seq_blk):
    fwd, sin_t = tables
    nw, nf = HYENA_W // tw, seq // fc
    tr = seq_blk * seq
    gated = gate is not None
    tok = lambda col, r0=0: pl.BlockSpec((tr, tw), lambda s, w, f: (r0 + s, col * nw + w))
    spec_c = pl.BlockSpec((fc, tw), lambda s, w, f: (f, order * nw + w))
    spec_s = pl.BlockSpec((fc, tw), lambda s, w, f: (nf + f, order * nw + w))
    in_specs = [tok(z_col), tok(xm_col)]
    args = [z, xm]
    if gated:
        in_specs.append(tok(OFF_HGATE // HYENA_W, gate_row0 // tr))
        args.append(gate)
    in_specs += [
        pl.BlockSpec((None, 1, tw), lambda s, w, f: (idx, 0, w)),
        pl.BlockSpec((fc, seq), lambda s, w, f: (f, 0)),
        pl.BlockSpec((fc, seq), lambda s, w, f: (nf + f, 0)),
        pl.BlockSpec((seq, fc), lambda s, w, f: (0, f)),
        pl.BlockSpec((seq, fc), lambda s, w, f: (0, f)),
        spec_c, spec_c, spec_s, spec_s,
    ]
    args += [bias, fwd, fwd, fwd, sin_t, spec_f, spec_b, spec_f, spec_b]
    return pl.pallas_call(
        functools.partial(_lconv_kernel, seq=seq, n_seq_blk=seq_blk, n_chunks=nf, gated=gated),
        out_shape=jax.ShapeDtypeStruct((n_seq * seq, HYENA_W), BF16),
        grid=(n_seq // seq_blk, nw, nf),
        in_specs=in_specs,
        out_specs=pl.BlockSpec((tr, tw), lambda s, w, f: (s, w)),
        scratch_shapes=[pltpu.VMEM((tr, tw), F32)] if nf > 1 else [],
        compiler_params=_params("arbitrary", "arbitrary", "arbitrary"),
        name=f"long_conv_{seq}_{order}",
    )(*args)


def _merge_kernel(ac_ref, al_ref, fc_ref, fl_ref, hc_ref, hl_ref, wa_ref, wf_ref, wh_ref,
                  ga_ref, gf_ref, gh_ref, o_ref, wab_ref, wfb_ref, whb_ref, *, n_ctx_tiles):
    i = pl.program_id(1)

    @pl.when(i == 0)
    def _():
        wab_ref[...] = wa_ref[...].astype(BF16)
        wfb_ref[...] = wf_ref[...].astype(BF16)
        whb_ref[...] = wh_ref[...].astype(BF16)

    def body(a_ref, f_ref, h_ref):
        m = jax.nn.sigmoid(ga_ref[...].astype(F32)) * _dot(a_ref[...], wab_ref[...])
        m = m + jax.nn.sigmoid(gf_ref[...].astype(F32)) * _dot(f_ref[...], wfb_ref[...])
        m = m + jax.nn.sigmoid(gh_ref[...].astype(F32)) * _dot(h_ref[...], whb_ref[...])
        o_ref[...] = m.astype(BF16)

    _by_group(i, n_ctx_tiles, body, (ac_ref, fc_ref, hc_ref), (al_ref, fl_ref, hl_ref))


def _merge(attn, fnet, hy, w_a, w_f, w_h, layer, proj, n_ctx, tm=512, tn=512):
    t = proj.shape[0]
    nct = n_ctx // tm
    act = lambda w: _pair_specs((tm, w), nct, inner=True)
    wgt = lambda k: pl.BlockSpec((None, k, tn), lambda j, i: (layer, 0, j))
    gate = lambda off: pl.BlockSpec((tm, tn), lambda j, i: (i, off // tn + j))
    return pl.pallas_call(
        functools.partial(_merge_kernel, n_ctx_tiles=nct),
        out_shape=jax.ShapeDtypeStruct((t, D_MODEL), BF16),
        grid=(D_MODEL // tn, t // tm),
        in_specs=act(ATTN_W) + act(FNET_W) + act(HYENA_W) + [wgt(ATTN_W), wgt(FNET_W), wgt(HYENA_W),
                                                             gate(OFF_GA), gate(OFF_GF), gate(OFF_GH)],
        out_specs=pl.BlockSpec((tm, tn), lambda j, i: (i, j)),
        scratch_shapes=[pltpu.VMEM((ATTN_W, tn), BF16), pltpu.VMEM((FNET_W, tn), BF16),
                        pltpu.VMEM((HYENA_W, tn), BF16)],
        compiler_params=_params("arbitrary", "arbitrary"),
        name="merge",
    )(*attn, *fnet, *hy, w_a, w_f, w_h, proj, proj, proj)


def kernel(x_prompt, x_sample, cache_k, cache_v, c, c_ctx, w_mod, b_mod, g_pre, w_in, q_norm, k_norm, hy_short_w, hy_short_b, hy_ffn_w1, hy_ffn_b1, hy_ffn_w2, hy_ffn_b2, hy_ffn_w3, hy_ffn_b3, hy_sin_freq, hy_bias, w_attn_o, w_fnet_o, w_hy_o, w_out, g_post):
    nb, seq, _ = x_prompt.shape
    db, dseq, _ = x_sample.shape
    assert 1 + db <= MOD_ROWS
    n_ctx, n_lat = nb * seq, db * dseq

    cond = jnp.concatenate([c_ctx[None, :], c, jnp.zeros((MOD_ROWS - 1 - db, D_MODEL), F32)], axis=0)
    mod = _modulation(cond, w_mod, b_mod)

    w_kv_b = w_in[:, :, W_OFF_K:W_OFF_K + W_KV].astype(BF16)
    cos2, sin2 = _rope_tables(dseq)
    fnet_tables = {s: _fnet_tables(s) for s in (seq, dseq)}
    conv_tables = {s: _conv_tables(s) for s in (seq, dseq)}
    bias = hy_bias.reshape(DEPTH * HYENA_ORDER, 1, HYENA_W)
    conv_tiles = {seq: (HYENA_W, seq, 4), dseq: (512, 256, 1)}
    skip_kv = lambda j, tn=512: jnp.where(j >= W_OFF_K // tn, j + W_KV // tn, j)

    xs = (x_prompt.reshape(n_ctx, D_MODEL), 0, x_sample.reshape(n_lat, D_MODEL), 0)
    new_k, new_v = [], []
    for l in range(DEPTH):
        mod3 = mod[l].reshape(MOD_ROWS, 1, 3 * D_MODEL)
        h = _prenorm(*xs, g_pre[l], mod3, n_ctx, n_lat, dseq)
        proj = _matmul_wcast(h, w_in, l, BF16, 1024, 512, "in_proj", n_out=PROJ_W, col_map=skip_kv)
        k, v = _kv_proj(h, w_kv_b, l, k_norm[l])
        new_k.append(k[:n_ctx].reshape(nb, seq, N_KV_HEADS, HEAD_DIM))
        new_v.append(v[:n_ctx].reshape(nb, seq, N_KV_HEADS, HEAD_DIM))

        attn = (_attention_ctx(proj, k, v, q_norm[l], nb, seq),
                _attention_lat(proj, k, v, cache_k, cache_v, l, q_norm[l], cos2, sin2, n_ctx, db, dseq))

        fnet, hy = [], []
        for row_start, n_seq, s in ((0, nb, seq), (n_ctx, db, dseq)):
            fnet.append(_fnet(proj, fnet_tables[s], row_start, n_seq, s))
            u = _short_conv(proj, hy_short_w[l], hy_short_b[l], row_start, n_seq * s, s)
            filt_f, filt_b = _hyena_filters(s, hy_ffn_w1[l], hy_ffn_b1[l], hy_ffn_w2[l], hy_ffn_b2[l],
                                            hy_ffn_w3[l], hy_ffn_b3[l], hy_sin_freq[l])
            fwd = conv_tables[s][0]
            spec_f = _matmul(fwd, filt_f, F32, 512, 512, f"filter_spectrum_fwd_{s}")
            spec_b = _matmul(fwd, filt_b, F32, 512, 512, f"filter_spectrum_bwd_{s}")
            tw, fc, blk = conv_tiles[s]
            z1 = _long_conv(u, 0, u, 1, None, 0, bias, l * HYENA_ORDER, 0, conv_tables[s],
                            spec_f, spec_b, n_seq, s, tw, fc, blk)
            hy.append(_long_conv(z1, 0, u, 2, proj, row_start, bias, l * HYENA_ORDER + 1, 1, conv_tables[s],
                                 spec_f, spec_b, n_seq, s, tw, fc, blk))

        merged = _merge(attn, fnet, hy, w_attn_o, w_fnet_o, w_hy_o, l, proj, n_ctx)
        r = _matmul_wcast(merged, w_out, l, BF16, 1024, 512, "out_proj")
        if l < DEPTH - 1:
            x_new = _post(*xs, r, 0, g_post[l], mod3, n_ctx, n_lat, dseq)
            xs = (x_new, 0, x_new, n_ctx)
        else:
            y_prompt = _post(*xs, r, 0, g_post[l], mod3, n_ctx, 0, dseq)
            y_sample = _post(*xs, r, n_ctx, g_post[l], mod3, 0, n_lat, dseq)

    return (y_prompt.reshape(nb, seq, D_MODEL), y_sample.reshape(db, dseq, D_MODEL),
            jnp.stack(new_k, axis=1), jnp.stack(new_v, axis=1))
```

```python
import functools
import math

import jax
import jax.numpy as jnp
from jax import lax
from jax.experimental import pallas as pl
from jax.experimental.pallas import tpu as pltpu

F32 = jnp.float32
BF16 = jnp.bfloat16

D_MODEL = 4096
DEPTH = 2
GRID_W = 64
HEAD_DIM = 128
N_HEADS = 16
N_KV_HEADS = 4
KV_GROUP = N_HEADS // N_KV_HEADS
ATTN_W = N_HEADS * HEAD_DIM
KV_W = N_KV_HEADS * HEAD_DIM
GROUP_W = KV_GROUP * HEAD_DIM
ATTN_SCALE = HEAD_DIM ** -0.5
Q_SCALE = ATTN_SCALE * math.log2(math.e)
ROPE_THETA = 10000.0
ROPE_PAIRS_AXIS = HEAD_DIM // 4
FNET_W = D_MODEL // 4
FNET_GROUPS = 4
FNET_GROUP_W = FNET_W // FNET_GROUPS
HYENA_W = D_MODEL // 4
HYENA_ORDER = 2
HYENA_SHORT = 3
HYENA_BANDS = 16
HYENA_POS_DIM = 1 + 2 * HYENA_BANDS
HYENA_POS_PAD = 128
HYENA_FFN_W = 64
HYENA_MIN_DECAY = math.log(1e-2) / 0.3
HYENA_MAX_DECAY = math.log(1e-2) / 1.5
EPS = 1e-6

W_OFF_K = ATTN_W
W_KV = 2 * KV_W
OFF_Q = 0
OFF_AGATE = OFF_Q + ATTN_W
OFF_FIN = OFF_AGATE + ATTN_W
OFF_FGATE = OFF_FIN + FNET_W
OFF_HV = OFF_FGATE + FNET_W
OFF_HX1 = OFF_HV + HYENA_W
OFF_HX2 = OFF_HX1 + HYENA_W
OFF_HGATE = OFF_HX2 + HYENA_W
OFF_GA = OFF_HGATE + HYENA_W
OFF_GF = OFF_GA + D_MODEL
OFF_GH = OFF_GF + D_MODEL
PROJ_W = OFF_GH + D_MODEL

MOD_ROWS = 8
ROW_TILE = 256
ATTN_ROWS = 256
TABLE_SPLIT = 64
VMEM_LIMIT = 56 * 1024 * 1024


def _params(*sem):
    return pltpu.CompilerParams(dimension_semantics=sem, vmem_limit_bytes=VMEM_LIMIT)


def _silu(x):
    return x * jax.nn.sigmoid(x)


def _dot(a, b):
    return jnp.dot(a, b, preferred_element_type=F32)


def _dot_exact(a, b):
    return jnp.dot(a, b, preferred_element_type=F32, precision=lax.Precision.HIGHEST)


def _pair_specs(block, n_ctx_tiles, ctx_tile0=0, lat_tile0=0, col=0, inner=False):
    def ctx(i):
        return (ctx_tile0 + jnp.clip(i, 0, max(n_ctx_tiles - 1, 0)), col)

    def lat(i):
        return (lat_tile0 + jnp.maximum(i - n_ctx_tiles, 0), col)

    if inner:
        return [pl.BlockSpec(block, lambda j, i: ctx(i)), pl.BlockSpec(block, lambda j, i: lat(i))]
    return [pl.BlockSpec(block, lambda i: ctx(i)), pl.BlockSpec(block, lambda i: lat(i))]


def _by_group(i, n_ctx_tiles, body, ctx_refs, lat_refs):
    @pl.when(i < n_ctx_tiles)
    def _():
        body(*ctx_refs)

    @pl.when(i >= n_ctx_tiles)
    def _():
        body(*lat_refs)


def _mod_kernel(c_ref, w_ref, b_ref, o_ref):
    s = _silu(c_ref[...]).astype(BF16)
    o_ref[0] = _dot(s, w_ref[0].astype(BF16)) + b_ref[0]


def _modulation(cond, w_mod, b_mod, tn=512):
    n = w_mod.shape[2]
    return pl.pallas_call(
        _mod_kernel,
        out_shape=jax.ShapeDtypeStruct((DEPTH, MOD_ROWS, n), F32),
        grid=(DEPTH, n // tn),
        in_specs=[
            pl.BlockSpec((MOD_ROWS, D_MODEL), lambda l, j: (0, 0)),
            pl.BlockSpec((1, D_MODEL, tn), lambda l, j: (l, 0, j)),
            pl.BlockSpec((1, 1, tn), lambda l, j: (l, 0, j)),
        ],
        out_specs=pl.BlockSpec((1, MOD_ROWS, tn), lambda l, j: (l, 0, j)),
        compiler_params=_params("arbitrary", "arbitrary"),
        name="modulation",
    )(cond, w_mod, b_mod.reshape(DEPTH, 1, n))


def _mod_spec(chunk, n_ctx_tiles, tiles_per_latent_seq):
    def row(i):
        return jnp.where(i < n_ctx_tiles, 0, 1 + (i - n_ctx_tiles) // tiles_per_latent_seq)
    return pl.BlockSpec((1, 1, D_MODEL), lambda i: (row(i), 0, chunk))


def _modulated_norm(x, g_ref, shift_ref, scale_ref):
    y = x * lax.rsqrt(jnp.mean(x * x, axis=-1, keepdims=True) + EPS) * g_ref[...]
    return (y * (1.0 + scale_ref[0]) + shift_ref[0]).astype(BF16)


def _prenorm_kernel(xc_ref, xl_ref, g_ref, shift_ref, scale_ref, o_ref, *, n_ctx_tiles):
    def body(x_ref):
        o_ref[...] = _modulated_norm(x_ref[...], g_ref, shift_ref, scale_ref)

    _by_group(pl.program_id(0), n_ctx_tiles, body, (xc_ref,), (xl_ref,))


def _prenorm(x_ctx, ctx_row0, x_lat, lat_row0, g, mod3, n_ctx, n_lat, lat_seq):
    tm = ROW_TILE
    nct = n_ctx // tm
    mod = functools.partial(_mod_spec, n_ctx_tiles=nct, tiles_per_latent_seq=lat_seq // tm)
    return pl.pallas_call(
        functools.partial(_prenorm_kernel, n_ctx_tiles=nct),
        out_shape=jax.ShapeDtypeStruct((n_ctx + n_lat, D_MODEL), BF16),
        grid=((n_ctx + n_lat) // tm,),
        in_specs=_pair_specs((tm, D_MODEL), nct, ctx_row0 // tm, lat_row0 // tm)
        + [pl.BlockSpec((1, D_MODEL), lambda i: (0, 0)), mod(0), mod(1)],
        out_specs=pl.BlockSpec((tm, D_MODEL), lambda i: (i, 0)),
        compiler_params=_params("arbitrary"),
        name="prenorm",
    )(x_ctx, x_lat, g.reshape(1, D_MODEL), mod3, mod3)


def _post_kernel(xc_ref, xl_ref, r_ref, g_ref, gate_ref, *rest, n_ctx_tiles, with_next):
    if with_next:
        g_next_ref, shift_ref, scale_ref, o_ref, h_ref = rest
    else:
        (o_ref,) = rest

    def body(x_ref):
        r = r_ref[...].astype(F32)
        y = r * lax.rsqrt(jnp.mean(r * r, axis=-1, keepdims=True) + EPS) * g_ref[...]
        x = x_ref[...] + gate_ref[0] * y
        o_ref[...] = x
        if with_next:
            h_ref[...] = _modulated_norm(x, g_next_ref, shift_ref, scale_ref)

    _by_group(pl.program_id(0), n_ctx_tiles, body, (xc_ref,), (xl_ref,))


def _post(x_ctx, ctx_row0, x_lat, lat_row0, r, r_row0, g, mod3, n_ctx, n_lat, lat_seq, next_norm=None):
    tm = ROW_TILE
    nct = n_ctx // tm
    r0 = r_row0 // tm
    rows = n_ctx + n_lat
    mod = functools.partial(_mod_spec, n_ctx_tiles=nct, tiles_per_latent_seq=lat_seq // tm)
    vec = pl.BlockSpec((1, D_MODEL), lambda i: (0, 0))
    tile = pl.BlockSpec((tm, D_MODEL), lambda i: (i, 0))
    in_specs = _pair_specs((tm, D_MODEL), nct, ctx_row0 // tm, lat_row0 // tm) + [
        pl.BlockSpec((tm, D_MODEL), lambda i: (r0 + i, 0)), vec, mod(2)]
    args = [x_ctx, x_lat, r, g.reshape(1, D_MODEL), mod3]
    out_shape, out_specs = jax.ShapeDtypeStruct((rows, D_MODEL), F32), tile
    if next_norm is not None:
        g_next, mod3_next = next_norm
        in_specs += [vec, mod(0), mod(1)]
        args += [g_next.reshape(1, D_MODEL), mod3_next, mod3_next]
        out_shape, out_specs = (out_shape, jax.ShapeDtypeStruct((rows, D_MODEL), BF16)), (tile, tile)
    return pl.pallas_call(
        functools.partial(_post_kernel, n_ctx_tiles=nct, with_next=next_norm is not None),
        out_shape=out_shape,
        grid=(rows // tm,),
        in_specs=in_specs,
        out_specs=out_specs,
        compiler_params=_params("arbitrary"),
        name="post",
    )(*args)


def _mm_kernel(a_ref, b_ref, o_ref):
    o_ref[...] = _dot(a_ref[...], b_ref[...]).astype(o_ref.dtype)


def _matmul(a, b, out_dtype, tm, tn, name):
    m, k = a.shape
    n = b.shape[1]
    return pl.pallas_call(
        _mm_kernel,
        out_shape=jax.ShapeDtypeStruct((m, n), out_dtype),
        grid=(m // tm, n // tn),
        in_specs=[pl.BlockSpec((tm, k), lambda i, j: (i, 0)), pl.BlockSpec((k, tn), lambda i, j: (0, j))],
        out_specs=pl.BlockSpec((tm, tn), lambda i, j: (i, j)),
        compiler_params=_params("arbitrary", "arbitrary"),
        name=name,
    )(a, b)


def _mm_wcast_kernel(a_ref, w_ref, o_ref, wb_ref):
    @pl.when(pl.program_id(1) == 0)
    def _():
        wb_ref[...] = w_ref[...].astype(BF16)

    o_ref[...] = _dot(a_ref[...], wb_ref[...]).astype(o_ref.dtype)


def _matmul_wcast(a, w, layer, out_dtype, tm, tn, name, n_out=None, col_map=None):
    m, k = a.shape
    n = n_out or w.shape[-1]
    col = col_map or (lambda j: j)
    return pl.pallas_call(
        _mm_wcast_kernel,
        out_shape=jax.ShapeDtypeStruct((m, n), out_dtype),
        grid=(n // tn, m // tm),
        in_specs=[
            pl.BlockSpec((tm, k), lambda j, i: (i, 0)),
            pl.BlockSpec((None, k, tn), lambda j, i: (layer, 0, col(j))),
        ],
        out_specs=pl.BlockSpec((tm, tn), lambda j, i: (i, j)),
        scratch_shapes=[pltpu.VMEM((k, tn), BF16)],
        compiler_params=_params("arbitrary", "arbitrary"),
        name=name,
    )(a, w)


def _kv_kernel(h_ref, w_ref, kn_ref, k_ref, v_ref):
    acc = _dot(h_ref[...], w_ref[...])
    for hd in range(N_KV_HEADS):
        sl = slice(hd * HEAD_DIM, (hd + 1) * HEAD_DIM)
        k_ref[:, sl] = _norm_head(acc[:, sl], kn_ref[...])
    v_ref[...] = acc[:, KV_W:]


def _kv_proj(h, w_kv, layer, k_norm, tm=512):
    t = h.shape[0]
    return pl.pallas_call(
        _kv_kernel,
        out_shape=(jax.ShapeDtypeStruct((t, KV_W), F32), jax.ShapeDtypeStruct((t, KV_W), F32)),
        grid=(t // tm,),
        in_specs=[
            pl.BlockSpec((tm, D_MODEL), lambda i: (i, 0)),
            pl.BlockSpec((None, D_MODEL, W_KV), lambda i: (layer, 0, 0)),
            pl.BlockSpec((1, HEAD_DIM), lambda i: (0, 0)),
        ],
        out_specs=(pl.BlockSpec((tm, KV_W), lambda i: (i, 0)), pl.BlockSpec((tm, KV_W), lambda i: (i, 0))),
        compiler_params=_params("arbitrary"),
        name="kv_proj",
    )(h, w_kv, k_norm.reshape(1, HEAD_DIM))


def _softmax_pv(s, v):
    m = jnp.max(s, axis=-1, keepdims=True)
    p = jnp.exp2(s - m)
    l = jnp.sum(p, axis=-1, keepdims=True)
    return _dot(p.astype(BF16), v) / l


def _qk(q, k):
    return lax.dot_general(q, k, (((1,), (1,)), ((), ())), preferred_element_type=F32)


def _norm_head(q, g):
    return q * lax.rsqrt(jnp.mean(q * q, axis=-1, keepdims=True) + EPS) * g


def _rope(x, cos2, sin2):
    return x * cos2 + pltpu.roll(x, HEAD_DIM // 2, 1) * sin2


def _attn_ctx_kernel(q_ref, k_ref, v_ref, ag_ref, qn_ref, o_ref):
    seq = q_ref.shape[0]
    for hd in range(N_KV_HEADS):
        kv = slice(hd * HEAD_DIM, (hd + 1) * HEAD_DIM)
        cols = [slice((hd * KV_GROUP + g) * HEAD_DIM, (hd * KV_GROUP + g + 1) * HEAD_DIM) for g in range(KV_GROUP)]
        q = jnp.concatenate([(_norm_head(q_ref[:, c].astype(F32), qn_ref[...]) * Q_SCALE).astype(BF16)
                             for c in cols], axis=0)
        o = _softmax_pv(_qk(q, k_ref[:, kv].astype(BF16)), v_ref[:, kv].astype(BF16))
        for g, c in enumerate(cols):
            o_ref[:, c] = (o[g * seq:(g + 1) * seq] * _silu(ag_ref[:, c].astype(F32))).astype(BF16)


def _attention_ctx(proj, k, v, q_norm, n_seq, seq):
    return pl.pallas_call(
        _attn_ctx_kernel,
        out_shape=jax.ShapeDtypeStruct((n_seq * seq, ATTN_W), BF16),
        grid=(n_seq,),
        in_specs=[
            pl.BlockSpec((seq, ATTN_W), lambda b: (b, OFF_Q // ATTN_W)),
            pl.BlockSpec((seq, KV_W), lambda b: (b, 0)),
            pl.BlockSpec((seq, KV_W), lambda b: (b, 0)),
            pl.BlockSpec((seq, ATTN_W), lambda b: (b, OFF_AGATE // ATTN_W)),
            pl.BlockSpec((1, HEAD_DIM), lambda b: (0, 0)),
        ],
        out_specs=pl.BlockSpec((seq, ATTN_W), lambda b: (b, 0)),
        compiler_params=_params("arbitrary"),
        name="attn_ctx",
    )(proj, k, v, proj, q_norm.reshape(1, HEAD_DIM))


def _attn_lat_kernel(q_ref, k_ref, v_ref, ck_ref, cv_ref, ag_ref, qn_ref, cosq_ref, sinq_ref,
                     cosk_ref, sink_ref, o_ref, k_s, v_s, *, seq):
    @pl.when(pl.program_id(2) == 0)
    def _():
        k_s[:seq] = _rope(k_ref[...], cosk_ref[...], sink_ref[...]).astype(BF16)
        k_s[seq:] = ck_ref[...].astype(BF16)
        v_s[:seq] = v_ref[...].astype(BF16)
        v_s[seq:] = cv_ref[...].astype(BF16)

    k = k_s[...]
    v = v_s[...]
    for g in range(KV_GROUP):
        sl = slice(g * HEAD_DIM, (g + 1) * HEAD_DIM)
        for r0 in range(0, q_ref.shape[0], ATTN_ROWS):
            rows = slice(r0, r0 + ATTN_ROWS)
            q = _norm_head(q_ref[rows, sl].astype(F32), qn_ref[...])
            q = _rope(q, cosq_ref[rows, :], sinq_ref[rows, :]) * Q_SCALE
            o = _softmax_pv(_qk(q.astype(BF16), k), v)
            o_ref[rows, sl] = (o * _silu(ag_ref[rows, sl].astype(F32))).astype(BF16)


def _attention_lat(proj, k, v, cache_k, cache_v, layer, q_norm, cos2, sin2, n_ctx, n_seq, seq, tq=512):
    past = cache_k.shape[2]
    ck = cache_k.reshape(cache_k.shape[0], DEPTH, past, KV_W)
    cv = cache_v.reshape(cache_v.shape[0], DEPTH, past, KV_W)
    nq = seq // tq
    row0, seq0 = n_ctx // tq, n_ctx // seq
    return pl.pallas_call(
        functools.partial(_attn_lat_kernel, seq=seq),
        out_shape=jax.ShapeDtypeStruct((n_seq * seq, ATTN_W), BF16),
        grid=(n_seq, N_KV_HEADS, nq),
        in_specs=[
            pl.BlockSpec((tq, GROUP_W), lambda b, h, i: (row0 + b * nq + i, OFF_Q // GROUP_W + h)),
            pl.BlockSpec((seq, HEAD_DIM), lambda b, h, i: (seq0 + b, h)),
            pl.BlockSpec((seq, HEAD_DIM), lambda b, h, i: (seq0 + b, h)),
            pl.BlockSpec((None, None, past, HEAD_DIM), lambda b, h, i: (b, layer, 0, h)),
            pl.BlockSpec((None, None, past, HEAD_DIM), lambda b, h, i: (b, layer, 0, h)),
            pl.BlockSpec((tq, GROUP_W), lambda b, h, i: (row0 + b * nq + i, OFF_AGATE // GROUP_W + h)),
            pl.BlockSpec((1, HEAD_DIM), lambda b, h, i: (0, 0)),
            pl.BlockSpec((tq, HEAD_DIM), lambda b, h, i: (i, 0)),
            pl.BlockSpec((tq, HEAD_DIM), lambda b, h, i: (i, 0)),
            pl.BlockSpec((seq, HEAD_DIM), lambda b, h, i: (0, 0)),
            pl.BlockSpec((seq, HEAD_DIM), lambda b, h, i: (0, 0)),
        ],
        out_specs=pl.BlockSpec((tq, GROUP_W), lambda b, h, i: (b * nq + i, h)),
        scratch_shapes=[pltpu.VMEM((seq + past, HEAD_DIM), BF16), pltpu.VMEM((seq + past, HEAD_DIM), BF16)],
        compiler_params=_params("arbitrary", "arbitrary", "arbitrary"),
        name="attn_lat",
    )(proj, k, v, ck, cv, proj, q_norm.reshape(1, HEAD_DIM), cos2, sin2, cos2, sin2)


def _rope_tables(seq):
    rows = seq // GRID_W
    row = jnp.repeat(jnp.arange(rows, dtype=F32), GRID_W)
    col = jnp.tile(jnp.arange(GRID_W, dtype=F32), rows)
    inv = ROPE_THETA ** (-jnp.arange(ROPE_PAIRS_AXIS, dtype=F32) / ROPE_PAIRS_AXIS)
    ang = jnp.concatenate([row[:, None] * inv, col[:, None] * inv], axis=-1)
    cos, sin = jnp.cos(ang), jnp.sin(ang)
    return jnp.concatenate([cos, cos], axis=-1), jnp.concatenate([-sin, sin], axis=-1)


def _dft_angle(i, j, n):
    return (2.0 * math.pi / n) * ((i[:, None] * j[None, :]) % n).astype(F32)


def _cos_sin_table(rows, cols, n):
    c = jnp.arange(cols, dtype=jnp.int32)
    hi = _dft_angle(TABLE_SPLIT * jnp.arange(rows // TABLE_SPLIT, dtype=jnp.int32), c, n)
    lo = _dft_angle(jnp.arange(TABLE_SPLIT, dtype=jnp.int32), c, n)
    ch, sh = jnp.cos(hi)[:, None, :], jnp.sin(hi)[:, None, :]
    cl, sl = jnp.cos(lo)[None, :, :], jnp.sin(lo)[None, :, :]
    return (ch * cl - sh * sl).reshape(rows, cols), (sh * cl + ch * sl).reshape(rows, cols)


def _fnet_kernel(u_ref, fg_ref, csc_ref, ml_ref, o_ref, t_s, *, seq, scale):
    @pl.when(pl.program_id(1) == 0)
    def _():
        for g in range(FNET_GROUPS):
            sl = slice(g * FNET_GROUP_W, (g + 1) * FNET_GROUP_W)
            t = _dot(u_ref[:, sl], csc_ref[...])
            t_s[:seq, sl] = t[:, :FNET_GROUP_W].astype(BF16)
            t_s[seq:, sl] = t[:, FNET_GROUP_W:].astype(BF16)

    y = _dot(ml_ref[...], t_s[...]) * scale
    o_ref[...] = (y * _silu(fg_ref[...].astype(F32))).astype(BF16)


def _fnet_tables(seq):
    j = jnp.arange(FNET_GROUP_W, dtype=jnp.int32)
    ang = _dft_angle(j, j, FNET_GROUP_W)
    csc = jnp.concatenate([jnp.cos(ang), jnp.sin(ang)], axis=1).astype(BF16)
    cos, sin = _cos_sin_table(seq, seq, seq)
    ml = jnp.concatenate([cos, -sin], axis=1).astype(BF16)
    return csc, ml


def _fnet(proj, tables, row_start, n_seq, seq, tr=256):
    csc, ml = tables
    nr = seq // tr
    s0, r0 = row_start // seq, row_start // tr
    return pl.pallas_call(
        functools.partial(_fnet_kernel, seq=seq, scale=1.0 / math.sqrt(seq * FNET_GROUP_W)),
        out_shape=jax.ShapeDtypeStruct((n_seq * seq, FNET_W), BF16),
        grid=(n_seq, nr),
        in_specs=[
            pl.BlockSpec((seq, FNET_W), lambda b, r: (s0 + b, OFF_FIN // FNET_W)),
            pl.BlockSpec((tr, FNET_W), lambda b, r: (r0 + b * nr + r, OFF_FGATE // FNET_W)),
            pl.BlockSpec((FNET_GROUP_W, 2 * FNET_GROUP_W), lambda b, r: (0, 0)),
            pl.BlockSpec((tr, 2 * seq), lambda b, r: (r, 0)),
        ],
        out_specs=pl.BlockSpec((tr, FNET_W), lambda b, r: (b * nr + r, 0)),
        scratch_shapes=[pltpu.VMEM((2 * seq, FNET_W), BF16)],
        compiler_params=_params("arbitrary", "arbitrary"),
        name=f"fnet_{seq}",
    )(proj, proj, csc, ml)


def _short_kernel(x_ref, w_ref, b_ref, o_ref, *, seq):
    x = x_ref[...].astype(F32)
    n = x.shape[0]
    pos = lax.broadcasted_iota(jnp.int32, x.shape, 0) % seq
    prev = jnp.where(pos == 0, 0.0, pltpu.roll(x, 1, 0))
    nxt = jnp.where(pos == seq - 1, 0.0, pltpu.roll(x, n - 1, 0))
    y = b_ref[...] + prev * w_ref[0:1, :] + x * w_ref[1:2, :] + nxt * w_ref[2:3, :]
    o_ref[...] = y.astype(BF16)


def _short_conv(proj, w, b, row_start, n_rows, seq, tr=2048, tw=512):
    width = 3 * HYENA_W
    r0 = row_start // tr
    return pl.pallas_call(
        functools.partial(_short_kernel, seq=seq),
        out_shape=jax.ShapeDtypeStruct((n_rows, width), BF16),
        grid=(n_rows // tr, width // tw),
        in_specs=[
            pl.BlockSpec((tr, tw), lambda s, c: (r0 + s, OFF_HV // tw + c)),
            pl.BlockSpec((HYENA_SHORT, tw), lambda s, c: (0, c)),
            pl.BlockSpec((1, tw), lambda s, c: (0, c)),
        ],
        out_specs=pl.BlockSpec((tr, tw), lambda s, c: (s, c)),
        compiler_params=_params("arbitrary", "arbitrary"),
        name=f"short_conv_{seq}",
    )(proj, w, b.reshape(1, width))


def _filt_mlp_kernel(feats_ref, w1_ref, b1_ref, w2_ref, b2_ref, fr_ref, o_ref):
    fr = fr_ref[...]
    h = jnp.sin(fr * (_dot_exact(feats_ref[...], w1_ref[...]) + b1_ref[...]))
    o_ref[...] = jnp.sin(fr * (_dot_exact(h, w2_ref[...]) + b2_ref[...]))


def _filt_kernel(h_ref, t_ref, w3f_ref, w3b_ref, b3f_ref, b3b_ref, dl_ref, of_ref, ob_ref):
    h = h_ref[...]
    decay = jnp.exp(-t_ref[...] * dl_ref[...])
    hf = (_dot_exact(h, w3f_ref[...]) + b3f_ref[...]) * decay
    hb = (_dot_exact(h, w3b_ref[...]) + b3b_ref[...]) * decay
    den = jnp.sum(jnp.abs(hf), axis=0, keepdims=True) + jnp.sum(jnp.abs(hb), axis=0, keepdims=True) + EPS
    of_ref[...] = (hf / den).astype(BF16)
    ob_ref[...] = (hb / den).astype(BF16)


def _hyena_filters(seq, w1, b1, w2, b2, w3, b3, freq, tw=512):
    t = jnp.arange(seq, dtype=F32)[:, None] / seq
    bands = jnp.arange(1, HYENA_BANDS + 1, dtype=F32)[None, :]
    feats = jnp.concatenate([t, jnp.cos(2 * math.pi * t * bands), jnp.sin(2 * math.pi * t * bands),
                             jnp.zeros((seq, HYENA_POS_PAD - HYENA_POS_DIM), F32)], axis=-1)
    w1p = jnp.concatenate([w1, jnp.zeros((HYENA_POS_PAD - HYENA_POS_DIM, HYENA_FFN_W), F32)], axis=0)
    deltas = jnp.abs(jnp.linspace(HYENA_MIN_DECAY, HYENA_MAX_DECAY, HYENA_W, dtype=F32))[None, :]
    hidden = pl.pallas_call(
        _filt_mlp_kernel,
        out_shape=jax.ShapeDtypeStruct((seq, HYENA_FFN_W), F32),
        name=f"hyena_filter_mlp_{seq}",
    )(feats, w1p, b1.reshape(1, -1), w2, b2.reshape(1, -1), freq.reshape(1, -1))
    nw = HYENA_W // tw
    small = lambda shape: pl.BlockSpec(shape, lambda o, c: (0, 0))
    out_sds = jax.ShapeDtypeStruct((seq, HYENA_ORDER * HYENA_W), BF16)
    fwd = lambda o, c: (0, (2 * o) * nw + c)
    bwd = lambda o, c: (0, (2 * o + 1) * nw + c)
    b3r = b3.reshape(1, -1)
    return pl.pallas_call(
        _filt_kernel,
        out_shape=(out_sds, out_sds),
        grid=(HYENA_ORDER, nw),
        in_specs=[
            small((seq, HYENA_FFN_W)), small((seq, 1)),
            pl.BlockSpec((HYENA_FFN_W, tw), fwd), pl.BlockSpec((HYENA_FFN_W, tw), bwd),
            pl.BlockSpec((1, tw), fwd), pl.BlockSpec((1, tw), bwd),
            pl.BlockSpec((1, tw), lambda o, c: (0, c)),
        ],
        out_specs=(pl.BlockSpec((seq, tw), lambda o, c: (0, o * nw + c)),
                   pl.BlockSpec((seq, tw), lambda o, c: (0, o * nw + c))),
        compiler_params=_params("arbitrary", "arbitrary"),
        name=f"hyena_filters_{seq}",
    )(hidden, t, w3, w3, b3r, b3r, deltas)


def _conv_tables(seq):
    i = jnp.arange(seq, dtype=jnp.int32)
    nyq = (1 - 2 * (i % 2)).astype(F32)
    cos, sin = _cos_sin_table(seq, seq, 2 * seq)
    fwd = jnp.concatenate([cos, jnp.where(i[:, None] == 0, nyq[None, :], sin)], axis=0)
    sin_t = jnp.where(i[None, :] == 0, nyq[:, None], sin)
    return fwd.astype(BF16), sin_t.astype(BF16)


def _filter_bins(ucf_ref, ucb_ref, usf_ref, usb_ref, first_chunk):
    usf, usb = usf_ref[...], usb_ref[...]
    gr = ucf_ref[...] + ucb_ref[...]
    gi = usf - usb
    packed = jnp.logical_and(lax.broadcasted_iota(jnp.int32, gr.shape, 0) == 0, first_chunk)
    gi = jnp.where(packed, 0.0, gi)
    gn = jnp.where(packed, usf + usb, gr)
    wgt = jnp.where(packed, 1.0, 2.0)
    return gr * wgt, gi * wgt, gn * wgt


def _spectrum_product(cos_ref, sin_ref, z, bins):
    gr, gi, gn = bins
    zr = _dot(cos_ref[...], z)
    zi = _dot(sin_ref[...], z)
    return (zr * gr - zi * gi).astype(BF16), (zr * gi + zi * gn).astype(BF16)


def _conv_epilogue(y, z, bias_ref, xm, gate, n):
    y = (y * (1.0 / n) + z.astype(F32) * bias_ref[...]) * xm.astype(F32)
    if gate is not None:
        y = y * _silu(gate.astype(F32))
    return y.astype(BF16)


def _lconv_short_kernel(*refs, seq, n_seq_blk, gated):
    refs = list(refs)
    z_ref, xm_ref = refs[:2]
    gate_ref = refs[2] if gated else None
    bias_ref, cos_ref, sin_ref, sin_t_ref, ucf_ref, ucb_ref, usf_ref, usb_ref, o_ref = refs[2 + gated:]
    bins = _filter_bins(ucf_ref, ucb_ref, usf_ref, usb_ref, True)
    for s in range(n_seq_blk):
        rows = slice(s * seq, (s + 1) * seq)
        z = z_ref[rows, :]
        pr, pi = _spectrum_product(cos_ref, sin_ref, z, bins)
        y = _dot(cos_ref[...], pr) + _dot(sin_t_ref[...], pi)
        o_ref[rows, :] = _conv_epilogue(y, z, bias_ref, xm_ref[rows, :],
                                        gate_ref[rows, :] if gated else None, 2 * seq)


def _lconv_long_kernel(*refs, seq, chunk, gated):
    refs = list(refs)
    z_ref, xm_ref = refs[:2]
    gate_ref = refs[2] if gated else None
    (bias_ref, cos_ref, sin_ref, sin_t_ref, ucf_ref, ucb_ref, usf_ref, usb_ref,
     o_ref, pr_s, pi_s) = refs[2 + gated:]
    n_chunks = seq // chunk
    p = pl.program_id(2)

    @pl.when(p < n_chunks)
    def _():
        bins = _filter_bins(ucf_ref, ucb_ref, usf_ref, usb_ref, p == 0)
        pr, pi = _spectrum_product(cos_ref, sin_ref, z_ref[...], bins)
        rows = pl.ds(pl.multiple_of(p * chunk, chunk), chunk)
        pr_s[rows, :] = pr
        pi_s[rows, :] = pi

    @pl.when(p >= n_chunks)
    def _():
        y = _dot(cos_ref[...], pr_s[...]) + _dot(sin_t_ref[...], pi_s[...])
        rows = pl.ds(pl.multiple_of((p - n_chunks) * chunk, chunk), chunk)
        o_ref[...] = _conv_epilogue(y, z_ref[rows, :], bias_ref, xm_ref[...],
                                    gate_ref[...] if gated else None, 2 * seq)


def _long_conv(z, z_col, xm, xm_col, gate, gate_row0, bias, idx, order, tables, spec_f, spec_b,
               n_seq, seq, tw, fc, seq_blk):
    fwd, sin_t = tables
    nw, nf = HYENA_W // tw, seq // fc
    gated = gate is not None
    out_shape = jax.ShapeDtypeStruct((n_seq * seq, HYENA_W), BF16)
    bias_spec = pl.BlockSpec((None, 1, tw), lambda s, w, p: (idx, 0, w))
    if nf == 1:
        tr = seq_blk * seq
        tok = lambda col, r0=0: pl.BlockSpec((tr, tw), lambda s, w, p: (r0 + s, col * nw + w))
        taps = lambda half: pl.BlockSpec((seq, tw), lambda s, w, p: (half, order * nw + w))
        table = lambda half: pl.BlockSpec((seq, seq), lambda s, w, p: (half, 0))
        in_specs = [tok(z_col), tok(xm_col)] + ([tok(OFF_HGATE // HYENA_W, gate_row0 // tr)] if gated else [])
        in_specs += [bias_spec, table(0), table(1), table(0), taps(0), taps(0), taps(1), taps(1)]
        kern = functools.partial(_lconv_short_kernel, seq=seq, n_seq_blk=seq_blk, gated=gated)
        grid, scratch = (n_seq // seq_blk, nw, 1), []
        out_spec = pl.BlockSpec((tr, tw), lambda s, w, p: (s, w))
    else:
        assert seq_blk == 1
        fwd_chunk = lambda p: jnp.minimum(p, nf - 1)
        out_chunk = lambda p: jnp.maximum(p - nf, 0)
        rows = lambda col, r0=0: pl.BlockSpec((fc, tw), lambda s, w, p: (r0 + s * nf + out_chunk(p), col * nw + w))
        taps = lambda half: pl.BlockSpec((fc, tw), lambda s, w, p: (half * nf + fwd_chunk(p), order * nw + w))
        in_specs = [pl.BlockSpec((seq, tw), lambda s, w, p: (s, z_col * nw + w)), rows(xm_col)]
        in_specs += [rows(OFF_HGATE // HYENA_W, gate_row0 // fc)] if gated else []
        in_specs += [
            bias_spec,
            pl.BlockSpec((fc, seq), lambda s, w, p: (p % nf, 0)),
            pl.BlockSpec((fc, seq), lambda s, w, p: (nf + fwd_chunk(p), 0)),
            pl.BlockSpec((fc, seq), lambda s, w, p: (out_chunk(p), 0)),
            taps(0), taps(0), taps(1), taps(1),
        ]
        kern = functools.partial(_lconv_long_kernel, seq=seq, chunk=fc, gated=gated)
        grid, scratch = (n_seq, nw, 2 * nf), [pltpu.VMEM((seq, tw), BF16), pltpu.VMEM((seq, tw), BF16)]
        out_spec = pl.BlockSpec((fc, tw), lambda s, w, p: (s * nf + out_chunk(p), w))
    args = [z, xm] + ([gate] if gated else []) + [bias, fwd, fwd, sin_t, spec_f, spec_b, spec_f, spec_b]
    return pl.pallas_call(
        kern,
        out_shape=out_shape,
        grid=grid,
        in_specs=in_specs,
        out_specs=out_spec,
        scratch_shapes=scratch,
        compiler_params=_params("arbitrary", "arbitrary", "arbitrary"),
        name=f"long_conv_{seq}_{order}",
    )(*args)


def _merge_kernel(ac_ref, al_ref, fc_ref, fl_ref, hc_ref, hl_ref, wa_ref, wf_ref, wh_ref,
                  ga_ref, gf_ref, gh_ref, o_ref, wab_ref, wfb_ref, whb_ref, *, n_ctx_tiles):
    i = pl.program_id(1)

    @pl.when(i == 0)
    def _():
        wab_ref[...] = wa_ref[...].astype(BF16)
        wfb_ref[...] = wf_ref[...].astype(BF16)
        whb_ref[...] = wh_ref[...].astype(BF16)

    def body(a_ref, f_ref, h_ref):
        m = jax.nn.sigmoid(ga_ref[...].astype(F32)) * _dot(a_ref[...], wab_ref[...])
        m = m + jax.nn.sigmoid(gf_ref[...].astype(F32)) * _dot(f_ref[...], wfb_ref[...])
        m = m + jax.nn.sigmoid(gh_ref[...].astype(F32)) * _dot(h_ref[...], whb_ref[...])
        o_ref[...] = m.astype(BF16)

    _by_group(i, n_ctx_tiles, body, (ac_ref, fc_ref, hc_ref), (al_ref, fl_ref, hl_ref))


def _merge(attn, fnet, hy, w_a, w_f, w_h, layer, proj, n_ctx, tm=512, tn=1024):
    t = proj.shape[0]
    nct = n_ctx // tm
    act = lambda w: _pair_specs((tm, w), nct, inner=True)
    wgt = lambda k: pl.BlockSpec((None, k, tn), lambda j, i: (layer, 0, j), pipeline_mode=pl.Buffered(1))
    gate = lambda off: pl.BlockSpec((tm, tn), lambda j, i: (i, off // tn + j))
    return pl.pallas_call(
        functools.partial(_merge_kernel, n_ctx_tiles=nct),
        out_shape=jax.ShapeDtypeStruct((t, D_MODEL), BF16),
        grid=(D_MODEL // tn, t // tm),
        in_specs=act(ATTN_W) + act(FNET_W) + act(HYENA_W) + [wgt(ATTN_W), wgt(FNET_W), wgt(HYENA_W),
                                                             gate(OFF_GA), gate(OFF_GF), gate(OFF_GH)],
        out_specs=pl.BlockSpec((tm, tn), lambda j, i: (i, j)),
        scratch_shapes=[pltpu.VMEM((ATTN_W, tn), BF16), pltpu.VMEM((FNET_W, tn), BF16),
                        pltpu.VMEM((HYENA_W, tn), BF16)],
        compiler_params=_params("arbitrary", "arbitrary"),
        name="merge",
    )(*attn, *fnet, *hy, w_a, w_f, w_h, proj, proj, proj)


def kernel(x_prompt, x_sample, cache_k, cache_v, c, c_ctx, w_mod, b_mod, g_pre, w_in, q_norm, k_norm, hy_short_w, hy_short_b, hy_ffn_w1, hy_ffn_b1, hy_ffn_w2, hy_ffn_b2, hy_ffn_w3, hy_ffn_b3, hy_sin_freq, hy_bias, w_attn_o, w_fnet_o, w_hy_o, w_out, g_post):
    nb, seq, _ = x_prompt.shape
    db, dseq, _ = x_sample.shape
    assert 1 + db <= MOD_ROWS
    n_ctx, n_lat = nb * seq, db * dseq

    cond = jnp.concatenate([c_ctx[None, :], c, jnp.zeros((MOD_ROWS - 1 - db, D_MODEL), F32)], axis=0)
    mod = _modulation(cond, w_mod, b_mod)

    w_kv_b = w_in[:, :, W_OFF_K:W_OFF_K + W_KV].astype(BF16)
    cos2, sin2 = _rope_tables(dseq)
    fnet_tables = {s: _fnet_tables(s) for s in (seq, dseq)}
    conv_tables = {s: _conv_tables(s) for s in (seq, dseq)}
    bias = hy_bias.reshape(DEPTH * HYENA_ORDER, 1, HYENA_W)
    conv_tiles = {seq: (HYENA_W, seq, 4), dseq: (512, 512, 1)}
    proj_tm, proj_tn = 512, 1024
    skip_kv = lambda j: jnp.where(j >= W_OFF_K // proj_tn, j + W_KV // proj_tn, j)

    xs = (x_prompt.reshape(n_ctx, D_MODEL), 0, x_sample.reshape(n_lat, D_MODEL), 0)
    new_k, new_v = [], []
    mod3s = [mod[l].reshape(MOD_ROWS, 1, 3 * D_MODEL) for l in range(DEPTH)]
    h = _prenorm(*xs, g_pre[0], mod3s[0], n_ctx, n_lat, dseq)
    for l in range(DEPTH):
        mod3 = mod3s[l]
        proj = _matmul_wcast(h, w_in, l, BF16, proj_tm, proj_tn, "in_proj", n_out=PROJ_W, col_map=skip_kv)
        k, v = _kv_proj(h, w_kv_b, l, k_norm[l])
        new_k.append(k[:n_ctx].reshape(nb, seq, N_KV_HEADS, HEAD_DIM))
        new_v.append(v[:n_ctx].reshape(nb, seq, N_KV_HEADS, HEAD_DIM))

        attn = (_attention_ctx(proj, k, v, q_norm[l], nb, seq),
                _attention_lat(proj, k, v, cache_k, cache_v, l, q_norm[l], cos2, sin2, n_ctx, db, dseq))

        fnet, hy = [], []
        for row_start, n_seq, s in ((0, nb, seq), (n_ctx, db, dseq)):
            fnet.append(_fnet(proj, fnet_tables[s], row_start, n_seq, s))
            u = _short_conv(proj, hy_short_w[l], hy_short_b[l], row_start, n_seq * s, s)
            filt_f, filt_b = _hyena_filters(s, hy_ffn_w1[l], hy_ffn_b1[l], hy_ffn_w2[l], hy_ffn_b2[l],
                                            hy_ffn_w3[l], hy_ffn_b3[l], hy_sin_freq[l])
            fwd = conv_tables[s][0]
            spec_tm = min(1024, 2 * s)
            spec_f = _matmul(fwd, filt_f, F32, spec_tm, 1024, f"filter_spectrum_fwd_{s}")
            spec_b = _matmul(fwd, filt_b, F32, spec_tm, 1024, f"filter_spectrum_bwd_{s}")
            tw, fc, blk = conv_tiles[s]
            z1 = _long_conv(u, 0, u, 1, None, 0, bias, l * HYENA_ORDER, 0, conv_tables[s],
                            spec_f, spec_b, n_seq, s, tw, fc, blk)
            hy.append(_long_conv(z1, 0, u, 2, proj, row_start, bias, l * HYENA_ORDER + 1, 1, conv_tables[s],
                                 spec_f, spec_b, n_seq, s, tw, fc, blk))

        merged = _merge(attn, fnet, hy, w_attn_o, w_fnet_o, w_hy_o, l, proj, n_ctx)
        r = _matmul_wcast(merged, w_out, l, BF16, proj_tm, proj_tn, "out_proj")
        if l < DEPTH - 1:
            x_new, h = _post(*xs, r, 0, g_post[l], mod3, n_ctx, n_lat, dseq, next_norm=(g_pre[l + 1], mod3s[l + 1]))
            xs = (x_new, 0, x_new, n_ctx)
        else:
            y_prompt = _post(*xs, r, 0, g_post[l], mod3, n_ctx, 0, dseq)
            y_sample = _post(*xs, r, n_ctx, g_post[l], mod3, 0, n_lat, dseq)

    return (y_prompt.reshape(nb, seq, D_MODEL), y_sample.reshape(db, dseq, D_MODEL),
            jnp.stack(new_k, axis=1), jnp.stack(new_v, axis=1))
```

```python
import functools
import math

import jax
import jax.numpy as jnp
from jax import lax
from jax.experimental import pallas as pl
from jax.experimental.pallas import tpu as pltpu

F32 = jnp.float32
BF16 = jnp.bfloat16

D_MODEL = 4096
DEPTH = 2
GRID_W = 64
HEAD_DIM = 128
N_HEADS = 16
N_KV_HEADS = 4
KV_GROUP = N_HEADS // N_KV_HEADS
ATTN_W = N_HEADS * HEAD_DIM
KV_W = N_KV_HEADS * HEAD_DIM
GROUP_W = KV_GROUP * HEAD_DIM
ATTN_SCALE = HEAD_DIM ** -0.5
Q_SCALE = ATTN_SCALE * math.log2(math.e)
ROPE_THETA = 10000.0
ROPE_PAIRS_AXIS = HEAD_DIM // 4
FNET_W = D_MODEL // 4
FNET_GROUPS = 4
FNET_GROUP_W = FNET_W // FNET_GROUPS
HYENA_W = D_MODEL // 4
HYENA_ORDER = 2
HYENA_SHORT = 3
HYENA_BANDS = 16
HYENA_POS_DIM = 1 + 2 * HYENA_BANDS
HYENA_POS_PAD = 128
HYENA_FFN_W = 64
HYENA_MIN_DECAY = math.log(1e-2) / 0.3
HYENA_MAX_DECAY = math.log(1e-2) / 1.5
EPS = 1e-6

W_OFF_K = ATTN_W
W_KV = 2 * KV_W
OFF_Q = 0
OFF_AGATE = OFF_Q + ATTN_W
OFF_FIN = OFF_AGATE + ATTN_W
OFF_FGATE = OFF_FIN + FNET_W
OFF_HV = OFF_FGATE + FNET_W
OFF_HX1 = OFF_HV + HYENA_W
OFF_HX2 = OFF_HX1 + HYENA_W
OFF_HGATE = OFF_HX2 + HYENA_W
OFF_GA = OFF_HGATE + HYENA_W
OFF_GF = OFF_GA + D_MODEL
OFF_GH = OFF_GF + D_MODEL
PROJ_W = OFF_GH + D_MODEL

MOD_ROWS = 8
ROW_TILE = 256
ATTN_ROWS = 256
TABLE_SPLIT = 64
VMEM_LIMIT = 56 * 1024 * 1024


def _params(*sem):
    return pltpu.CompilerParams(dimension_semantics=sem, vmem_limit_bytes=VMEM_LIMIT)


def _silu(x):
    return x * jax.nn.sigmoid(x)


def _dot(a, b):
    return jnp.dot(a, b, preferred_element_type=F32)


def _dot_exact(a, b):
    return jnp.dot(a, b, preferred_element_type=F32, precision=lax.Precision.HIGHEST)


def _pair_specs(block, n_ctx_tiles, ctx_tile0=0, lat_tile0=0, col=0, inner=False):
    def ctx(i):
        return (ctx_tile0 + jnp.clip(i, 0, max(n_ctx_tiles - 1, 0)), col)

    def lat(i):
        return (lat_tile0 + jnp.maximum(i - n_ctx_tiles, 0), col)

    if inner:
        return [pl.BlockSpec(block, lambda j, i: ctx(i)), pl.BlockSpec(block, lambda j, i: lat(i))]
    return [pl.BlockSpec(block, lambda i: ctx(i)), pl.BlockSpec(block, lambda i: lat(i))]


def _by_group(i, n_ctx_tiles, body, ctx_refs, lat_refs):
    @pl.when(i < n_ctx_tiles)
    def _():
        body(*ctx_refs)

    @pl.when(i >= n_ctx_tiles)
    def _():
        body(*lat_refs)


def _mod_kernel(c_ref, w_ref, b_ref, o_ref):
    s = _silu(c_ref[...]).astype(BF16)
    o_ref[0] = _dot(s, w_ref[0].astype(BF16)) + b_ref[0]


def _modulation(cond, w_mod, b_mod, tn=512):
    n = w_mod.shape[2]
    return pl.pallas_call(
        _mod_kernel,
        out_shape=jax.ShapeDtypeStruct((DEPTH, MOD_ROWS, n), F32),
        grid=(DEPTH, n // tn),
        in_specs=[
            pl.BlockSpec((MOD_ROWS, D_MODEL), lambda l, j: (0, 0)),
            pl.BlockSpec((1, D_MODEL, tn), lambda l, j: (l, 0, j)),
            pl.BlockSpec((1, 1, tn), lambda l, j: (l, 0, j)),
        ],
        out_specs=pl.BlockSpec((1, MOD_ROWS, tn), lambda l, j: (l, 0, j)),
        compiler_params=_params("arbitrary", "arbitrary"),
        name="modulation",
    )(cond, w_mod, b_mod.reshape(DEPTH, 1, n))


def _mod_spec(chunk, n_ctx_tiles, tiles_per_latent_seq):
    def row(i):
        return jnp.where(i < n_ctx_tiles, 0, 1 + (i - n_ctx_tiles) // tiles_per_latent_seq)
    return pl.BlockSpec((1, 1, D_MODEL), lambda i: (row(i), 0, chunk))


def _modulated_norm(x, g_ref, shift_ref, scale_ref):
    y = x * lax.rsqrt(jnp.mean(x * x, axis=-1, keepdims=True) + EPS) * g_ref[...]
    return (y * (1.0 + scale_ref[0]) + shift_ref[0]).astype(BF16)


def _prenorm_kernel(xc_ref, xl_ref, g_ref, shift_ref, scale_ref, o_ref, *, n_ctx_tiles):
    def body(x_ref):
        o_ref[...] = _modulated_norm(x_ref[...], g_ref, shift_ref, scale_ref)

    _by_group(pl.program_id(0), n_ctx_tiles, body, (xc_ref,), (xl_ref,))


def _prenorm(x_ctx, ctx_row0, x_lat, lat_row0, g, mod3, n_ctx, n_lat, lat_seq):
    tm = ROW_TILE
    nct = n_ctx // tm
    mod = functools.partial(_mod_spec, n_ctx_tiles=nct, tiles_per_latent_seq=lat_seq // tm)
    return pl.pallas_call(
        functools.partial(_prenorm_kernel, n_ctx_tiles=nct),
        out_shape=jax.ShapeDtypeStruct((n_ctx + n_lat, D_MODEL), BF16),
        grid=((n_ctx + n_lat) // tm,),
        in_specs=_pair_specs((tm, D_MODEL), nct, ctx_row0 // tm, lat_row0 // tm)
        + [pl.BlockSpec((1, D_MODEL), lambda i: (0, 0)), mod(0), mod(1)],
        out_specs=pl.BlockSpec((tm, D_MODEL), lambda i: (i, 0)),
        compiler_params=_params("arbitrary"),
        name="prenorm",
    )(x_ctx, x_lat, g.reshape(1, D_MODEL), mod3, mod3)


def _post_kernel(xc_ref, xl_ref, r_ref, g_ref, gate_ref, *rest, n_ctx_tiles, with_next):
    if with_next:
        g_next_ref, shift_ref, scale_ref, o_ref, h_ref = rest
    else:
        (o_ref,) = rest

    def body(x_ref):
        r = r_ref[...].astype(F32)
        y = r * lax.rsqrt(jnp.mean(r * r, axis=-1, keepdims=True) + EPS) * g_ref[...]
        x = x_ref[...] + gate_ref[0] * y
        o_ref[...] = x
        if with_next:
            h_ref[...] = _modulated_norm(x, g_next_ref, shift_ref, scale_ref)

    _by_group(pl.program_id(0), n_ctx_tiles, body, (xc_ref,), (xl_ref,))


def _post(x_ctx, ctx_row0, x_lat, lat_row0, r, r_row0, g, mod3, n_ctx, n_lat, lat_seq, next_norm=None):
    tm = ROW_TILE
    nct = n_ctx // tm
    r0 = r_row0 // tm
    rows = n_ctx + n_lat
    mod = functools.partial(_mod_spec, n_ctx_tiles=nct, tiles_per_latent_seq=lat_seq // tm)
    vec = pl.BlockSpec((1, D_MODEL), lambda i: (0, 0))
    tile = pl.BlockSpec((tm, D_MODEL), lambda i: (i, 0))
    in_specs = _pair_specs((tm, D_MODEL), nct, ctx_row0 // tm, lat_row0 // tm) + [
        pl.BlockSpec((tm, D_MODEL), lambda i: (r0 + i, 0)), vec, mod(2)]
    args = [x_ctx, x_lat, r, g.reshape(1, D_MODEL), mod3]
    out_shape, out_specs = jax.ShapeDtypeStruct((rows, D_MODEL), F32), tile
    if next_norm is not None:
        g_next, mod3_next = next_norm
        in_specs += [vec, mod(0), mod(1)]
        args += [g_next.reshape(1, D_MODEL), mod3_next, mod3_next]
        out_shape, out_specs = (out_shape, jax.ShapeDtypeStruct((rows, D_MODEL), BF16)), (tile, tile)
    return pl.pallas_call(
        functools.partial(_post_kernel, n_ctx_tiles=nct, with_next=next_norm is not None),
        out_shape=out_shape,
        grid=(rows // tm,),
        in_specs=in_specs,
        out_specs=out_specs,
        compiler_params=_params("arbitrary"),
        name="post",
    )(*args)


def _mm_kernel(a_ref, b_ref, o_ref):
    o_ref[...] = _dot(a_ref[...], b_ref[...]).astype(o_ref.dtype)


def _matmul(a, b, out_dtype, tm, tn, name):
    m, k = a.shape
    n = b.shape[1]
    return pl.pallas_call(
        _mm_kernel,
        out_shape=jax.ShapeDtypeStruct((m, n), out_dtype),
        grid=(m // tm, n // tn),
        in_specs=[pl.BlockSpec((tm, k), lambda i, j: (i, 0)), pl.BlockSpec((k, tn), lambda i, j: (0, j))],
        out_specs=pl.BlockSpec((tm, tn), lambda i, j: (i, j)),
        compiler_params=_params("arbitrary", "arbitrary"),
        name=name,
    )(a, b)


def _mm_wcast_kernel(a_ref, w_ref, o_ref, wb_ref):
    @pl.when(pl.program_id(1) == 0)
    def _():
        wb_ref[...] = w_ref[...].astype(BF16)

    o_ref[...] = _dot(a_ref[...], wb_ref[...]).astype(o_ref.dtype)


def _matmul_wcast(a, w, layer, out_dtype, tm, tn, name, n_out=None, col_map=None):
    m, k = a.shape
    n = n_out or w.shape[-1]
    col = col_map or (lambda j: j)
    return pl.pallas_call(
        _mm_wcast_kernel,
        out_shape=jax.ShapeDtypeStruct((m, n), out_dtype),
        grid=(n // tn, m // tm),
        in_specs=[
            pl.BlockSpec((tm, k), lambda j, i: (i, 0)),
            pl.BlockSpec((None, k, tn), lambda j, i: (layer, 0, col(j))),
        ],
        out_specs=pl.BlockSpec((tm, tn), lambda j, i: (i, j)),
        scratch_shapes=[pltpu.VMEM((k, tn), BF16)],
        compiler_params=_params("arbitrary", "arbitrary"),
        name=name,
    )(a, w)


def _kv_kernel(h_ref, w_ref, kn_ref, k_ref, v_ref):
    acc = _dot(h_ref[...], w_ref[...])
    for hd in range(N_KV_HEADS):
        sl = slice(hd * HEAD_DIM, (hd + 1) * HEAD_DIM)
        k_ref[:, sl] = _norm_head(acc[:, sl], kn_ref[...])
    v_ref[...] = acc[:, KV_W:]


def _kv_proj(h, w_kv, layer, k_norm, tm=512):
    t = h.shape[0]
    return pl.pallas_call(
        _kv_kernel,
        out_shape=(jax.ShapeDtypeStruct((t, KV_W), F32), jax.ShapeDtypeStruct((t, KV_W), F32)),
        grid=(t // tm,),
        in_specs=[
            pl.BlockSpec((tm, D_MODEL), lambda i: (i, 0)),
            pl.BlockSpec((None, D_MODEL, W_KV), lambda i: (layer, 0, 0)),
            pl.BlockSpec((1, HEAD_DIM), lambda i: (0, 0)),
        ],
        out_specs=(pl.BlockSpec((tm, KV_W), lambda i: (i, 0)), pl.BlockSpec((tm, KV_W), lambda i: (i, 0))),
        compiler_params=_params("arbitrary"),
        name="kv_proj",
    )(h, w_kv, k_norm.reshape(1, HEAD_DIM))


def _softmax_pv(s, v):
    m = jnp.max(s, axis=-1, keepdims=True)
    p = jnp.exp2(s - m)
    l = jnp.sum(p, axis=-1, keepdims=True)
    return _dot(p.astype(BF16), v) / l


def _qk(q, k):
    return lax.dot_general(q, k, (((1,), (1,)), ((), ())), preferred_element_type=F32)


def _norm_head(q, g):
    return q * lax.rsqrt(jnp.mean(q * q, axis=-1, keepdims=True) + EPS) * g


def _rope(x, cos2, sin2):
    return x * cos2 + pltpu.roll(x, HEAD_DIM // 2, 1) * sin2


def _attn_ctx_kernel(q_ref, k_ref, v_ref, ag_ref, qn_ref, o_ref):
    seq = q_ref.shape[0]
    for hd in range(N_KV_HEADS):
        kv = slice(hd * HEAD_DIM, (hd + 1) * HEAD_DIM)
        cols = [slice((hd * KV_GROUP + g) * HEAD_DIM, (hd * KV_GROUP + g + 1) * HEAD_DIM) for g in range(KV_GROUP)]
        q = jnp.concatenate([(_norm_head(q_ref[:, c].astype(F32), qn_ref[...]) * Q_SCALE).astype(BF16)
                             for c in cols], axis=0)
        o = _softmax_pv(_qk(q, k_ref[:, kv].astype(BF16)), v_ref[:, kv].astype(BF16))
        for g, c in enumerate(cols):
            o_ref[:, c] = (o[g * seq:(g + 1) * seq] * _silu(ag_ref[:, c].astype(F32))).astype(BF16)


def _attention_ctx(proj, k, v, q_norm, n_seq, seq):
    return pl.pallas_call(
        _attn_ctx_kernel,
        out_shape=jax.ShapeDtypeStruct((n_seq * seq, ATTN_W), BF16),
        grid=(n_seq,),
        in_specs=[
            pl.BlockSpec((seq, ATTN_W), lambda b: (b, OFF_Q // ATTN_W)),
            pl.BlockSpec((seq, KV_W), lambda b: (b, 0)),
            pl.BlockSpec((seq, KV_W), lambda b: (b, 0)),
            pl.BlockSpec((seq, ATTN_W), lambda b: (b, OFF_AGATE // ATTN_W)),
            pl.BlockSpec((1, HEAD_DIM), lambda b: (0, 0)),
        ],
        out_specs=pl.BlockSpec((seq, ATTN_W), lambda b: (b, 0)),
        compiler_params=_params("arbitrary"),
        name="attn_ctx",
    )(proj, k, v, proj, q_norm.reshape(1, HEAD_DIM))


def _attn_lat_kernel(q_ref, k_ref, v_ref, ck_ref, cv_ref, ag_ref, qn_ref, cosq_ref, sinq_ref,
                     cosk_ref, sink_ref, o_ref, k_s, v_s, *, seq):
    @pl.when(pl.program_id(2) == 0)
    def _():
        k_s[:seq] = _rope(k_ref[...], cosk_ref[...], sink_ref[...]).astype(BF16)
        k_s[seq:] = ck_ref[...].astype(BF16)
        v_s[:seq] = v_ref[...].astype(BF16)
        v_s[seq:] = cv_ref[...].astype(BF16)

    k = k_s[...]
    v = v_s[...]
    for g in range(KV_GROUP):
        sl = slice(g * HEAD_DIM, (g + 1) * HEAD_DIM)
        for r0 in range(0, q_ref.shape[0], ATTN_ROWS):
            rows = slice(r0, r0 + ATTN_ROWS)
            q = _norm_head(q_ref[rows, sl].astype(F32), qn_ref[...])
            q = _rope(q, cosq_ref[rows, :], sinq_ref[rows, :]) * Q_SCALE
            o = _softmax_pv(_qk(q.astype(BF16), k), v)
            o_ref[rows, sl] = (o * _silu(ag_ref[rows, sl].astype(F32))).astype(BF16)


def _attention_lat(proj, k, v, cache_k, cache_v, layer, q_norm, cos2, sin2, n_ctx, n_seq, seq, tq=512):
    past = cache_k.shape[2]
    ck = cache_k.reshape(cache_k.shape[0], DEPTH, past, KV_W)
    cv = cache_v.reshape(cache_v.shape[0], DEPTH, past, KV_W)
    nq = seq // tq
    row0, seq0 = n_ctx // tq, n_ctx // seq
    return pl.pallas_call(
        functools.partial(_attn_lat_kernel, seq=seq),
        out_shape=jax.ShapeDtypeStruct((n_seq * seq, ATTN_W), BF16),
        grid=(n_seq, N_KV_HEADS, nq),
        in_specs=[
            pl.BlockSpec((tq, GROUP_W), lambda b, h, i: (row0 + b * nq + i, OFF_Q // GROUP_W + h)),
            pl.BlockSpec((seq, HEAD_DIM), lambda b, h, i: (seq0 + b, h)),
            pl.BlockSpec((seq, HEAD_DIM), lambda b, h, i: (seq0 + b, h)),
            pl.BlockSpec((None, None, past, HEAD_DIM), lambda b, h, i: (b, layer, 0, h)),
            pl.BlockSpec((None, None, past, HEAD_DIM), lambda b, h, i: (b, layer, 0, h)),
            pl.BlockSpec((tq, GROUP_W), lambda b, h, i: (row0 + b * nq + i, OFF_AGATE // GROUP_W + h)),
            pl.BlockSpec((1, HEAD_DIM), lambda b, h, i: (0, 0)),
            pl.BlockSpec((tq, HEAD_DIM), lambda b, h, i: (i, 0)),
            pl.BlockSpec((tq, HEAD_DIM), lambda b, h, i: (i, 0)),
            pl.BlockSpec((seq, HEAD_DIM), lambda b, h, i: (0, 0)),
            pl.BlockSpec((seq, HEAD_DIM), lambda b, h, i: (0, 0)),
        ],
        out_specs=pl.BlockSpec((tq, GROUP_W), lambda b, h, i: (b * nq + i, h)),
        scratch_shapes=[pltpu.VMEM((seq + past, HEAD_DIM), BF16), pltpu.VMEM((seq + past, HEAD_DIM), BF16)],
        compiler_params=_params("arbitrary", "arbitrary", "arbitrary"),
        name="attn_lat",
    )(proj, k, v, ck, cv, proj, q_norm.reshape(1, HEAD_DIM), cos2, sin2, cos2, sin2)


def _rope_tables(seq):
    rows = seq // GRID_W
    row = jnp.repeat(jnp.arange(rows, dtype=F32), GRID_W)
    col = jnp.tile(jnp.arange(GRID_W, dtype=F32), rows)
    inv = ROPE_THETA ** (-jnp.arange(ROPE_PAIRS_AXIS, dtype=F32) / ROPE_PAIRS_AXIS)
    ang = jnp.concatenate([row[:, None] * inv, col[:, None] * inv], axis=-1)
    cos, sin = jnp.cos(ang), jnp.sin(ang)
    return jnp.concatenate([cos, cos], axis=-1), jnp.concatenate([-sin, sin], axis=-1)


def _dft_angle(i, j, n):
    return (2.0 * math.pi / n) * ((i[:, None] * j[None, :]) % n).astype(F32)


def _cos_sin_table(rows, cols, n):
    c = jnp.arange(cols, dtype=jnp.int32)
    hi = _dft_angle(TABLE_SPLIT * jnp.arange(rows // TABLE_SPLIT, dtype=jnp.int32), c, n)
    lo = _dft_angle(jnp.arange(TABLE_SPLIT, dtype=jnp.int32), c, n)
    ch, sh = jnp.cos(hi)[:, None, :], jnp.sin(hi)[:, None, :]
    cl, sl = jnp.cos(lo)[None, :, :], jnp.sin(lo)[None, :, :]
    return (ch * cl - sh * sl).reshape(rows, cols), (sh * cl + ch * sl).reshape(rows, cols)


def _fnet_kernel(u_ref, fg_ref, csc_ref, cos_ref, sin_ref, o_ref, tc_s, ts_s, *, scale):
    @pl.when(pl.program_id(1) == 0)
    def _():
        for g in range(FNET_GROUPS):
            sl = slice(g * FNET_GROUP_W, (g + 1) * FNET_GROUP_W)
            t = _dot(u_ref[:, sl], csc_ref[...])
            tc_s[:, sl] = t[:, :FNET_GROUP_W].astype(BF16)
            ts_s[:, sl] = t[:, FNET_GROUP_W:].astype(BF16)

    y = (_dot(cos_ref[...], tc_s[...]) - _dot(sin_ref[...], ts_s[...])) * scale
    o_ref[...] = (y * _silu(fg_ref[...].astype(F32))).astype(BF16)


def _fnet_tables(seq):
    j = jnp.arange(FNET_GROUP_W, dtype=jnp.int32)
    ang = _dft_angle(j, j, FNET_GROUP_W)
    csc = jnp.concatenate([jnp.cos(ang), jnp.sin(ang)], axis=1).astype(BF16)
    cos, sin = _cos_sin_table(seq, seq, seq)
    return csc, cos.astype(BF16), sin.astype(BF16)


def _fnet(proj, tables, row_start, n_seq, seq, tr=256):
    csc, cos, sin = tables
    nr = seq // tr
    s0, r0 = row_start // seq, row_start // tr
    return pl.pallas_call(
        functools.partial(_fnet_kernel, scale=1.0 / math.sqrt(seq * FNET_GROUP_W)),
        out_shape=jax.ShapeDtypeStruct((n_seq * seq, FNET_W), BF16),
        grid=(n_seq, nr),
        in_specs=[
            pl.BlockSpec((seq, FNET_W), lambda b, r: (s0 + b, OFF_FIN // FNET_W)),
            pl.BlockSpec((tr, FNET_W), lambda b, r: (r0 + b * nr + r, OFF_FGATE // FNET_W)),
            pl.BlockSpec((FNET_GROUP_W, 2 * FNET_GROUP_W), lambda b, r: (0, 0)),
            pl.BlockSpec((tr, seq), lambda b, r: (r, 0)),
            pl.BlockSpec((tr, seq), lambda b, r: (r, 0)),
        ],
        out_specs=pl.BlockSpec((tr, FNET_W), lambda b, r: (b * nr + r, 0)),
        scratch_shapes=[pltpu.VMEM((seq, FNET_W), BF16), pltpu.VMEM((seq, FNET_W), BF16)],
        compiler_params=_params("arbitrary", "arbitrary"),
        name=f"fnet_{seq}",
    )(proj, proj, csc, cos, sin)


def _short_kernel(x_ref, w_ref, b_ref, o_ref, *, seq):
    x = x_ref[...].astype(F32)
    n = x.shape[0]
    pos = lax.broadcasted_iota(jnp.int32, x.shape, 0) % seq
    prev = jnp.where(pos == 0, 0.0, pltpu.roll(x, 1, 0))
    nxt = jnp.where(pos == seq - 1, 0.0, pltpu.roll(x, n - 1, 0))
    y = b_ref[...] + prev * w_ref[0:1, :] + x * w_ref[1:2, :] + nxt * w_ref[2:3, :]
    o_ref[...] = y.astype(BF16)


def _short_conv(proj, w, b, row_start, n_rows, seq, tr=2048, tw=512):
    width = 3 * HYENA_W
    r0 = row_start // tr
    return pl.pallas_call(
        functools.partial(_short_kernel, seq=seq),
        out_shape=jax.ShapeDtypeStruct((n_rows, width), BF16),
        grid=(n_rows // tr, width // tw),
        in_specs=[
            pl.BlockSpec((tr, tw), lambda s, c: (r0 + s, OFF_HV // tw + c)),
            pl.BlockSpec((HYENA_SHORT, tw), lambda s, c: (0, c)),
            pl.BlockSpec((1, tw), lambda s, c: (0, c)),
        ],
        out_specs=pl.BlockSpec((tr, tw), lambda s, c: (s, c)),
        compiler_params=_params("arbitrary", "arbitrary"),
        name=f"short_conv_{seq}",
    )(proj, w, b.reshape(1, width))


def _filt_mlp_kernel(feats_ref, w1_ref, b1_ref, w2_ref, b2_ref, fr_ref, o_ref):
    fr = fr_ref[...]
    h = jnp.sin(fr * (_dot_exact(feats_ref[...], w1_ref[...]) + b1_ref[...]))
    o_ref[...] = jnp.sin(fr * (_dot_exact(h, w2_ref[...]) + b2_ref[...]))


def _filt_kernel(h_ref, t_ref, w3f_ref, w3b_ref, b3f_ref, b3b_ref, dl_ref, sum_ref, dif_ref, nyq_ref):
    h = h_ref[...]
    decay = jnp.exp(-t_ref[...] * dl_ref[...])
    hf = (_dot_exact(h, w3f_ref[...]) + b3f_ref[...]) * decay
    hb = (_dot_exact(h, w3b_ref[...]) + b3b_ref[...]) * decay
    den = jnp.sum(jnp.abs(hf), axis=0, keepdims=True) + jnp.sum(jnp.abs(hb), axis=0, keepdims=True) + EPS
    hs = (hf + hb) / den
    sum_ref[...] = hs.astype(BF16)
    dif_ref[...] = ((hf - hb) / den).astype(BF16)
    sign = (1 - 2 * (lax.broadcasted_iota(jnp.int32, hs.shape, 0) % 2)).astype(F32)
    nyq_ref[...] = jnp.sum(hs * sign, axis=0, keepdims=True)


def _hyena_filters(seq, w1, b1, w2, b2, w3, b3, freq, tw=512):
    t = jnp.arange(seq, dtype=F32)[:, None] / seq
    bands = jnp.arange(1, HYENA_BANDS + 1, dtype=F32)[None, :]
    feats = jnp.concatenate([t, jnp.cos(2 * math.pi * t * bands), jnp.sin(2 * math.pi * t * bands),
                             jnp.zeros((seq, HYENA_POS_PAD - HYENA_POS_DIM), F32)], axis=-1)
    w1p = jnp.concatenate([w1, jnp.zeros((HYENA_POS_PAD - HYENA_POS_DIM, HYENA_FFN_W), F32)], axis=0)
    deltas = jnp.abs(jnp.linspace(HYENA_MIN_DECAY, HYENA_MAX_DECAY, HYENA_W, dtype=F32))[None, :]
    hidden = pl.pallas_call(
        _filt_mlp_kernel,
        out_shape=jax.ShapeDtypeStruct((seq, HYENA_FFN_W), F32),
        name=f"hyena_filter_mlp_{seq}",
    )(feats, w1p, b1.reshape(1, -1), w2, b2.reshape(1, -1), freq.reshape(1, -1))
    nw = HYENA_W // tw
    small = lambda shape: pl.BlockSpec(shape, lambda o, c: (0, 0))
    out_sds = jax.ShapeDtypeStruct((seq, HYENA_ORDER * HYENA_W), BF16)
    fwd = lambda o, c: (0, (2 * o) * nw + c)
    bwd = lambda o, c: (0, (2 * o + 1) * nw + c)
    b3r = b3.reshape(1, -1)
    return pl.pallas_call(
        _filt_kernel,
        out_shape=(out_sds, out_sds, jax.ShapeDtypeStruct((1, HYENA_ORDER * HYENA_W), F32)),
        grid=(HYENA_ORDER, nw),
        in_specs=[
            small((seq, HYENA_FFN_W)), small((seq, 1)),
            pl.BlockSpec((HYENA_FFN_W, tw), fwd), pl.BlockSpec((HYENA_FFN_W, tw), bwd),
            pl.BlockSpec((1, tw), fwd), pl.BlockSpec((1, tw), bwd),
            pl.BlockSpec((1, tw), lambda o, c: (0, c)),
        ],
        out_specs=(pl.BlockSpec((seq, tw), lambda o, c: (0, o * nw + c)),
                   pl.BlockSpec((seq, tw), lambda o, c: (0, o * nw + c)),
                   pl.BlockSpec((1, tw), lambda o, c: (0, o * nw + c))),
        compiler_params=_params("arbitrary", "arbitrary"),
        name=f"hyena_filters_{seq}",
    )(hidden, t, w3, w3, b3r, b3r, deltas)


def _conv_tables(seq):
    i = jnp.arange(seq, dtype=jnp.int32)
    nyq = (1 - 2 * (i % 2)).astype(F32)
    cos, sin = _cos_sin_table(seq, seq, 2 * seq)
    sin_f = jnp.where(i[:, None] == 0, nyq[None, :], sin)
    sin_t = jnp.where(i[None, :] == 0, nyq[:, None], sin)
    return cos.astype(BF16), sin_f.astype(BF16), sin_t.astype(BF16)


def _filter_bins(uc_ref, us_ref, nyq_ref, first_chunk):
    gr = uc_ref[...]
    packed = jnp.logical_and(lax.broadcasted_iota(jnp.int32, gr.shape, 0) == 0, first_chunk)
    gi = jnp.where(packed, 0.0, us_ref[...])
    gn = jnp.where(packed, nyq_ref[...], gr)
    wgt = jnp.where(packed, 1.0, 2.0)
    return gr * wgt, gi * wgt, gn * wgt


def _spectrum_product(cos_ref, sin_ref, z, bins):
    gr, gi, gn = bins
    zr = _dot(cos_ref[...], z)
    zi = _dot(sin_ref[...], z)
    return (zr * gr - zi * gi).astype(BF16), (zr * gi + zi * gn).astype(BF16)


def _conv_epilogue(y, z, bias_ref, xm, gate, n):
    y = (y * (1.0 / n) + z.astype(F32) * bias_ref[...]) * xm.astype(F32)
    if gate is not None:
        y = y * _silu(gate.astype(F32))
    return y.astype(BF16)


def _lconv_short_kernel(*refs, seq, n_seq_blk, gated):
    refs = list(refs)
    z_ref, xm_ref = refs[:2]
    gate_ref = refs[2] if gated else None
    bias_ref, cos_ref, sin_ref, sin_t_ref, uc_ref, us_ref, nyq_ref, o_ref = refs[2 + gated:]
    bins = _filter_bins(uc_ref, us_ref, nyq_ref, True)
    for s in range(n_seq_blk):
        rows = slice(s * seq, (s + 1) * seq)
        z = z_ref[rows, :]
        pr, pi = _spectrum_product(cos_ref, sin_ref, z, bins)
        y = _dot(cos_ref[...], pr) + _dot(sin_t_ref[...], pi)
        o_ref[rows, :] = _conv_epilogue(y, z, bias_ref, xm_ref[rows, :],
                                        gate_ref[rows, :] if gated else None, 2 * seq)


def _lconv_long_kernel(*refs, seq, chunk, gated):
    refs = list(refs)
    z_ref, xm_ref = refs[:2]
    gate_ref = refs[2] if gated else None
    bias_ref, cos_ref, sin_ref, sin_t_ref, uc_ref, us_ref, nyq_ref, o_ref, pr_s, pi_s = refs[2 + gated:]
    n_chunks = seq // chunk
    p = pl.program_id(2)

    @pl.when(p < n_chunks)
    def _():
        bins = _filter_bins(uc_ref, us_ref, nyq_ref, p == 0)
        pr, pi = _spectrum_product(cos_ref, sin_ref, z_ref[...], bins)
        rows = pl.ds(pl.multiple_of(p * chunk, chunk), chunk)
        pr_s[rows, :] = pr
        pi_s[rows, :] = pi

    @pl.when(p >= n_chunks)
    def _():
        y = _dot(cos_ref[...], pr_s[...]) + _dot(sin_t_ref[...], pi_s[...])
        rows = pl.ds(pl.multiple_of((p - n_chunks) * chunk, chunk), chunk)
        o_ref[...] = _conv_epilogue(y, z_ref[rows, :], bias_ref, xm_ref[...],
                                    gate_ref[...] if gated else None, 2 * seq)


def _long_conv(z, z_col, xm, xm_col, gate, gate_row0, bias, idx, order, tables, spectrum,
               n_seq, seq, tw, fc, seq_blk):
    nw, nf = HYENA_W // tw, seq // fc
    gated = gate is not None
    out_shape = jax.ShapeDtypeStruct((n_seq * seq, HYENA_W), BF16)
    bias_spec = pl.BlockSpec((None, 1, tw), lambda s, w, p: (idx, 0, w))
    nyq_spec = pl.BlockSpec((1, tw), lambda s, w, p: (0, order * nw + w))
    if nf == 1:
        tr = seq_blk * seq
        tok = lambda col, r0=0: pl.BlockSpec((tr, tw), lambda s, w, p: (r0 + s, col * nw + w))
        taps = pl.BlockSpec((seq, tw), lambda s, w, p: (0, order * nw + w))
        table = pl.BlockSpec((seq, seq), lambda s, w, p: (0, 0))
        in_specs = [tok(z_col), tok(xm_col)] + ([tok(OFF_HGATE // HYENA_W, gate_row0 // tr)] if gated else [])
        in_specs += [bias_spec, table, table, table, taps, taps, nyq_spec]
        kern = functools.partial(_lconv_short_kernel, seq=seq, n_seq_blk=seq_blk, gated=gated)
        grid, scratch = (n_seq // seq_blk, nw, 1), []
        out_spec = pl.BlockSpec((tr, tw), lambda s, w, p: (s, w))
    else:
        assert seq_blk == 1
        fwd_chunk = lambda p: jnp.minimum(p, nf - 1)
        out_chunk = lambda p: jnp.maximum(p - nf, 0)
        rows = lambda col, r0=0: pl.BlockSpec((fc, tw), lambda s, w, p: (r0 + s * nf + out_chunk(p), col * nw + w))
        taps = pl.BlockSpec((fc, tw), lambda s, w, p: (fwd_chunk(p), order * nw + w))
        in_specs = [pl.BlockSpec((seq, tw), lambda s, w, p: (s, z_col * nw + w)), rows(xm_col)]
        in_specs += [rows(OFF_HGATE // HYENA_W, gate_row0 // fc)] if gated else []
        in_specs += [
            bias_spec,
            pl.BlockSpec((fc, seq), lambda s, w, p: (p % nf, 0)),
            pl.BlockSpec((fc, seq), lambda s, w, p: (fwd_chunk(p), 0)),
            pl.BlockSpec((fc, seq), lambda s, w, p: (out_chunk(p), 0)),
            taps, taps, nyq_spec,
        ]
        kern = functools.partial(_lconv_long_kernel, seq=seq, chunk=fc, gated=gated)
        grid, scratch = (n_seq, nw, 2 * nf), [pltpu.VMEM((seq, tw), BF16), pltpu.VMEM((seq, tw), BF16)]
        out_spec = pl.BlockSpec((fc, tw), lambda s, w, p: (s * nf + out_chunk(p), w))
    args = [z, xm] + ([gate] if gated else []) + [bias, *tables, *spectrum]
    return pl.pallas_call(
        kern,
        out_shape=out_shape,
        grid=grid,
        in_specs=in_specs,
        out_specs=out_spec,
        scratch_shapes=scratch,
        compiler_params=_params("arbitrary", "arbitrary", "arbitrary"),
        name=f"long_conv_{seq}_{order}",
    )(*args)


def _merge_kernel(ac_ref, al_ref, fc_ref, fl_ref, hc_ref, hl_ref, wa_ref, wf_ref, wh_ref,
                  ga_ref, gf_ref, gh_ref, o_ref, wab_ref, wfb_ref, whb_ref, *, n_ctx_tiles):
    i = pl.program_id(1)

    @pl.when(i == 0)
    def _():
        wab_ref[...] = wa_ref[...].astype(BF16)
        wfb_ref[...] = wf_ref[...].astype(BF16)
        whb_ref[...] = wh_ref[...].astype(BF16)

    def body(a_ref, f_ref, h_ref):
        m = jax.nn.sigmoid(ga_ref[...].astype(F32)) * _dot(a_ref[...], wab_ref[...])
        m = m + jax.nn.sigmoid(gf_ref[...].astype(F32)) * _dot(f_ref[...], wfb_ref[...])
        m = m + jax.nn.sigmoid(gh_ref[...].astype(F32)) * _dot(h_ref[...], whb_ref[...])
        o_ref[...] = m.astype(BF16)

    _by_group(i, n_ctx_tiles, body, (ac_ref, fc_ref, hc_ref), (al_ref, fl_ref, hl_ref))


def _merge(attn, fnet, hy, w_a, w_f, w_h, layer, proj, n_ctx, tm=512, tn=1024):
    t = proj.shape[0]
    nct = n_ctx // tm
    act = lambda w: _pair_specs((tm, w), nct, inner=True)
    wgt = lambda k: pl.BlockSpec((None, k, tn), lambda j, i: (layer, 0, j), pipeline_mode=pl.Buffered(1))
    gate = lambda off: pl.BlockSpec((tm, tn), lambda j, i: (i, off // tn + j))
    return pl.pallas_call(
        functools.partial(_merge_kernel, n_ctx_tiles=nct),
        out_shape=jax.ShapeDtypeStruct((t, D_MODEL), BF16),
        grid=(D_MODEL // tn, t // tm),
        in_specs=act(ATTN_W) + act(FNET_W) + act(HYENA_W) + [wgt(ATTN_W), wgt(FNET_W), wgt(HYENA_W),
                                                             gate(OFF_GA), gate(OFF_GF), gate(OFF_GH)],
        out_specs=pl.BlockSpec((tm, tn), lambda j, i: (i, j)),
        scratch_shapes=[pltpu.VMEM((ATTN_W, tn), BF16), pltpu.VMEM((FNET_W, tn), BF16),
                        pltpu.VMEM((HYENA_W, tn), BF16)],
        compiler_params=_params("arbitrary", "arbitrary"),
        name="merge",
    )(*attn, *fnet, *hy, w_a, w_f, w_h, proj, proj, proj)


def kernel(x_prompt, x_sample, cache_k, cache_v, c, c_ctx, w_mod, b_mod, g_pre, w_in, q_norm, k_norm, hy_short_w, hy_short_b, hy_ffn_w1, hy_ffn_b1, hy_ffn_w2, hy_ffn_b2, hy_ffn_w3, hy_ffn_b3, hy_sin_freq, hy_bias, w_attn_o, w_fnet_o, w_hy_o, w_out, g_post):
    nb, seq, _ = x_prompt.shape
    db, dseq, _ = x_sample.shape
    assert 1 + db <= MOD_ROWS
    n_ctx, n_lat = nb * seq, db * dseq

    cond = jnp.concatenate([c_ctx[None, :], c, jnp.zeros((MOD_ROWS - 1 - db, D_MODEL), F32)], axis=0)
    mod = _modulation(cond, w_mod, b_mod)

    w_kv_b = w_in[:, :, W_OFF_K:W_OFF_K + W_KV].astype(BF16)
    cos2, sin2 = _rope_tables(dseq)
    fnet_tables = {s: _fnet_tables(s) for s in (seq, dseq)}
    conv_tables = {s: _conv_tables(s) for s in (seq, dseq)}
    bias = hy_bias.reshape(DEPTH * HYENA_ORDER, 1, HYENA_W)
    conv_tiles = {seq: (HYENA_W, seq, 4), dseq: (512, 512, 1)}
    proj_tm, proj_tn = 512, 1024
    skip_kv = lambda j: jnp.where(j >= W_OFF_K // proj_tn, j + W_KV // proj_tn, j)

    xs = (x_prompt.reshape(n_ctx, D_MODEL), 0, x_sample.reshape(n_lat, D_MODEL), 0)
    new_k, new_v = [], []
    mod3s = [mod[l].reshape(MOD_ROWS, 1, 3 * D_MODEL) for l in range(DEPTH)]
    h = _prenorm(*xs, g_pre[0], mod3s[0], n_ctx, n_lat, dseq)
    for l in range(DEPTH):
        mod3 = mod3s[l]
        proj = _matmul_wcast(h, w_in, l, BF16, proj_tm, proj_tn, "in_proj", n_out=PROJ_W, col_map=skip_kv)
        k, v = _kv_proj(h, w_kv_b, l, k_norm[l])
        new_k.append(k[:n_ctx].reshape(nb, seq, N_KV_HEADS, HEAD_DIM))
        new_v.append(v[:n_ctx].reshape(nb, seq, N_KV_HEADS, HEAD_DIM))

        attn = (_attention_ctx(proj, k, v, q_norm[l], nb, seq),
                _attention_lat(proj, k, v, cache_k, cache_v, l, q_norm[l], cos2, sin2, n_ctx, db, dseq))

        fnet, hy = [], []
        for row_start, n_seq, s in ((0, nb, seq), (n_ctx, db, dseq)):
            fnet.append(_fnet(proj, fnet_tables[s], row_start, n_seq, s))
            u = _short_conv(proj, hy_short_w[l], hy_short_b[l], row_start, n_seq * s, s)
            tap_sum, tap_dif, tap_nyq = _hyena_filters(s, hy_ffn_w1[l], hy_ffn_b1[l], hy_ffn_w2[l], hy_ffn_b2[l],
                                                       hy_ffn_w3[l], hy_ffn_b3[l], hy_sin_freq[l])
            cos, sin_f, _ = conv_tables[s]
            spec_tm = min(1024, s)
            spectrum = (_matmul(cos, tap_sum, F32, spec_tm, 1024, f"filter_spectrum_cos_{s}"),
                        _matmul(sin_f, tap_dif, F32, spec_tm, 1024, f"filter_spectrum_sin_{s}"), tap_nyq)
            tw, fc, blk = conv_tiles[s]
            z1 = _long_conv(u, 0, u, 1, None, 0, bias, l * HYENA_ORDER, 0, conv_tables[s],
                            spectrum, n_seq, s, tw, fc, blk)
            hy.append(_long_conv(z1, 0, u, 2, proj, row_start, bias, l * HYENA_ORDER + 1, 1, conv_tables[s],
                                 spectrum, n_seq, s, tw, fc, blk))

        merged = _merge(attn, fnet, hy, w_attn_o, w_fnet_o, w_hy_o, l, proj, n_ctx)
        r = _matmul_wcast(merged, w_out, l, BF16, proj_tm, proj_tn, "out_proj")
        if l < DEPTH - 1:
            x_new, h = _post(*xs, r, 0, g_post[l], mod3, n_ctx, n_lat, dseq, next_norm=(g_pre[l + 1], mod3s[l + 1]))
            xs = (x_new, 0, x_new, n_ctx)
        else:
            y_prompt = _post(*xs, r, 0, g_post[l], mod3, n_ctx, 0, dseq)
            y_sample = _post(*xs, r, n_ctx, g_post[l], mod3, 0, n_lat, dseq)

    return (y_prompt.reshape(nb, seq, D_MODEL), y_sample.reshape(db, dseq, D_MODEL),
            jnp.stack(new_k, axis=1), jnp.stack(new_v, axis=1))
```

```python
import functools
import math

import jax
import jax.numpy as jnp
from jax import lax
from jax.experimental import pallas as pl
from jax.experimental.pallas import tpu as pltpu

F32 = jnp.float32
BF16 = jnp.bfloat16

D_MODEL = 4096
DEPTH = 2
GRID_W = 64
HEAD_DIM = 128
N_HEADS = 16
N_KV_HEADS = 4
KV_GROUP = N_HEADS // N_KV_HEADS
ATTN_W = N_HEADS * HEAD_DIM
KV_W = N_KV_HEADS * HEAD_DIM
GROUP_W = KV_GROUP * HEAD_DIM
ATTN_SCALE = HEAD_DIM ** -0.5
Q_SCALE = ATTN_SCALE * math.log2(math.e)
ROPE_THETA = 10000.0
ROPE_PAIRS_AXIS = HEAD_DIM // 4
FNET_W = D_MODEL // 4
FNET_GROUPS = 4
FNET_GROUP_W = FNET_W // FNET_GROUPS
HYENA_W = D_MODEL // 4
HYENA_ORDER = 2
HYENA_SHORT = 3
HYENA_BANDS = 16
HYENA_POS_DIM = 1 + 2 * HYENA_BANDS
HYENA_POS_PAD = 128
HYENA_FFN_W = 64
HYENA_MIN_DECAY = math.log(1e-2) / 0.3
HYENA_MAX_DECAY = math.log(1e-2) / 1.5
EPS = 1e-6

W_OFF_K = ATTN_W
W_KV = 2 * KV_W
OFF_Q = 0
OFF_AGATE = OFF_Q + ATTN_W
OFF_FIN = OFF_AGATE + ATTN_W
OFF_FGATE = OFF_FIN + FNET_W
OFF_HV = OFF_FGATE + FNET_W
OFF_HX1 = OFF_HV + HYENA_W
OFF_HX2 = OFF_HX1 + HYENA_W
OFF_HGATE = OFF_HX2 + HYENA_W
OFF_GA = OFF_HGATE + HYENA_W
OFF_GF = OFF_GA + D_MODEL
OFF_GH = OFF_GF + D_MODEL
PROJ_W = OFF_GH + D_MODEL

MOD_ROWS = 8
ROW_TILE = 256
ATTN_ROWS = 256
TABLE_SPLIT = 64
VMEM_LIMIT = 56 * 1024 * 1024


def _params(*sem):
    return pltpu.CompilerParams(dimension_semantics=sem, vmem_limit_bytes=VMEM_LIMIT)


def _silu(x):
    return x * jax.nn.sigmoid(x)


def _dot(a, b):
    return jnp.dot(a, b, preferred_element_type=F32)


def _dot_exact(a, b):
    return jnp.dot(a, b, preferred_element_type=F32, precision=lax.Precision.HIGHEST)


def _pair_specs(block, n_ctx_tiles, ctx_tile0=0, lat_tile0=0, col=0, inner=False):
    def ctx(i):
        return (ctx_tile0 + jnp.clip(i, 0, max(n_ctx_tiles - 1, 0)), col)

    def lat(i):
        return (lat_tile0 + jnp.maximum(i - n_ctx_tiles, 0), col)

    if inner:
        return [pl.BlockSpec(block, lambda j, i: ctx(i)), pl.BlockSpec(block, lambda j, i: lat(i))]
    return [pl.BlockSpec(block, lambda i: ctx(i)), pl.BlockSpec(block, lambda i: lat(i))]


def _by_group(i, n_ctx_tiles, body, ctx_refs, lat_refs):
    @pl.when(i < n_ctx_tiles)
    def _():
        body(*ctx_refs)

    @pl.when(i >= n_ctx_tiles)
    def _():
        body(*lat_refs)


def _mod_kernel(c_ref, w_ref, b_ref, o_ref):
    s = _silu(c_ref[...]).astype(BF16)
    o_ref[0] = _dot(s, w_ref[0].astype(BF16)) + b_ref[0]


def _modulation(cond, w_mod, b_mod, tn=512):
    n = w_mod.shape[2]
    return pl.pallas_call(
        _mod_kernel,
        out_shape=jax.ShapeDtypeStruct((DEPTH, MOD_ROWS, n), F32),
        grid=(DEPTH, n // tn),
        in_specs=[
            pl.BlockSpec((MOD_ROWS, D_MODEL), lambda l, j: (0, 0)),
            pl.BlockSpec((1, D_MODEL, tn), lambda l, j: (l, 0, j)),
            pl.BlockSpec((1, 1, tn), lambda l, j: (l, 0, j)),
        ],
        out_specs=pl.BlockSpec((1, MOD_ROWS, tn), lambda l, j: (l, 0, j)),
        compiler_params=_params("arbitrary", "arbitrary"),
        name="modulation",
    )(cond, w_mod, b_mod.reshape(DEPTH, 1, n))


def _mod_spec(chunk, n_ctx_tiles, tiles_per_latent_seq):
    def row(i):
        return jnp.where(i < n_ctx_tiles, 0, 1 + (i - n_ctx_tiles) // tiles_per_latent_seq)
    return pl.BlockSpec((1, 1, D_MODEL), lambda i: (row(i), 0, chunk))


def _modulated_norm(x, g_ref, shift_ref, scale_ref):
    y = x * lax.rsqrt(jnp.mean(x * x, axis=-1, keepdims=True) + EPS) * g_ref[...]
    return (y * (1.0 + scale_ref[0]) + shift_ref[0]).astype(BF16)


def _prenorm_kernel(xc_ref, xl_ref, g_ref, shift_ref, scale_ref, o_ref, *, n_ctx_tiles):
    def body(x_ref):
        o_ref[...] = _modulated_norm(x_ref[...], g_ref, shift_ref, scale_ref)

    _by_group(pl.program_id(0), n_ctx_tiles, body, (xc_ref,), (xl_ref,))


def _prenorm(x_ctx, ctx_row0, x_lat, lat_row0, g, mod3, n_ctx, n_lat, lat_seq):
    tm = ROW_TILE
    nct = n_ctx // tm
    mod = functools.partial(_mod_spec, n_ctx_tiles=nct, tiles_per_latent_seq=lat_seq // tm)
    return pl.pallas_call(
        functools.partial(_prenorm_kernel, n_ctx_tiles=nct),
        out_shape=jax.ShapeDtypeStruct((n_ctx + n_lat, D_MODEL), BF16),
        grid=((n_ctx + n_lat) // tm,),
        in_specs=_pair_specs((tm, D_MODEL), nct, ctx_row0 // tm, lat_row0 // tm)
        + [pl.BlockSpec((1, D_MODEL), lambda i: (0, 0)), mod(0), mod(1)],
        out_specs=pl.BlockSpec((tm, D_MODEL), lambda i: (i, 0)),
        compiler_params=_params("arbitrary"),
        name="prenorm",
    )(x_ctx, x_lat, g.reshape(1, D_MODEL), mod3, mod3)


def _post_kernel(xc_ref, xl_ref, r_ref, g_ref, gate_ref, *rest, n_ctx_tiles, with_next):
    if with_next:
        g_next_ref, shift_ref, scale_ref, o_ref, h_ref = rest
    else:
        (o_ref,) = rest

    def body(x_ref):
        r = r_ref[...].astype(F32)
        y = r * lax.rsqrt(jnp.mean(r * r, axis=-1, keepdims=True) + EPS) * g_ref[...]
        x = x_ref[...] + gate_ref[0] * y
        o_ref[...] = x
        if with_next:
            h_ref[...] = _modulated_norm(x, g_next_ref, shift_ref, scale_ref)

    _by_group(pl.program_id(0), n_ctx_tiles, body, (xc_ref,), (xl_ref,))


def _post(x_ctx, ctx_row0, x_lat, lat_row0, r, r_row0, g, mod3, n_ctx, n_lat, lat_seq, next_norm=None):
    tm = ROW_TILE
    nct = n_ctx // tm
    r0 = r_row0 // tm
    rows = n_ctx + n_lat
    mod = functools.partial(_mod_spec, n_ctx_tiles=nct, tiles_per_latent_seq=lat_seq // tm)
    vec = pl.BlockSpec((1, D_MODEL), lambda i: (0, 0))
    tile = pl.BlockSpec((tm, D_MODEL), lambda i: (i, 0))
    in_specs = _pair_specs((tm, D_MODEL), nct, ctx_row0 // tm, lat_row0 // tm) + [
        pl.BlockSpec((tm, D_MODEL), lambda i: (r0 + i, 0)), vec, mod(2)]
    args = [x_ctx, x_lat, r, g.reshape(1, D_MODEL), mod3]
    out_shape, out_specs = jax.ShapeDtypeStruct((rows, D_MODEL), F32), tile
    if next_norm is not None:
        g_next, mod3_next = next_norm
        in_specs += [vec, mod(0), mod(1)]
        args += [g_next.reshape(1, D_MODEL), mod3_next, mod3_next]
        out_shape, out_specs = (out_shape, jax.ShapeDtypeStruct((rows, D_MODEL), BF16)), (tile, tile)
    return pl.pallas_call(
        functools.partial(_post_kernel, n_ctx_tiles=nct, with_next=next_norm is not None),
        out_shape=out_shape,
        grid=(rows // tm,),
        in_specs=in_specs,
        out_specs=out_specs,
        compiler_params=_params("arbitrary"),
        name="post",
    )(*args)


def _mm_kernel(a_ref, b_ref, o_ref):
    o_ref[...] = _dot(a_ref[...], b_ref[...]).astype(o_ref.dtype)


def _matmul(a, b, out_dtype, tm, tn, name):
    m, k = a.shape
    n = b.shape[1]
    return pl.pallas_call(
        _mm_kernel,
        out_shape=jax.ShapeDtypeStruct((m, n), out_dtype),
        grid=(m // tm, n // tn),
        in_specs=[pl.BlockSpec((tm, k), lambda i, j: (i, 0)), pl.BlockSpec((k, tn), lambda i, j: (0, j))],
        out_specs=pl.BlockSpec((tm, tn), lambda i, j: (i, j)),
        compiler_params=_params("arbitrary", "arbitrary"),
        name=name,
    )(a, b)


def _mm_wcast_kernel(a_ref, w_ref, o_ref, wb_ref):
    @pl.when(pl.program_id(1) == 0)
    def _():
        wb_ref[...] = w_ref[...].astype(BF16)

    o_ref[...] = _dot(a_ref[...], wb_ref[...]).astype(o_ref.dtype)


def _matmul_wcast(a, w, layer, out_dtype, tm, tn, name, n_out=None, col_map=None):
    m, k = a.shape
    n = n_out or w.shape[-1]
    col = col_map or (lambda j: j)
    return pl.pallas_call(
        _mm_wcast_kernel,
        out_shape=jax.ShapeDtypeStruct((m, n), out_dtype),
        grid=(n // tn, m // tm),
        in_specs=[
            pl.BlockSpec((tm, k), lambda j, i: (i, 0)),
            pl.BlockSpec((None, k, tn), lambda j, i: (layer, 0, col(j))),
        ],
        out_specs=pl.BlockSpec((tm, tn), lambda j, i: (i, j)),
        scratch_shapes=[pltpu.VMEM((k, tn), BF16)],
        compiler_params=_params("arbitrary", "arbitrary"),
        name=name,
    )(a, w)


def _kv_kernel(h_ref, w_ref, kn_ref, k_ref, v_ref):
    acc = _dot(h_ref[...], w_ref[...])
    for hd in range(N_KV_HEADS):
        sl = slice(hd * HEAD_DIM, (hd + 1) * HEAD_DIM)
        k_ref[:, sl] = _norm_head(acc[:, sl], kn_ref[...])
    v_ref[...] = acc[:, KV_W:]


def _kv_proj(h, w_kv, layer, k_norm, tm=512):
    t = h.shape[0]
    return pl.pallas_call(
        _kv_kernel,
        out_shape=(jax.ShapeDtypeStruct((t, KV_W), F32), jax.ShapeDtypeStruct((t, KV_W), F32)),
        grid=(t // tm,),
        in_specs=[
            pl.BlockSpec((tm, D_MODEL), lambda i: (i, 0)),
            pl.BlockSpec((None, D_MODEL, W_KV), lambda i: (layer, 0, 0)),
            pl.BlockSpec((1, HEAD_DIM), lambda i: (0, 0)),
        ],
        out_specs=(pl.BlockSpec((tm, KV_W), lambda i: (i, 0)), pl.BlockSpec((tm, KV_W), lambda i: (i, 0))),
        compiler_params=_params("arbitrary"),
        name="kv_proj",
    )(h, w_kv, k_norm.reshape(1, HEAD_DIM))


def _cache_kernel(*refs):
    k_refs, v_refs, (ko_ref, vo_ref) = refs[:DEPTH], refs[DEPTH:2 * DEPTH], refs[2 * DEPTH:]
    for l in range(DEPTH):
        @pl.when(pl.program_id(0) == l)
        def _():
            for hd in range(N_KV_HEADS):
                sl = slice(hd * HEAD_DIM, (hd + 1) * HEAD_DIM)
                ko_ref[:, hd, :] = k_refs[l][:, sl]
                vo_ref[:, hd, :] = v_refs[l][:, sl]


def _new_cache(ks, vs, n_seq, seq):
    def layer_spec(l):
        return pl.BlockSpec((seq, KV_W), lambda cur, b: (jnp.where(cur == l, b, 0), 0))

    out_sds = jax.ShapeDtypeStruct((n_seq, DEPTH, seq, N_KV_HEADS, HEAD_DIM), F32)
    out_spec = pl.BlockSpec((None, None, seq, N_KV_HEADS, HEAD_DIM), lambda cur, b: (b, cur, 0, 0, 0))
    return pl.pallas_call(
        _cache_kernel,
        out_shape=(out_sds, out_sds),
        grid=(DEPTH, n_seq),
        in_specs=[layer_spec(l) for l in range(DEPTH)] * 2,
        out_specs=(out_spec, out_spec),
        compiler_params=_params("arbitrary", "arbitrary"),
        name="new_cache",
    )(*ks, *vs)


def _softmax_pv(s, v):
    m = jnp.max(s, axis=-1, keepdims=True)
    p = jnp.exp2(s - m)
    l = jnp.sum(p, axis=-1, keepdims=True)
    return _dot(p.astype(BF16), v) / l


def _qk(q, k):
    return lax.dot_general(q, k, (((1,), (1,)), ((), ())), preferred_element_type=F32)


def _norm_head(q, g):
    return q * lax.rsqrt(jnp.mean(q * q, axis=-1, keepdims=True) + EPS) * g


def _rope(x, cos2, sin2):
    return x * cos2 + pltpu.roll(x, HEAD_DIM // 2, 1) * sin2


def _attn_ctx_kernel(q_ref, k_ref, v_ref, ag_ref, qn_ref, o_ref):
    seq = q_ref.shape[0]
    for hd in range(N_KV_HEADS):
        kv = slice(hd * HEAD_DIM, (hd + 1) * HEAD_DIM)
        cols = [slice((hd * KV_GROUP + g) * HEAD_DIM, (hd * KV_GROUP + g + 1) * HEAD_DIM) for g in range(KV_GROUP)]
        q = jnp.concatenate([(_norm_head(q_ref[:, c].astype(F32), qn_ref[...]) * Q_SCALE).astype(BF16)
                             for c in cols], axis=0)
        o = _softmax_pv(_qk(q, k_ref[:, kv].astype(BF16)), v_ref[:, kv].astype(BF16))
        for g, c in enumerate(cols):
            o_ref[:, c] = (o[g * seq:(g + 1) * seq] * _silu(ag_ref[:, c].astype(F32))).astype(BF16)


def _attention_ctx(proj, k, v, q_norm, n_seq, seq):
    return pl.pallas_call(
        _attn_ctx_kernel,
        out_shape=jax.ShapeDtypeStruct((n_seq * seq, ATTN_W), BF16),
        grid=(n_seq,),
        in_specs=[
            pl.BlockSpec((seq, ATTN_W), lambda b: (b, OFF_Q // ATTN_W)),
            pl.BlockSpec((seq, KV_W), lambda b: (b, 0)),
            pl.BlockSpec((seq, KV_W), lambda b: (b, 0)),
            pl.BlockSpec((seq, ATTN_W), lambda b: (b, OFF_AGATE // ATTN_W)),
            pl.BlockSpec((1, HEAD_DIM), lambda b: (0, 0)),
        ],
        out_specs=pl.BlockSpec((seq, ATTN_W), lambda b: (b, 0)),
        compiler_params=_params("arbitrary"),
        name="attn_ctx",
    )(proj, k, v, proj, q_norm.reshape(1, HEAD_DIM))


def _attn_lat_kernel(q_ref, k_ref, v_ref, ck_ref, cv_ref, ag_ref, qn_ref, cosq_ref, sinq_ref,
                     cosk_ref, sink_ref, o_ref, k_s, v_s, *, seq):
    @pl.when(pl.program_id(2) == 0)
    def _():
        k_s[:seq] = _rope(k_ref[...], cosk_ref[...], sink_ref[...]).astype(BF16)
        k_s[seq:] = ck_ref[...].astype(BF16)
        v_s[:seq] = v_ref[...].astype(BF16)
        v_s[seq:] = cv_ref[...].astype(BF16)

    k = k_s[...]
    v = v_s[...]
    for g in range(KV_GROUP):
        sl = slice(g * HEAD_DIM, (g + 1) * HEAD_DIM)
        for r0 in range(0, q_ref.shape[0], ATTN_ROWS):
            rows = slice(r0, r0 + ATTN_ROWS)
            q = _norm_head(q_ref[rows, sl].astype(F32), qn_ref[...])
            q = _rope(q, cosq_ref[rows, :], sinq_ref[rows, :]) * Q_SCALE
            o = _softmax_pv(_qk(q.astype(BF16), k), v)
            o_ref[rows, sl] = (o * _silu(ag_ref[rows, sl].astype(F32))).astype(BF16)


def _attention_lat(proj, k, v, cache_k, cache_v, layer, q_norm, cos2, sin2, n_ctx, n_seq, seq, tq=512):
    past = cache_k.shape[2]
    ck = cache_k.reshape(cache_k.shape[0], DEPTH, past, KV_W)
    cv = cache_v.reshape(cache_v.shape[0], DEPTH, past, KV_W)
    nq = seq // tq
    row0, seq0 = n_ctx // tq, n_ctx // seq
    return pl.pallas_call(
        functools.partial(_attn_lat_kernel, seq=seq),
        out_shape=jax.ShapeDtypeStruct((n_seq * seq, ATTN_W), BF16),
        grid=(n_seq, N_KV_HEADS, nq),
        in_specs=[
            pl.BlockSpec((tq, GROUP_W), lambda b, h, i: (row0 + b * nq + i, OFF_Q // GROUP_W + h)),
            pl.BlockSpec((seq, HEAD_DIM), lambda b, h, i: (seq0 + b, h)),
            pl.BlockSpec((seq, HEAD_DIM), lambda b, h, i: (seq0 + b, h)),
            pl.BlockSpec((None, None, past, HEAD_DIM), lambda b, h, i: (b, layer, 0, h)),
            pl.BlockSpec((None, None, past, HEAD_DIM), lambda b, h, i: (b, layer, 0, h)),
            pl.BlockSpec((tq, GROUP_W), lambda b, h, i: (row0 + b * nq + i, OFF_AGATE // GROUP_W + h)),
            pl.BlockSpec((1, HEAD_DIM), lambda b, h, i: (0, 0)),
            pl.BlockSpec((tq, HEAD_DIM), lambda b, h, i: (i, 0)),
            pl.BlockSpec((tq, HEAD_DIM), lambda b, h, i: (i, 0)),
            pl.BlockSpec((seq, HEAD_DIM), lambda b, h, i: (0, 0)),
            pl.BlockSpec((seq, HEAD_DIM), lambda b, h, i: (0, 0)),
        ],
        out_specs=pl.BlockSpec((tq, GROUP_W), lambda b, h, i: (b * nq + i, h)),
        scratch_shapes=[pltpu.VMEM((seq + past, HEAD_DIM), BF16), pltpu.VMEM((seq + past, HEAD_DIM), BF16)],
        compiler_params=_params("arbitrary", "arbitrary", "arbitrary"),
        name="attn_lat",
    )(proj, k, v, ck, cv, proj, q_norm.reshape(1, HEAD_DIM), cos2, sin2, cos2, sin2)


def _rope_tables(seq):
    rows = seq // GRID_W
    row = jnp.repeat(jnp.arange(rows, dtype=F32), GRID_W)
    col = jnp.tile(jnp.arange(GRID_W, dtype=F32), rows)
    inv = ROPE_THETA ** (-jnp.arange(ROPE_PAIRS_AXIS, dtype=F32) / ROPE_PAIRS_AXIS)
    ang = jnp.concatenate([row[:, None] * inv, col[:, None] * inv], axis=-1)
    cos, sin = jnp.cos(ang), jnp.sin(ang)
    return jnp.concatenate([cos, cos], axis=-1), jnp.concatenate([-sin, sin], axis=-1)


def _dft_angle(i, j, n):
    return (2.0 * math.pi / n) * ((i[:, None] * j[None, :]) % n).astype(F32)


def _cos_sin_table(rows, cols, n):
    c = jnp.arange(cols, dtype=jnp.int32)
    hi = _dft_angle(TABLE_SPLIT * jnp.arange(rows // TABLE_SPLIT, dtype=jnp.int32), c, n)
    lo = _dft_angle(jnp.arange(TABLE_SPLIT, dtype=jnp.int32), c, n)
    ch, sh = jnp.cos(hi)[:, None, :], jnp.sin(hi)[:, None, :]
    cl, sl = jnp.cos(lo)[None, :, :], jnp.sin(lo)[None, :, :]
    return (ch * cl - sh * sl).reshape(rows, cols), (sh * cl + ch * sl).reshape(rows, cols)


def _fnet_kernel(u_ref, fg_ref, csc_ref, cos_ref, sin_ref, o_ref, tc_s, ts_s, *, scale):
    @pl.when(pl.program_id(1) == 0)
    def _():
        for g in range(FNET_GROUPS):
            sl = slice(g * FNET_GROUP_W, (g + 1) * FNET_GROUP_W)
            t = _dot(u_ref[:, sl], csc_ref[...])
            tc_s[:, sl] = t[:, :FNET_GROUP_W].astype(BF16)
            ts_s[:, sl] = t[:, FNET_GROUP_W:].astype(BF16)

    y = (_dot(cos_ref[...], tc_s[...]) - _dot(sin_ref[...], ts_s[...])) * scale
    o_ref[...] = (y * _silu(fg_ref[...].astype(F32))).astype(BF16)


def _fnet_tables(seq):
    j = jnp.arange(FNET_GROUP_W, dtype=jnp.int32)
    ang = _dft_angle(j, j, FNET_GROUP_W)
    csc = jnp.concatenate([jnp.cos(ang), jnp.sin(ang)], axis=1).astype(BF16)
    cos, sin = _cos_sin_table(seq, seq, seq)
    return csc, cos.astype(BF16), sin.astype(BF16)


def _fnet(proj, tables, row_start, n_seq, seq, tr=256):
    csc, cos, sin = tables
    nr = seq // tr
    s0, r0 = row_start // seq, row_start // tr
    return pl.pallas_call(
        functools.partial(_fnet_kernel, scale=1.0 / math.sqrt(seq * FNET_GROUP_W)),
        out_shape=jax.ShapeDtypeStruct((n_seq * seq, FNET_W), BF16),
        grid=(n_seq, nr),
        in_specs=[
            pl.BlockSpec((seq, FNET_W), lambda b, r: (s0 + b, OFF_FIN // FNET_W)),
            pl.BlockSpec((tr, FNET_W), lambda b, r: (r0 + b * nr + r, OFF_FGATE // FNET_W)),
            pl.BlockSpec((FNET_GROUP_W, 2 * FNET_GROUP_W), lambda b, r: (0, 0)),
            pl.BlockSpec((tr, seq), lambda b, r: (r, 0)),
            pl.BlockSpec((tr, seq), lambda b, r: (r, 0)),
        ],
        out_specs=pl.BlockSpec((tr, FNET_W), lambda b, r: (b * nr + r, 0)),
        scratch_shapes=[pltpu.VMEM((seq, FNET_W), BF16), pltpu.VMEM((seq, FNET_W), BF16)],
        compiler_params=_params("arbitrary", "arbitrary"),
        name=f"fnet_{seq}",
    )(proj, proj, csc, cos, sin)


def _short_kernel(x_ref, w_ref, b_ref, o_ref, *, seq):
    x = x_ref[...].astype(F32)
    n = x.shape[0]
    pos = lax.broadcasted_iota(jnp.int32, x.shape, 0) % seq
    prev = jnp.where(pos == 0, 0.0, pltpu.roll(x, 1, 0))
    nxt = jnp.where(pos == seq - 1, 0.0, pltpu.roll(x, n - 1, 0))
    y = b_ref[...] + prev * w_ref[0:1, :] + x * w_ref[1:2, :] + nxt * w_ref[2:3, :]
    o_ref[...] = y.astype(BF16)


def _short_conv(proj, w, b, row_start, n_rows, seq, tr=2048, tw=512):
    width = 3 * HYENA_W
    r0 = row_start // tr
    return pl.pallas_call(
        functools.partial(_short_kernel, seq=seq),
        out_shape=jax.ShapeDtypeStruct((n_rows, width), BF16),
        grid=(n_rows // tr, width // tw),
        in_specs=[
            pl.BlockSpec((tr, tw), lambda s, c: (r0 + s, OFF_HV // tw + c)),
            pl.BlockSpec((HYENA_SHORT, tw), lambda s, c: (0, c)),
            pl.BlockSpec((1, tw), lambda s, c: (0, c)),
        ],
        out_specs=pl.BlockSpec((tr, tw), lambda s, c: (s, c)),
        compiler_params=_params("arbitrary", "arbitrary"),
        name=f"short_conv_{seq}",
    )(proj, w, b.reshape(1, width))


def _filt_mlp_kernel(feats_ref, w1_ref, b1_ref, w2_ref, b2_ref, fr_ref, o_ref):
    fr = fr_ref[...]
    h = jnp.sin(fr * (_dot_exact(feats_ref[...], w1_ref[...]) + b1_ref[...]))
    o_ref[...] = jnp.sin(fr * (_dot_exact(h, w2_ref[...]) + b2_ref[...]))


def _filt_kernel(h_ref, t_ref, w3f_ref, w3b_ref, b3f_ref, b3b_ref, dl_ref, sum_ref, dif_ref, nyq_ref):
    h = h_ref[...]
    decay = jnp.exp(-t_ref[...] * dl_ref[...])
    hf = (_dot_exact(h, w3f_ref[...]) + b3f_ref[...]) * decay
    hb = (_dot_exact(h, w3b_ref[...]) + b3b_ref[...]) * decay
    den = jnp.sum(jnp.abs(hf), axis=0, keepdims=True) + jnp.sum(jnp.abs(hb), axis=0, keepdims=True) + EPS
    hs = (hf + hb) / den
    sum_ref[...] = hs.astype(BF16)
    dif_ref[...] = ((hf - hb) / den).astype(BF16)
    sign = (1 - 2 * (lax.broadcasted_iota(jnp.int32, hs.shape, 0) % 2)).astype(F32)
    nyq_ref[...] = jnp.sum(hs * sign, axis=0, keepdims=True)


def _hyena_filters(seq, w1, b1, w2, b2, w3, b3, freq, tw=512):
    t = jnp.arange(seq, dtype=F32)[:, None] / seq
    bands = jnp.arange(1, HYENA_BANDS + 1, dtype=F32)[None, :]
    feats = jnp.concatenate([t, jnp.cos(2 * math.pi * t * bands), jnp.sin(2 * math.pi * t * bands),
                             jnp.zeros((seq, HYENA_POS_PAD - HYENA_POS_DIM), F32)], axis=-1)
    w1p = jnp.concatenate([w1, jnp.zeros((HYENA_POS_PAD - HYENA_POS_DIM, HYENA_FFN_W), F32)], axis=0)
    deltas = jnp.abs(jnp.linspace(HYENA_MIN_DECAY, HYENA_MAX_DECAY, HYENA_W, dtype=F32))[None, :]
    hidden = pl.pallas_call(
        _filt_mlp_kernel,
        out_shape=jax.ShapeDtypeStruct((seq, HYENA_FFN_W), F32),
        name=f"hyena_filter_mlp_{seq}",
    )(feats, w1p, b1.reshape(1, -1), w2, b2.reshape(1, -1), freq.reshape(1, -1))
    nw = HYENA_W // tw
    small = lambda shape: pl.BlockSpec(shape, lambda o, c: (0, 0))
    out_sds = jax.ShapeDtypeStruct((seq, HYENA_ORDER * HYENA_W), BF16)
    fwd = lambda o, c: (0, (2 * o) * nw + c)
    bwd = lambda o, c: (0, (2 * o + 1) * nw + c)
    b3r = b3.reshape(1, -1)
    return pl.pallas_call(
        _filt_kernel,
        out_shape=(out_sds, out_sds, jax.ShapeDtypeStruct((1, HYENA_ORDER * HYENA_W), F32)),
        grid=(HYENA_ORDER, nw),
        in_specs=[
            small((seq, HYENA_FFN_W)), small((seq, 1)),
            pl.BlockSpec((HYENA_FFN_W, tw), fwd), pl.BlockSpec((HYENA_FFN_W, tw), bwd),
            pl.BlockSpec((1, tw), fwd), pl.BlockSpec((1, tw), bwd),
            pl.BlockSpec((1, tw), lambda o, c: (0, c)),
        ],
        out_specs=(pl.BlockSpec((seq, tw), lambda o, c: (0, o * nw + c)),
                   pl.BlockSpec((seq, tw), lambda o, c: (0, o * nw + c)),
                   pl.BlockSpec((1, tw), lambda o, c: (0, o * nw + c))),
        compiler_params=_params("arbitrary", "arbitrary"),
        name=f"hyena_filters_{seq}",
    )(hidden, t, w3, w3, b3r, b3r, deltas)


def _conv_tables(seq):
    i = jnp.arange(seq, dtype=jnp.int32)
    nyq = (1 - 2 * (i % 2)).astype(F32)
    cos, sin = _cos_sin_table(seq, seq, 2 * seq)
    sin_f = jnp.where(i[:, None] == 0, nyq[None, :], sin)
    sin_t = jnp.where(i[None, :] == 0, nyq[:, None], sin)
    return cos.astype(BF16), sin_f.astype(BF16), sin_t.astype(BF16)


def _filter_bins(uc_ref, us_ref, nyq_ref, first_chunk):
    gr = uc_ref[...]
    packed = jnp.logical_and(lax.broadcasted_iota(jnp.int32, gr.shape, 0) == 0, first_chunk)
    gi = jnp.where(packed, 0.0, us_ref[...])
    gn = jnp.where(packed, nyq_ref[...], gr)
    wgt = jnp.where(packed, 1.0, 2.0)
    return gr * wgt, gi * wgt, gn * wgt


def _spectrum_product(cos_ref, sin_ref, z, bins):
    gr, gi, gn = bins
    zr = _dot(cos_ref[...], z)
    zi = _dot(sin_ref[...], z)
    return (zr * gr - zi * gi).astype(BF16), (zr * gi + zi * gn).astype(BF16)


def _conv_epilogue(y, z, bias_ref, xm, gate, n):
    y = (y * (1.0 / n) + z.astype(F32) * bias_ref[...]) * xm.astype(F32)
    if gate is not None:
        y = y * _silu(gate.astype(F32))
    return y.astype(BF16)


def _lconv_short_kernel(*refs, seq, n_seq_blk, gated):
    refs = list(refs)
    z_ref, xm_ref = refs[:2]
    gate_ref = refs[2] if gated else None
    bias_ref, cos_ref, sin_ref, sin_t_ref, uc_ref, us_ref, nyq_ref, o_ref = refs[2 + gated:]
    bins = _filter_bins(uc_ref, us_ref, nyq_ref, True)
    for s in range(n_seq_blk):
        rows = slice(s * seq, (s + 1) * seq)
        z = z_ref[rows, :]
        pr, pi = _spectrum_product(cos_ref, sin_ref, z, bins)
        y = _dot(cos_ref[...], pr) + _dot(sin_t_ref[...], pi)
        o_ref[rows, :] = _conv_epilogue(y, z, bias_ref, xm_ref[rows, :],
                                        gate_ref[rows, :] if gated else None, 2 * seq)


def _lconv_long_kernel(*refs, seq, chunk, gated):
    refs = list(refs)
    z_ref, xm_ref = refs[:2]
    gate_ref = refs[2] if gated else None
    bias_ref, cos_ref, sin_ref, sin_t_ref, uc_ref, us_ref, nyq_ref, o_ref, pr_s, pi_s = refs[2 + gated:]
    n_chunks = seq // chunk
    p = pl.program_id(2)

    @pl.when(p < n_chunks)
    def _():
        bins = _filter_bins(uc_ref, us_ref, nyq_ref, p == 0)
        pr, pi = _spectrum_product(cos_ref, sin_ref, z_ref[...], bins)
        rows = pl.ds(pl.multiple_of(p * chunk, chunk), chunk)
        pr_s[rows, :] = pr
        pi_s[rows, :] = pi

    @pl.when(p >= n_chunks)
    def _():
        y = _dot(cos_ref[...], pr_s[...]) + _dot(sin_t_ref[...], pi_s[...])
        rows = pl.ds(pl.multiple_of((p - n_chunks) * chunk, chunk), chunk)
        o_ref[...] = _conv_epilogue(y, z_ref[rows, :], bias_ref, xm_ref[...],
                                    gate_ref[...] if gated else None, 2 * seq)


def _long_conv(z, z_col, xm, xm_col, gate, gate_row0, bias, idx, order, tables, spectrum,
               n_seq, seq, tw, fc, seq_blk):
    nw, nf = HYENA_W // tw, seq // fc
    gated = gate is not None
    out_shape = jax.ShapeDtypeStruct((n_seq * seq, HYENA_W), BF16)
    bias_spec = pl.BlockSpec((None, 1, tw), lambda s, w, p: (idx, 0, w))
    nyq_spec = pl.BlockSpec((1, tw), lambda s, w, p: (0, order * nw + w))
    if nf == 1:
        tr = seq_blk * seq
        tok = lambda col, r0=0: pl.BlockSpec((tr, tw), lambda s, w, p: (r0 + s, col * nw + w))
        taps = pl.BlockSpec((seq, tw), lambda s, w, p: (0, order * nw + w))
        table = pl.BlockSpec((seq, seq), lambda s, w, p: (0, 0))
        in_specs = [tok(z_col), tok(xm_col)] + ([tok(OFF_HGATE // HYENA_W, gate_row0 // tr)] if gated else [])
        in_specs += [bias_spec, table, table, table, taps, taps, nyq_spec]
        kern = functools.partial(_lconv_short_kernel, seq=seq, n_seq_blk=seq_blk, gated=gated)
        grid, scratch = (n_seq // seq_blk, nw, 1), []
        out_spec = pl.BlockSpec((tr, tw), lambda s, w, p: (s, w))
    else:
        assert seq_blk == 1
        fwd_chunk = lambda p: jnp.minimum(p, nf - 1)
        out_chunk = lambda p: jnp.maximum(p - nf, 0)
        rows = lambda col, r0=0: pl.BlockSpec((fc, tw), lambda s, w, p: (r0 + s * nf + out_chunk(p), col * nw + w))
        taps = pl.BlockSpec((fc, tw), lambda s, w, p: (fwd_chunk(p), order * nw + w))
        in_specs = [pl.BlockSpec((seq, tw), lambda s, w, p: (s, z_col * nw + w)), rows(xm_col)]
        in_specs += [rows(OFF_HGATE // HYENA_W, gate_row0 // fc)] if gated else []
        in_specs += [
            bias_spec,
            pl.BlockSpec((fc, seq), lambda s, w, p: (p % nf, 0)),
            pl.BlockSpec((fc, seq), lambda s, w, p: (fwd_chunk(p), 0)),
            pl.BlockSpec((fc, seq), lambda s, w, p: (out_chunk(p), 0)),
            taps, taps, nyq_spec,
        ]
        kern = functools.partial(_lconv_long_kernel, seq=seq, chunk=fc, gated=gated)
        grid, scratch = (n_seq, nw, 2 * nf), [pltpu.VMEM((seq, tw), BF16), pltpu.VMEM((seq, tw), BF16)]
        out_spec = pl.BlockSpec((fc, tw), lambda s, w, p: (s * nf + out_chunk(p), w))
    args = [z, xm] + ([gate] if gated else []) + [bias, *tables, *spectrum]
    return pl.pallas_call(
        kern,
        out_shape=out_shape,
        grid=grid,
        in_specs=in_specs,
        out_specs=out_spec,
        scratch_shapes=scratch,
        compiler_params=_params("arbitrary", "arbitrary", "arbitrary"),
        name=f"long_conv_{seq}_{order}",
    )(*args)


def _merge_kernel(ac_ref, al_ref, fc_ref, fl_ref, hc_ref, hl_ref, wa_ref, wf_ref, wh_ref,
                  ga_ref, gf_ref, gh_ref, o_ref, wab_ref, wfb_ref, whb_ref, *, n_ctx_tiles):
    i = pl.program_id(1)

    @pl.when(i == 0)
    def _():
        wab_ref[...] = wa_ref[...].astype(BF16)
        wfb_ref[...] = wf_ref[...].astype(BF16)
        whb_ref[...] = wh_ref[...].astype(BF16)

    def body(a_ref, f_ref, h_ref):
        m = jax.nn.sigmoid(ga_ref[...].astype(F32)) * _dot(a_ref[...], wab_ref[...])
        m = m + jax.nn.sigmoid(gf_ref[...].astype(F32)) * _dot(f_ref[...], wfb_ref[...])
        m = m + jax.nn.sigmoid(gh_ref[...].astype(F32)) * _dot(h_ref[...], whb_ref[...])
        o_ref[...] = m.astype(BF16)

    _by_group(i, n_ctx_tiles, body, (ac_ref, fc_ref, hc_ref), (al_ref, fl_ref, hl_ref))


def _merge(attn, fnet, hy, w_a, w_f, w_h, layer, proj, n_ctx, tm=512, tn=1024):
    t = proj.shape[0]
    nct = n_ctx // tm
    act = lambda w: _pair_specs((tm, w), nct, inner=True)
    wgt = lambda k: pl.BlockSpec((None, k, tn), lambda j, i: (layer, 0, j), pipeline_mode=pl.Buffered(1))
    gate = lambda off: pl.BlockSpec((tm, tn), lambda j, i: (i, off // tn + j))
    return pl.pallas_call(
        functools.partial(_merge_kernel, n_ctx_tiles=nct),
        out_shape=jax.ShapeDtypeStruct((t, D_MODEL), BF16),
        grid=(D_MODEL // tn, t // tm),
        in_specs=act(ATTN_W) + act(FNET_W) + act(HYENA_W) + [wgt(ATTN_W), wgt(FNET_W), wgt(HYENA_W),
                                                             gate(OFF_GA), gate(OFF_GF), gate(OFF_GH)],
        out_specs=pl.BlockSpec((tm, tn), lambda j, i: (i, j)),
        scratch_shapes=[pltpu.VMEM((ATTN_W, tn), BF16), pltpu.VMEM((FNET_W, tn), BF16),
                        pltpu.VMEM((HYENA_W, tn), BF16)],
        compiler_params=_params("arbitrary", "arbitrary"),
        name="merge",
    )(*attn, *fnet, *hy, w_a, w_f, w_h, proj, proj, proj)


def kernel(x_prompt, x_sample, cache_k, cache_v, c, c_ctx, w_mod, b_mod, g_pre, w_in, q_norm, k_norm, hy_short_w, hy_short_b, hy_ffn_w1, hy_ffn_b1, hy_ffn_w2, hy_ffn_b2, hy_ffn_w3, hy_ffn_b3, hy_sin_freq, hy_bias, w_attn_o, w_fnet_o, w_hy_o, w_out, g_post):
    nb, seq, _ = x_prompt.shape
    db, dseq, _ = x_sample.shape
    assert 1 + db <= MOD_ROWS
    n_ctx, n_lat = nb * seq, db * dseq

    cond = jnp.concatenate([c_ctx[None, :], c, jnp.zeros((MOD_ROWS - 1 - db, D_MODEL), F32)], axis=0)
    mod = _modulation(cond, w_mod, b_mod)

    w_kv_b = w_in[:, :, W_OFF_K:W_OFF_K + W_KV].astype(BF16)
    cos2, sin2 = _rope_tables(dseq)
    fnet_tables = {s: _fnet_tables(s) for s in (seq, dseq)}
    conv_tables = {s: _conv_tables(s) for s in (seq, dseq)}
    bias = hy_bias.reshape(DEPTH * HYENA_ORDER, 1, HYENA_W)
    conv_tiles = {seq: (HYENA_W, seq, 4), dseq: (512, 512, 1)}
    proj_tm, proj_tn = 512, 1024
    skip_kv = lambda j: jnp.where(j >= W_OFF_K // proj_tn, j + W_KV // proj_tn, j)

    xs = (x_prompt.reshape(n_ctx, D_MODEL), 0, x_sample.reshape(n_lat, D_MODEL), 0)
    new_k, new_v = [], []
    mod3s = [mod[l].reshape(MOD_ROWS, 1, 3 * D_MODEL) for l in range(DEPTH)]
    h = _prenorm(*xs, g_pre[0], mod3s[0], n_ctx, n_lat, dseq)
    for l in range(DEPTH):
        mod3 = mod3s[l]
        proj = _matmul_wcast(h, w_in, l, BF16, proj_tm, proj_tn, "in_proj", n_out=PROJ_W, col_map=skip_kv)
        k, v = _kv_proj(h, w_kv_b, l, k_norm[l])
        new_k.append(k)
        new_v.append(v)

        attn = (_attention_ctx(proj, k, v, q_norm[l], nb, seq),
                _attention_lat(proj, k, v, cache_k, cache_v, l, q_norm[l], cos2, sin2, n_ctx, db, dseq))

        fnet, hy = [], []
        for row_start, n_seq, s in ((0, nb, seq), (n_ctx, db, dseq)):
            fnet.append(_fnet(proj, fnet_tables[s], row_start, n_seq, s))
            u = _short_conv(proj, hy_short_w[l], hy_short_b[l], row_start, n_seq * s, s)
            tap_sum, tap_dif, tap_nyq = _hyena_filters(s, hy_ffn_w1[l], hy_ffn_b1[l], hy_ffn_w2[l], hy_ffn_b2[l],
                                                       hy_ffn_w3[l], hy_ffn_b3[l], hy_sin_freq[l])
            cos, sin_f, _ = conv_tables[s]
            spec_tm = min(1024, s)
            spectrum = (_matmul(cos, tap_sum, F32, spec_tm, 1024, f"filter_spectrum_cos_{s}"),
                        _matmul(sin_f, tap_dif, F32, spec_tm, 1024, f"filter_spectrum_sin_{s}"), tap_nyq)
            tw, fc, blk = conv_tiles[s]
            z1 = _long_conv(u, 0, u, 1, None, 0, bias, l * HYENA_ORDER, 0, conv_tables[s],
                            spectrum, n_seq, s, tw, fc, blk)
            hy.append(_long_conv(z1, 0, u, 2, proj, row_start, bias, l * HYENA_ORDER + 1, 1, conv_tables[s],
                                 spectrum, n_seq, s, tw, fc, blk))

        merged = _merge(attn, fnet, hy, w_attn_o, w_fnet_o, w_hy_o, l, proj, n_ctx)
        r = _matmul_wcast(merged, w_out, l, BF16, proj_tm, proj_tn, "out_proj")
        if l < DEPTH - 1:
            x_new, h = _post(*xs, r, 0, g_post[l], mod3, n_ctx, n_lat, dseq, next_norm=(g_pre[l + 1], mod3s[l + 1]))
            xs = (x_new, 0, x_new, n_ctx)
        else:
            y_prompt = _post(*xs, r, 0, g_post[l], mod3, n_ctx, 0, dseq)
            y_sample = _post(*xs, r, n_ctx, g_post[l], mod3, 0, n_lat, dseq)

    cache_k_new, cache_v_new = _new_cache(new_k, new_v, nb, seq)
    return (y_prompt.reshape(nb, seq, D_MODEL), y_sample.reshape(db, dseq, D_MODEL), cache_k_new, cache_v_new)
```

```python
import functools
import math

import jax
import jax.numpy as jnp
from jax import lax
from jax.experimental import pallas as pl
from jax.experimental.pallas import tpu as pltpu

F32 = jnp.float32
BF16 = jnp.bfloat16

D_MODEL = 4096
DEPTH = 2
GRID_W = 64
HEAD_DIM = 128
N_HEADS = 16
N_KV_HEADS = 4
KV_GROUP = N_HEADS // N_KV_HEADS
ATTN_W = N_HEADS * HEAD_DIM
KV_W = N_KV_HEADS * HEAD_DIM
GROUP_W = KV_GROUP * HEAD_DIM
ATTN_SCALE = HEAD_DIM ** -0.5
Q_SCALE = ATTN_SCALE * math.log2(math.e)
ROPE_THETA = 10000.0
ROPE_PAIRS_AXIS = HEAD_DIM // 4
FNET_W = D_MODEL // 4
FNET_GROUPS = 4
FNET_GROUP_W = FNET_W // FNET_GROUPS
HYENA_W = D_MODEL // 4
HYENA_ORDER = 2
HYENA_SHORT = 3
HYENA_BANDS = 16
HYENA_POS_DIM = 1 + 2 * HYENA_BANDS
HYENA_POS_PAD = 128
HYENA_FFN_W = 64
HYENA_MIN_DECAY = math.log(1e-2) / 0.3
HYENA_MAX_DECAY = math.log(1e-2) / 1.5
EPS = 1e-6

W_OFF_K = ATTN_W
W_KV = 2 * KV_W
OFF_Q = 0
OFF_AGATE = OFF_Q + ATTN_W
OFF_FIN = OFF_AGATE + ATTN_W
OFF_FGATE = OFF_FIN + FNET_W
OFF_HV = OFF_FGATE + FNET_W
OFF_HX1 = OFF_HV + HYENA_W
OFF_HX2 = OFF_HX1 + HYENA_W
OFF_HGATE = OFF_HX2 + HYENA_W
OFF_GA = OFF_HGATE + HYENA_W
OFF_GF = OFF_GA + D_MODEL
OFF_GH = OFF_GF + D_MODEL
PROJ_W = OFF_GH + D_MODEL

MOD_ROWS = 8
ROW_TILE = 256
ATTN_ROWS = 256
TABLE_SPLIT = 64
VMEM_LIMIT = 56 * 1024 * 1024


def _params(*sem):
    return pltpu.CompilerParams(dimension_semantics=sem, vmem_limit_bytes=VMEM_LIMIT)


def _silu(x):
    return x * jax.nn.sigmoid(x)


def _dot(a, b):
    return jnp.dot(a, b, preferred_element_type=F32)


def _dot_exact(a, b):
    return jnp.dot(a, b, preferred_element_type=F32, precision=lax.Precision.HIGHEST)


def _pair_specs(block, n_ctx_tiles, ctx_tile0=0, lat_tile0=0, col=0, inner=False):
    def ctx(i):
        return (ctx_tile0 + jnp.clip(i, 0, max(n_ctx_tiles - 1, 0)), col)

    def lat(i):
        return (lat_tile0 + jnp.maximum(i - n_ctx_tiles, 0), col)

    if inner:
        return [pl.BlockSpec(block, lambda j, i: ctx(i)), pl.BlockSpec(block, lambda j, i: lat(i))]
    return [pl.BlockSpec(block, lambda i: ctx(i)), pl.BlockSpec(block, lambda i: lat(i))]


def _by_group(i, n_ctx_tiles, body, ctx_refs, lat_refs):
    @pl.when(i < n_ctx_tiles)
    def _():
        body(*ctx_refs)

    @pl.when(i >= n_ctx_tiles)
    def _():
        body(*lat_refs)


def _mod_kernel(c_ref, w_ref, b_ref, o_ref):
    s = _silu(c_ref[...]).astype(BF16)
    o_ref[0] = _dot(s, w_ref[0].astype(BF16)) + b_ref[0]


def _modulation(cond, w_mod, b_mod, tn=512):
    n = w_mod.shape[2]
    return pl.pallas_call(
        _mod_kernel,
        out_shape=jax.ShapeDtypeStruct((DEPTH, MOD_ROWS, n), F32),
        grid=(DEPTH, n // tn),
        in_specs=[
            pl.BlockSpec((MOD_ROWS, D_MODEL), lambda l, j: (0, 0)),
            pl.BlockSpec((1, D_MODEL, tn), lambda l, j: (l, 0, j)),
            pl.BlockSpec((1, 1, tn), lambda l, j: (l, 0, j)),
        ],
        out_specs=pl.BlockSpec((1, MOD_ROWS, tn), lambda l, j: (l, 0, j)),
        compiler_params=_params("arbitrary", "arbitrary"),
        name="modulation",
    )(cond, w_mod, b_mod.reshape(DEPTH, 1, n))


def _mod_spec(chunk, n_ctx_tiles, tiles_per_latent_seq):
    def row(i):
        return jnp.where(i < n_ctx_tiles, 0, 1 + (i - n_ctx_tiles) // tiles_per_latent_seq)
    return pl.BlockSpec((1, 1, D_MODEL), lambda i: (row(i), 0, chunk))


def _modulated_norm(x, g_ref, shift_ref, scale_ref):
    y = x * lax.rsqrt(jnp.mean(x * x, axis=-1, keepdims=True) + EPS) * g_ref[...]
    return (y * (1.0 + scale_ref[0]) + shift_ref[0]).astype(BF16)


def _prenorm_kernel(xc_ref, xl_ref, g_ref, shift_ref, scale_ref, o_ref, *, n_ctx_tiles):
    def body(x_ref):
        o_ref[...] = _modulated_norm(x_ref[...], g_ref, shift_ref, scale_ref)

    _by_group(pl.program_id(0), n_ctx_tiles, body, (xc_ref,), (xl_ref,))


def _prenorm(x_ctx, ctx_row0, x_lat, lat_row0, g, mod3, n_ctx, n_lat, lat_seq):
    tm = 2 * ROW_TILE
    nct = n_ctx // tm
    mod = functools.partial(_mod_spec, n_ctx_tiles=nct, tiles_per_latent_seq=lat_seq // tm)
    return pl.pallas_call(
        functools.partial(_prenorm_kernel, n_ctx_tiles=nct),
        out_shape=jax.ShapeDtypeStruct((n_ctx + n_lat, D_MODEL), BF16),
        grid=((n_ctx + n_lat) // tm,),
        in_specs=_pair_specs((tm, D_MODEL), nct, ctx_row0 // tm, lat_row0 // tm)
        + [pl.BlockSpec((1, D_MODEL), lambda i: (0, 0)), mod(0), mod(1)],
        out_specs=pl.BlockSpec((tm, D_MODEL), lambda i: (i, 0)),
        compiler_params=_params("arbitrary"),
        name="prenorm",
    )(x_ctx, x_lat, g.reshape(1, D_MODEL), mod3, mod3)


def _post_kernel(xc_ref, xl_ref, r_ref, g_ref, gate_ref, *rest, n_ctx_tiles, with_next):
    if with_next:
        g_next_ref, shift_ref, scale_ref, o_ref, h_ref = rest
    else:
        (o_ref,) = rest

    def body(x_ref):
        r = r_ref[...].astype(F32)
        y = r * lax.rsqrt(jnp.mean(r * r, axis=-1, keepdims=True) + EPS) * g_ref[...]
        x = x_ref[...] + gate_ref[0] * y
        o_ref[...] = x
        if with_next:
            h_ref[...] = _modulated_norm(x, g_next_ref, shift_ref, scale_ref)

    _by_group(pl.program_id(0), n_ctx_tiles, body, (xc_ref,), (xl_ref,))


def _post(x_ctx, ctx_row0, x_lat, lat_row0, r, r_row0, g, mod3, n_ctx, n_lat, lat_seq, next_norm=None):
    tm = ROW_TILE
    nct = n_ctx // tm
    r0 = r_row0 // tm
    rows = n_ctx + n_lat
    mod = functools.partial(_mod_spec, n_ctx_tiles=nct, tiles_per_latent_seq=lat_seq // tm)
    vec = pl.BlockSpec((1, D_MODEL), lambda i: (0, 0))
    tile = pl.BlockSpec((tm, D_MODEL), lambda i: (i, 0))
    in_specs = _pair_specs((tm, D_MODEL), nct, ctx_row0 // tm, lat_row0 // tm) + [
        pl.BlockSpec((tm, D_MODEL), lambda i: (r0 + i, 0)), vec, mod(2)]
    args = [x_ctx, x_lat, r, g.reshape(1, D_MODEL), mod3]
    out_shape, out_specs = jax.ShapeDtypeStruct((rows, D_MODEL), F32), tile
    if next_norm is not None:
        g_next, mod3_next = next_norm
        in_specs += [vec, mod(0), mod(1)]
        args += [g_next.reshape(1, D_MODEL), mod3_next, mod3_next]
        out_shape, out_specs = (out_shape, jax.ShapeDtypeStruct((rows, D_MODEL), BF16)), (tile, tile)
    return pl.pallas_call(
        functools.partial(_post_kernel, n_ctx_tiles=nct, with_next=next_norm is not None),
        out_shape=out_shape,
        grid=(rows // tm,),
        in_specs=in_specs,
        out_specs=out_specs,
        compiler_params=_params("arbitrary"),
        name="post",
    )(*args)


def _mm_kernel(a_ref, b_ref, o_ref):
    o_ref[...] = _dot(a_ref[...], b_ref[...]).astype(o_ref.dtype)


def _matmul(a, b, out_dtype, tm, tn, name):
    m, k = a.shape
    n = b.shape[1]
    return pl.pallas_call(
        _mm_kernel,
        out_shape=jax.ShapeDtypeStruct((m, n), out_dtype),
        grid=(m // tm, n // tn),
        in_specs=[pl.BlockSpec((tm, k), lambda i, j: (i, 0)), pl.BlockSpec((k, tn), lambda i, j: (0, j))],
        out_specs=pl.BlockSpec((tm, tn), lambda i, j: (i, j)),
        compiler_params=_params("arbitrary", "arbitrary"),
        name=name,
    )(a, b)


def _mm_wcast_kernel(a_ref, w_ref, o_ref, wb_ref):
    @pl.when(pl.program_id(1) == 0)
    def _():
        wb_ref[...] = w_ref[...].astype(BF16)

    o_ref[...] = _dot(a_ref[...], wb_ref[...]).astype(o_ref.dtype)


def _matmul_wcast(a, w, layer, out_dtype, tm, tn, name, n_out=None, col_map=None):
    m, k = a.shape
    n = n_out or w.shape[-1]
    col = col_map or (lambda j: j)
    return pl.pallas_call(
        _mm_wcast_kernel,
        out_shape=jax.ShapeDtypeStruct((m, n), out_dtype),
        grid=(n // tn, m // tm),
        in_specs=[
            pl.BlockSpec((tm, k), lambda j, i: (i, 0)),
            pl.BlockSpec((None, k, tn), lambda j, i: (layer, 0, col(j))),
        ],
        out_specs=pl.BlockSpec((tm, tn), lambda j, i: (i, j)),
        scratch_shapes=[pltpu.VMEM((k, tn), BF16)],
        compiler_params=_params("arbitrary", "arbitrary"),
        name=name,
    )(a, w)


def _kv_kernel(h_ref, w_ref, kn_ref, k_ref, v_ref):
    acc = _dot(h_ref[...], w_ref[...])
    for hd in range(N_KV_HEADS):
        sl = slice(hd * HEAD_DIM, (hd + 1) * HEAD_DIM)
        k_ref[:, sl] = _norm_head(acc[:, sl], kn_ref[...])
    v_ref[...] = acc[:, KV_W:]


def _kv_proj(h, w_kv, layer, k_norm, tm=512):
    t = h.shape[0]
    return pl.pallas_call(
        _kv_kernel,
        out_shape=(jax.ShapeDtypeStruct((t, KV_W), F32), jax.ShapeDtypeStruct((t, KV_W), F32)),
        grid=(t // tm,),
        in_specs=[
            pl.BlockSpec((tm, D_MODEL), lambda i: (i, 0)),
            pl.BlockSpec((None, D_MODEL, W_KV), lambda i: (layer, 0, 0)),
            pl.BlockSpec((1, HEAD_DIM), lambda i: (0, 0)),
        ],
        out_specs=(pl.BlockSpec((tm, KV_W), lambda i: (i, 0)), pl.BlockSpec((tm, KV_W), lambda i: (i, 0))),
        compiler_params=_params("arbitrary"),
        name="kv_proj",
    )(h, w_kv, k_norm.reshape(1, HEAD_DIM))


def _cache_kernel(*refs):
    k_refs, v_refs, (ko_ref, vo_ref) = refs[:DEPTH], refs[DEPTH:2 * DEPTH], refs[2 * DEPTH:]
    for l in range(DEPTH):
        @pl.when(pl.program_id(0) == l)
        def _():
            for hd in range(N_KV_HEADS):
                sl = slice(hd * HEAD_DIM, (hd + 1) * HEAD_DIM)
                ko_ref[:, hd, :] = k_refs[l][:, sl]
                vo_ref[:, hd, :] = v_refs[l][:, sl]


def _new_cache(ks, vs, n_seq, seq):
    def layer_spec(l):
        return pl.BlockSpec((seq, KV_W), lambda cur, b: (jnp.where(cur == l, b, 0), 0))

    out_sds = jax.ShapeDtypeStruct((n_seq, DEPTH, seq, N_KV_HEADS, HEAD_DIM), F32)
    out_spec = pl.BlockSpec((None, None, seq, N_KV_HEADS, HEAD_DIM), lambda cur, b: (b, cur, 0, 0, 0))
    return pl.pallas_call(
        _cache_kernel,
        out_shape=(out_sds, out_sds),
        grid=(DEPTH, n_seq),
        in_specs=[layer_spec(l) for l in range(DEPTH)] * 2,
        out_specs=(out_spec, out_spec),
        compiler_params=_params("arbitrary", "arbitrary"),
        name="new_cache",
    )(*ks, *vs)


def _softmax_pv(s, v):
    m = jnp.max(s, axis=-1, keepdims=True)
    p = jnp.exp2(s - m)
    l = jnp.sum(p, axis=-1, keepdims=True)
    return _dot(p.astype(BF16), v) / l


def _qk(q, k):
    return lax.dot_general(q, k, (((1,), (1,)), ((), ())), preferred_element_type=F32)


def _norm_head(q, g):
    return q * lax.rsqrt(jnp.mean(q * q, axis=-1, keepdims=True) + EPS) * g


def _rope(x, cos2, sin2):
    return x * cos2 + pltpu.roll(x, HEAD_DIM // 2, 1) * sin2


def _attn_ctx_kernel(q_ref, k_ref, v_ref, ag_ref, qn_ref, o_ref):
    seq = q_ref.shape[0]
    for hd in range(N_KV_HEADS):
        kv = slice(hd * HEAD_DIM, (hd + 1) * HEAD_DIM)
        cols = [slice((hd * KV_GROUP + g) * HEAD_DIM, (hd * KV_GROUP + g + 1) * HEAD_DIM) for g in range(KV_GROUP)]
        q = jnp.concatenate([(_norm_head(q_ref[:, c].astype(F32), qn_ref[...]) * Q_SCALE).astype(BF16)
                             for c in cols], axis=0)
        o = _softmax_pv(_qk(q, k_ref[:, kv].astype(BF16)), v_ref[:, kv].astype(BF16))
        for g, c in enumerate(cols):
            o_ref[:, c] = (o[g * seq:(g + 1) * seq] * _silu(ag_ref[:, c].astype(F32))).astype(BF16)


def _attention_ctx(proj, k, v, q_norm, n_seq, seq):
    return pl.pallas_call(
        _attn_ctx_kernel,
        out_shape=jax.ShapeDtypeStruct((n_seq * seq, ATTN_W), BF16),
        grid=(n_seq,),
        in_specs=[
            pl.BlockSpec((seq, ATTN_W), lambda b: (b, OFF_Q // ATTN_W)),
            pl.BlockSpec((seq, KV_W), lambda b: (b, 0)),
            pl.BlockSpec((seq, KV_W), lambda b: (b, 0)),
            pl.BlockSpec((seq, ATTN_W), lambda b: (b, OFF_AGATE // ATTN_W)),
            pl.BlockSpec((1, HEAD_DIM), lambda b: (0, 0)),
        ],
        out_specs=pl.BlockSpec((seq, ATTN_W), lambda b: (b, 0)),
        compiler_params=_params("arbitrary"),
        name="attn_ctx",
    )(proj, k, v, proj, q_norm.reshape(1, HEAD_DIM))


def _attn_lat_kernel(q_ref, k_ref, v_ref, ck_ref, cv_ref, ag_ref, qn_ref, cosq_ref, sinq_ref,
                     cosk_ref, sink_ref, o_ref, k_s, v_s, *, seq):
    @pl.when(pl.program_id(2) == 0)
    def _():
        k_s[:seq] = _rope(k_ref[...], cosk_ref[...], sink_ref[...]).astype(BF16)
        k_s[seq:] = ck_ref[...].astype(BF16)
        v_s[:seq] = v_ref[...].astype(BF16)
        v_s[seq:] = cv_ref[...].astype(BF16)

    k = k_s[...]
    v = v_s[...]
    for g in range(KV_GROUP):
        sl = slice(g * HEAD_DIM, (g + 1) * HEAD_DIM)
        for r0 in range(0, q_ref.shape[0], ATTN_ROWS):
            rows = slice(r0, r0 + ATTN_ROWS)
            q = _norm_head(q_ref[rows, sl].astype(F32), qn_ref[...])
            q = _rope(q, cosq_ref[rows, :], sinq_ref[rows, :]) * Q_SCALE
            o = _softmax_pv(_qk(q.astype(BF16), k), v)
            o_ref[rows, sl] = (o * _silu(ag_ref[rows, sl].astype(F32))).astype(BF16)


def _attention_lat(proj, k, v, cache_k, cache_v, layer, q_norm, cos2, sin2, n_ctx, n_seq, seq, tq=1024):
    past = cache_k.shape[2]
    ck = cache_k.reshape(cache_k.shape[0], DEPTH, past, KV_W)
    cv = cache_v.reshape(cache_v.shape[0], DEPTH, past, KV_W)
    nq = seq // tq
    row0, seq0 = n_ctx // tq, n_ctx // seq
    return pl.pallas_call(
        functools.partial(_attn_lat_kernel, seq=seq),
        out_shape=jax.ShapeDtypeStruct((n_seq * seq, ATTN_W), BF16),
        grid=(n_seq, N_KV_HEADS, nq),
        in_specs=[
            pl.BlockSpec((tq, GROUP_W), lambda b, h, i: (row0 + b * nq + i, OFF_Q // GROUP_W + h)),
            pl.BlockSpec((seq, HEAD_DIM), lambda b, h, i: (seq0 + b, h)),
            pl.BlockSpec((seq, HEAD_DIM), lambda b, h, i: (seq0 + b, h)),
            pl.BlockSpec((None, None, past, HEAD_DIM), lambda b, h, i: (b, layer, 0, h)),
            pl.BlockSpec((None, None, past, HEAD_DIM), lambda b, h, i: (b, layer, 0, h)),
            pl.BlockSpec((tq, GROUP_W), lambda b, h, i: (row0 + b * nq + i, OFF_AGATE // GROUP_W + h)),
            pl.BlockSpec((1, HEAD_DIM), lambda b, h, i: (0, 0)),
            pl.BlockSpec((tq, HEAD_DIM), lambda b, h, i: (i, 0)),
            pl.BlockSpec((tq, HEAD_DIM), lambda b, h, i: (i, 0)),
            pl.BlockSpec((seq, HEAD_DIM), lambda b, h, i: (0, 0)),
            pl.BlockSpec((seq, HEAD_DIM), lambda b, h, i: (0, 0)),
        ],
        out_specs=pl.BlockSpec((tq, GROUP_W), lambda b, h, i: (b * nq + i, h)),
        scratch_shapes=[pltpu.VMEM((seq + past, HEAD_DIM), BF16), pltpu.VMEM((seq + past, HEAD_DIM), BF16)],
        compiler_params=_params("arbitrary", "arbitrary", "arbitrary"),
        name="attn_lat",
    )(proj, k, v, ck, cv, proj, q_norm.reshape(1, HEAD_DIM), cos2, sin2, cos2, sin2)


def _rope_tables(seq):
    rows = seq // GRID_W
    row = jnp.repeat(jnp.arange(rows, dtype=F32), GRID_W)
    col = jnp.tile(jnp.arange(GRID_W, dtype=F32), rows)
    inv = ROPE_THETA ** (-jnp.arange(ROPE_PAIRS_AXIS, dtype=F32) / ROPE_PAIRS_AXIS)
    ang = jnp.concatenate([row[:, None] * inv, col[:, None] * inv], axis=-1)
    cos, sin = jnp.cos(ang), jnp.sin(ang)
    return jnp.concatenate([cos, cos], axis=-1), jnp.concatenate([-sin, sin], axis=-1)


def _dft_angle(i, j, n):
    return (2.0 * math.pi / n) * ((i[:, None] * j[None, :]) % n).astype(F32)


def _cos_sin_table(rows, cols, n):
    c = jnp.arange(cols, dtype=jnp.int32)
    hi = _dft_angle(TABLE_SPLIT * jnp.arange(rows // TABLE_SPLIT, dtype=jnp.int32), c, n)
    lo = _dft_angle(jnp.arange(TABLE_SPLIT, dtype=jnp.int32), c, n)
    ch, sh = jnp.cos(hi)[:, None, :], jnp.sin(hi)[:, None, :]
    cl, sl = jnp.cos(lo)[None, :, :], jnp.sin(lo)[None, :, :]
    return (ch * cl - sh * sl).reshape(rows, cols), (sh * cl + ch * sl).reshape(rows, cols)


def _fnet_kernel(u_ref, fg_ref, csc_ref, cos_ref, sin_ref, o_ref, tc_s, ts_s, *, scale):
    @pl.when(pl.program_id(1) == 0)
    def _():
        for g in range(FNET_GROUPS):
            sl = slice(g * FNET_GROUP_W, (g + 1) * FNET_GROUP_W)
            t = _dot(u_ref[:, sl], csc_ref[...])
            tc_s[:, sl] = t[:, :FNET_GROUP_W].astype(BF16)
            ts_s[:, sl] = t[:, FNET_GROUP_W:].astype(BF16)

    y = (_dot(cos_ref[...], tc_s[...]) - _dot(sin_ref[...], ts_s[...])) * scale
    o_ref[...] = (y * _silu(fg_ref[...].astype(F32))).astype(BF16)


def _fnet_tables(seq):
    j = jnp.arange(FNET_GROUP_W, dtype=jnp.int32)
    ang = _dft_angle(j, j, FNET_GROUP_W)
    csc = jnp.concatenate([jnp.cos(ang), jnp.sin(ang)], axis=1).astype(BF16)
    cos, sin = _cos_sin_table(seq, seq, seq)
    return csc, cos.astype(BF16), sin.astype(BF16)


def _fnet(proj, tables, row_start, n_seq, seq, tr=256):
    csc, cos, sin = tables
    nr = seq // tr
    s0, r0 = row_start // seq, row_start // tr
    return pl.pallas_call(
        functools.partial(_fnet_kernel, scale=1.0 / math.sqrt(seq * FNET_GROUP_W)),
        out_shape=jax.ShapeDtypeStruct((n_seq * seq, FNET_W), BF16),
        grid=(n_seq, nr),
        in_specs=[
            pl.BlockSpec((seq, FNET_W), lambda b, r: (s0 + b, OFF_FIN // FNET_W)),
            pl.BlockSpec((tr, FNET_W), lambda b, r: (r0 + b * nr + r, OFF_FGATE // FNET_W)),
            pl.BlockSpec((FNET_GROUP_W, 2 * FNET_GROUP_W), lambda b, r: (0, 0)),
            pl.BlockSpec((tr, seq), lambda b, r: (r, 0)),
            pl.BlockSpec((tr, seq), lambda b, r: (r, 0)),
        ],
        out_specs=pl.BlockSpec((tr, FNET_W), lambda b, r: (b * nr + r, 0)),
        scratch_shapes=[pltpu.VMEM((seq, FNET_W), BF16), pltpu.VMEM((seq, FNET_W), BF16)],
        compiler_params=_params("arbitrary", "arbitrary"),
        name=f"fnet_{seq}",
    )(proj, proj, csc, cos, sin)


def _short_kernel(x_ref, w_ref, b_ref, o_ref, *, seq):
    x = x_ref[...].astype(F32)
    n = x.shape[0]
    pos = lax.broadcasted_iota(jnp.int32, x.shape, 0) % seq
    prev = jnp.where(pos == 0, 0.0, pltpu.roll(x, 1, 0))
    nxt = jnp.where(pos == seq - 1, 0.0, pltpu.roll(x, n - 1, 0))
    y = b_ref[...] + prev * w_ref[0:1, :] + x * w_ref[1:2, :] + nxt * w_ref[2:3, :]
    o_ref[...] = y.astype(BF16)


def _short_conv(proj, w, b, row_start, n_rows, seq, tr=2048, tw=512):
    width = 3 * HYENA_W
    r0 = row_start // tr
    return pl.pallas_call(
        functools.partial(_short_kernel, seq=seq),
        out_shape=jax.ShapeDtypeStruct((n_rows, width), BF16),
        grid=(n_rows // tr, width // tw),
        in_specs=[
            pl.BlockSpec((tr, tw), lambda s, c: (r0 + s, OFF_HV // tw + c)),
            pl.BlockSpec((HYENA_SHORT, tw), lambda s, c: (0, c)),
            pl.BlockSpec((1, tw), lambda s, c: (0, c)),
        ],
        out_specs=pl.BlockSpec((tr, tw), lambda s, c: (s, c)),
        compiler_params=_params("arbitrary", "arbitrary"),
        name=f"short_conv_{seq}",
    )(proj, w, b.reshape(1, width))


def _filt_mlp_kernel(feats_ref, w1_ref, b1_ref, w2_ref, b2_ref, fr_ref, o_ref):
    fr = fr_ref[...]
    h = jnp.sin(fr * (_dot_exact(feats_ref[...], w1_ref[...]) + b1_ref[...]))
    o_ref[...] = jnp.sin(fr * (_dot_exact(h, w2_ref[...]) + b2_ref[...]))


def _filt_kernel(h_ref, t_ref, w3f_ref, w3b_ref, b3f_ref, b3b_ref, dl_ref, sum_ref, dif_ref, nyq_ref):
    h = h_ref[...]
    decay = jnp.exp(-t_ref[...] * dl_ref[...])
    hf = (_dot_exact(h, w3f_ref[...]) + b3f_ref[...]) * decay
    hb = (_dot_exact(h, w3b_ref[...]) + b3b_ref[...]) * decay
    den = jnp.sum(jnp.abs(hf), axis=0, keepdims=True) + jnp.sum(jnp.abs(hb), axis=0, keepdims=True) + EPS
    hs = (hf + hb) / den
    sum_ref[...] = hs.astype(BF16)
    dif_ref[...] = ((hf - hb) / den).astype(BF16)
    sign = (1 - 2 * (lax.broadcasted_iota(jnp.int32, hs.shape, 0) % 2)).astype(F32)
    nyq_ref[...] = jnp.sum(hs * sign, axis=0, keepdims=True)


def _hyena_filters(seq, w1, b1, w2, b2, w3, b3, freq, tw=512):
    t = jnp.arange(seq, dtype=F32)[:, None] / seq
    bands = jnp.arange(1, HYENA_BANDS + 1, dtype=F32)[None, :]
    feats = jnp.concatenate([t, jnp.cos(2 * math.pi * t * bands), jnp.sin(2 * math.pi * t * bands),
                             jnp.zeros((seq, HYENA_POS_PAD - HYENA_POS_DIM), F32)], axis=-1)
    w1p = jnp.concatenate([w1, jnp.zeros((HYENA_POS_PAD - HYENA_POS_DIM, HYENA_FFN_W), F32)], axis=0)
    deltas = jnp.abs(jnp.linspace(HYENA_MIN_DECAY, HYENA_MAX_DECAY, HYENA_W, dtype=F32))[None, :]
    hidden = pl.pallas_call(
        _filt_mlp_kernel,
        out_shape=jax.ShapeDtypeStruct((seq, HYENA_FFN_W), F32),
        name=f"hyena_filter_mlp_{seq}",
    )(feats, w1p, b1.reshape(1, -1), w2, b2.reshape(1, -1), freq.reshape(1, -1))
    nw = HYENA_W // tw
    small = lambda shape: pl.BlockSpec(shape, lambda o, c: (0, 0))
    out_sds = jax.ShapeDtypeStruct((seq, HYENA_ORDER * HYENA_W), BF16)
    fwd = lambda o, c: (0, (2 * o) * nw + c)
    bwd = lambda o, c: (0, (2 * o + 1) * nw + c)
    b3r = b3.reshape(1, -1)
    return pl.pallas_call(
        _filt_kernel,
        out_shape=(out_sds, out_sds, jax.ShapeDtypeStruct((1, HYENA_ORDER * HYENA_W), F32)),
        grid=(HYENA_ORDER, nw),
        in_specs=[
            small((seq, HYENA_FFN_W)), small((seq, 1)),
            pl.BlockSpec((HYENA_FFN_W, tw), fwd), pl.BlockSpec((HYENA_FFN_W, tw), bwd),
            pl.BlockSpec((1, tw), fwd), pl.BlockSpec((1, tw), bwd),
            pl.BlockSpec((1, tw), lambda o, c: (0, c)),
        ],
        out_specs=(pl.BlockSpec((seq, tw), lambda o, c: (0, o * nw + c)),
                   pl.BlockSpec((seq, tw), lambda o, c: (0, o * nw + c)),
                   pl.BlockSpec((1, tw), lambda o, c: (0, o * nw + c))),
        compiler_params=_params("arbitrary", "arbitrary"),
        name=f"hyena_filters_{seq}",
    )(hidden, t, w3, w3, b3r, b3r, deltas)


def _conv_tables(seq):
    i = jnp.arange(seq, dtype=jnp.int32)
    nyq = (1 - 2 * (i % 2)).astype(F32)
    cos, sin = _cos_sin_table(seq, seq, 2 * seq)
    sin_f = jnp.where(i[:, None] == 0, nyq[None, :], sin)
    sin_t = jnp.where(i[None, :] == 0, nyq[:, None], sin)
    return cos.astype(BF16), sin_f.astype(BF16), sin_t.astype(BF16)


def _filter_bins(uc_ref, us_ref, nyq_ref, first_chunk):
    gr = uc_ref[...]
    packed = jnp.logical_and(lax.broadcasted_iota(jnp.int32, gr.shape, 0) == 0, first_chunk)
    gi = jnp.where(packed, 0.0, us_ref[...])
    gn = jnp.where(packed, nyq_ref[...], gr)
    wgt = jnp.where(packed, 1.0, 2.0)
    return gr * wgt, gi * wgt, gn * wgt


def _spectrum_product(cos_ref, sin_ref, z, bins):
    gr, gi, gn = bins
    zr = _dot(cos_ref[...], z)
    zi = _dot(sin_ref[...], z)
    return (zr * gr - zi * gi).astype(BF16), (zr * gi + zi * gn).astype(BF16)


def _conv_epilogue(y, z, bias_ref, xm, gate, n):
    y = (y * (1.0 / n) + z.astype(F32) * bias_ref[...]) * xm.astype(F32)
    if gate is not None:
        y = y * _silu(gate.astype(F32))
    return y.astype(BF16)


def _lconv_short_kernel(*refs, seq, n_seq_blk, gated):
    refs = list(refs)
    z_ref, xm_ref = refs[:2]
    gate_ref = refs[2] if gated else None
    bias_ref, cos_ref, sin_ref, sin_t_ref, uc_ref, us_ref, nyq_ref, o_ref = refs[2 + gated:]
    bins = _filter_bins(uc_ref, us_ref, nyq_ref, True)
    for s in range(n_seq_blk):
        rows = slice(s * seq, (s + 1) * seq)
        z = z_ref[rows, :]
        pr, pi = _spectrum_product(cos_ref, sin_ref, z, bins)
        y = _dot(cos_ref[...], pr) + _dot(sin_t_ref[...], pi)
        o_ref[rows, :] = _conv_epilogue(y, z, bias_ref, xm_ref[rows, :],
                                        gate_ref[rows, :] if gated else None, 2 * seq)


def _lconv_long_kernel(*refs, seq, chunk, gated):
    refs = list(refs)
    z_ref, xm_ref = refs[:2]
    gate_ref = refs[2] if gated else None
    bias_ref, cos_ref, sin_ref, sin_t_ref, uc_ref, us_ref, nyq_ref, o_ref, pr_s, pi_s = refs[2 + gated:]
    n_chunks = seq // chunk
    p = pl.program_id(2)

    @pl.when(p < n_chunks)
    def _():
        bins = _filter_bins(uc_ref, us_ref, nyq_ref, p == 0)
        pr, pi = _spectrum_product(cos_ref, sin_ref, z_ref[...], bins)
        rows = pl.ds(pl.multiple_of(p * chunk, chunk), chunk)
        pr_s[rows, :] = pr
        pi_s[rows, :] = pi

    @pl.when(p >= n_chunks)
    def _():
        y = _dot(cos_ref[...], pr_s[...]) + _dot(sin_t_ref[...], pi_s[...])
        rows = pl.ds(pl.multiple_of((p - n_chunks) * chunk, chunk), chunk)
        o_ref[...] = _conv_epilogue(y, z_ref[rows, :], bias_ref, xm_ref[...],
                                    gate_ref[...] if gated else None, 2 * seq)


def _long_conv(z, z_col, xm, xm_col, gate, gate_row0, bias, idx, order, tables, spectrum,
               n_seq, seq, tw, fc, seq_blk):
    nw, nf = HYENA_W // tw, seq // fc
    gated = gate is not None
    out_shape = jax.ShapeDtypeStruct((n_seq * seq, HYENA_W), BF16)
    bias_spec = pl.BlockSpec((None, 1, tw), lambda s, w, p: (idx, 0, w))
    nyq_spec = pl.BlockSpec((1, tw), lambda s, w, p: (0, order * nw + w))
    if nf == 1:
        tr = seq_blk * seq
        tok = lambda col, r0=0: pl.BlockSpec((tr, tw), lambda s, w, p: (r0 + s, col * nw + w))
        taps = pl.BlockSpec((seq, tw), lambda s, w, p: (0, order * nw + w))
        table = pl.BlockSpec((seq, seq), lambda s, w, p: (0, 0))
        in_specs = [tok(z_col), tok(xm_col)] + ([tok(OFF_HGATE // HYENA_W, gate_row0 // tr)] if gated else [])
        in_specs += [bias_spec, table, table, table, taps, taps, nyq_spec]
        kern = functools.partial(_lconv_short_kernel, seq=seq, n_seq_blk=seq_blk, gated=gated)
        grid, scratch = (n_seq // seq_blk, nw, 1), []
        out_spec = pl.BlockSpec((tr, tw), lambda s, w, p: (s, w))
    else:
        assert seq_blk == 1
        fwd_chunk = lambda p: jnp.minimum(p, nf - 1)
        out_chunk = lambda p: jnp.maximum(p - nf, 0)
        rows = lambda col, r0=0: pl.BlockSpec((fc, tw), lambda s, w, p: (r0 + s * nf + out_chunk(p), col * nw + w))
        taps = pl.BlockSpec((fc, tw), lambda s, w, p: (fwd_chunk(p), order * nw + w))
        in_specs = [pl.BlockSpec((seq, tw), lambda s, w, p: (s, z_col * nw + w)), rows(xm_col)]
        in_specs += [rows(OFF_HGATE // HYENA_W, gate_row0 // fc)] if gated else []
        in_specs += [
            bias_spec,
            pl.BlockSpec((fc, seq), lambda s, w, p: (p % nf, 0)),
            pl.BlockSpec((fc, seq), lambda s, w, p: (fwd_chunk(p), 0)),
            pl.BlockSpec((fc, seq), lambda s, w, p: (out_chunk(p), 0)),
            taps, taps, nyq_spec,
        ]
        kern = functools.partial(_lconv_long_kernel, seq=seq, chunk=fc, gated=gated)
        grid, scratch = (n_seq, nw, 2 * nf), [pltpu.VMEM((seq, tw), BF16), pltpu.VMEM((seq, tw), BF16)]
        out_spec = pl.BlockSpec((fc, tw), lambda s, w, p: (s * nf + out_chunk(p), w))
    args = [z, xm] + ([gate] if gated else []) + [bias, *tables, *spectrum]
    return pl.pallas_call(
        kern,
        out_shape=out_shape,
        grid=grid,
        in_specs=in_specs,
        out_specs=out_spec,
        scratch_shapes=scratch,
        compiler_params=_params("arbitrary", "arbitrary", "arbitrary"),
        name=f"long_conv_{seq}_{order}",
    )(*args)


def _merge_kernel(ac_ref, al_ref, fc_ref, fl_ref, hc_ref, hl_ref, wa_ref, wf_ref, wh_ref,
                  ga_ref, gf_ref, gh_ref, o_ref, wab_ref, wfb_ref, whb_ref, *, n_ctx_tiles):
    i = pl.program_id(1)

    @pl.when(i == 0)
    def _():
        wab_ref[...] = wa_ref[...].astype(BF16)
        wfb_ref[...] = wf_ref[...].astype(BF16)
        whb_ref[...] = wh_ref[...].astype(BF16)

    def body(a_ref, f_ref, h_ref):
        m = jax.nn.sigmoid(ga_ref[...].astype(F32)) * _dot(a_ref[...], wab_ref[...])
        m = m + jax.nn.sigmoid(gf_ref[...].astype(F32)) * _dot(f_ref[...], wfb_ref[...])
        m = m + jax.nn.sigmoid(gh_ref[...].astype(F32)) * _dot(h_ref[...], whb_ref[...])
        o_ref[...] = m.astype(BF16)

    _by_group(i, n_ctx_tiles, body, (ac_ref, fc_ref, hc_ref), (al_ref, fl_ref, hl_ref))


def _merge(attn, fnet, hy, w_a, w_f, w_h, layer, proj, n_ctx, tm=512, tn=1024):
    t = proj.shape[0]
    nct = n_ctx // tm
    act = lambda w: _pair_specs((tm, w), nct, inner=True)
    wgt = lambda k: pl.BlockSpec((None, k, tn), lambda j, i: (layer, 0, j), pipeline_mode=pl.Buffered(1))
    gate = lambda off: pl.BlockSpec((tm, tn), lambda j, i: (i, off // tn + j))
    return pl.pallas_call(
        functools.partial(_merge_kernel, n_ctx_tiles=nct),
        out_shape=jax.ShapeDtypeStruct((t, D_MODEL), BF16),
        grid=(D_MODEL // tn, t // tm),
        in_specs=act(ATTN_W) + act(FNET_W) + act(HYENA_W) + [wgt(ATTN_W), wgt(FNET_W), wgt(HYENA_W),
                                                             gate(OFF_GA), gate(OFF_GF), gate(OFF_GH)],
        out_specs=pl.BlockSpec((tm, tn), lambda j, i: (i, j)),
        scratch_shapes=[pltpu.VMEM((ATTN_W, tn), BF16), pltpu.VMEM((FNET_W, tn), BF16),
                        pltpu.VMEM((HYENA_W, tn), BF16)],
        compiler_params=_params("arbitrary", "arbitrary"),
        name="merge",
    )(*attn, *fnet, *hy, w_a, w_f, w_h, proj, proj, proj)


def kernel(x_prompt, x_sample, cache_k, cache_v, c, c_ctx, w_mod, b_mod, g_pre, w_in, q_norm, k_norm, hy_short_w, hy_short_b, hy_ffn_w1, hy_ffn_b1, hy_ffn_w2, hy_ffn_b2, hy_ffn_w3, hy_ffn_b3, hy_sin_freq, hy_bias, w_attn_o, w_fnet_o, w_hy_o, w_out, g_post):
    nb, seq, _ = x_prompt.shape
    db, dseq, _ = x_sample.shape
    assert 1 + db <= MOD_ROWS
    n_ctx, n_lat = nb * seq, db * dseq

    cond = jnp.concatenate([c_ctx[None, :], c, jnp.zeros((MOD_ROWS - 1 - db, D_MODEL), F32)], axis=0)
    mod = _modulation(cond, w_mod, b_mod)

    w_kv_b = w_in[:, :, W_OFF_K:W_OFF_K + W_KV].astype(BF16)
    cos2, sin2 = _rope_tables(dseq)
    fnet_tables = {s: _fnet_tables(s) for s in (seq, dseq)}
    conv_tables = {s: _conv_tables(s) for s in (seq, dseq)}
    bias = hy_bias.reshape(DEPTH * HYENA_ORDER, 1, HYENA_W)
    conv_tiles = {seq: (HYENA_W, seq, 4), dseq: (HYENA_W, 512, 1)}
    proj_tm, proj_tn = 512, 1024
    skip_kv = lambda j: jnp.where(j >= W_OFF_K // proj_tn, j + W_KV // proj_tn, j)

    xs = (x_prompt.reshape(n_ctx, D_MODEL), 0, x_sample.reshape(n_lat, D_MODEL), 0)
    new_k, new_v = [], []
    mod3s = [mod[l].reshape(MOD_ROWS, 1, 3 * D_MODEL) for l in range(DEPTH)]
    h = _prenorm(*xs, g_pre[0], mod3s[0], n_ctx, n_lat, dseq)
    for l in range(DEPTH):
        mod3 = mod3s[l]
        proj = _matmul_wcast(h, w_in, l, BF16, proj_tm, proj_tn, "in_proj", n_out=PROJ_W, col_map=skip_kv)
        k, v = _kv_proj(h, w_kv_b, l, k_norm[l])
        new_k.append(k)
        new_v.append(v)

        attn = (_attention_ctx(proj, k, v, q_norm[l], nb, seq),
                _attention_lat(proj, k, v, cache_k, cache_v, l, q_norm[l], cos2, sin2, n_ctx, db, dseq))

        fnet, hy = [], []
        for row_start, n_seq, s in ((0, nb, seq), (n_ctx, db, dseq)):
            fnet.append(_fnet(proj, fnet_tables[s], row_start, n_seq, s))
            u = _short_conv(proj, hy_short_w[l], hy_short_b[l], row_start, n_seq * s, s)
            tap_sum, tap_dif, tap_nyq = _hyena_filters(s, hy_ffn_w1[l], hy_ffn_b1[l], hy_ffn_w2[l], hy_ffn_b2[l],
                                                       hy_ffn_w3[l], hy_ffn_b3[l], hy_sin_freq[l])
            cos, sin_f, _ = conv_tables[s]
            spec_tm = min(1024, s)
            spectrum = (_matmul(cos, tap_sum, F32, spec_tm, 1024, f"filter_spectrum_cos_{s}"),
                        _matmul(sin_f, tap_dif, F32, spec_tm, 1024, f"filter_spectrum_sin_{s}"), tap_nyq)
            tw, fc, blk = conv_tiles[s]
            z1 = _long_conv(u, 0, u, 1, None, 0, bias, l * HYENA_ORDER, 0, conv_tables[s],
                            spectrum, n_seq, s, tw, fc, blk)
            hy.append(_long_conv(z1, 0, u, 2, proj, row_start, bias, l * HYENA_ORDER + 1, 1, conv_tables[s],
                                 spectrum, n_seq, s, tw, fc, blk))

        merged = _merge(attn, fnet, hy, w_attn_o, w_fnet_o, w_hy_o, l, proj, n_ctx)
        r = _matmul_wcast(merged, w_out, l, BF16, proj_tm, proj_tn, "out_proj")
        if l < DEPTH - 1:
            x_new, h = _post(*xs, r, 0, g_post[l], mod3, n_ctx, n_lat, dseq, next_norm=(g_pre[l + 1], mod3s[l + 1]))
            xs = (x_new, 0, x_new, n_ctx)
        else:
            y_prompt = _post(*xs, r, 0, g_post[l], mod3, n_ctx, 0, dseq)
            y_sample = _post(*xs, r, n_ctx, g_post[l], mod3, 0, n_lat, dseq)

    cache_k_new, cache_v_new = _new_cache(new_k, new_v, nb, seq)
    return (y_prompt.reshape(nb, seq, D_MODEL), y_sample.reshape(db, dseq, D_MODEL), cache_k_new, cache_v_new)
```

```python
import functools
import math

import jax
import jax.numpy as jnp
from jax import lax
from jax.experimental import pallas as pl
from jax.experimental.pallas import tpu as pltpu

F32 = jnp.float32
BF16 = jnp.bfloat16

D_MODEL = 4096
DEPTH = 2
GRID_W = 64
HEAD_DIM = 128
N_HEADS = 16
N_KV_HEADS = 4
KV_GROUP = N_HEADS // N_KV_HEADS
ATTN_W = N_HEADS * HEAD_DIM
KV_W = N_KV_HEADS * HEAD_DIM
GROUP_W = KV_GROUP * HEAD_DIM
ATTN_SCALE = HEAD_DIM ** -0.5
Q_SCALE = ATTN_SCALE * math.log2(math.e)
ROPE_THETA = 10000.0
ROPE_PAIRS_AXIS = HEAD_DIM // 4
FNET_W = D_MODEL // 4
FNET_GROUPS = 4
FNET_GROUP_W = FNET_W // FNET_GROUPS
HYENA_W = D_MODEL // 4
HYENA_ORDER = 2
HYENA_SHORT = 3
HYENA_BANDS = 16
HYENA_POS_DIM = 1 + 2 * HYENA_BANDS
HYENA_POS_PAD = 128
HYENA_FFN_W = 64
HYENA_MIN_DECAY = math.log(1e-2) / 0.3
HYENA_MAX_DECAY = math.log(1e-2) / 1.5
EPS = 1e-6

W_OFF_K = ATTN_W
W_KV = 2 * KV_W
OFF_Q = 0
OFF_AGATE = OFF_Q + ATTN_W
OFF_FIN = OFF_AGATE + ATTN_W
OFF_FGATE = OFF_FIN + FNET_W
OFF_HV = OFF_FGATE + FNET_W
OFF_HX1 = OFF_HV + HYENA_W
OFF_HX2 = OFF_HX1 + HYENA_W
OFF_HGATE = OFF_HX2 + HYENA_W
OFF_GA = OFF_HGATE + HYENA_W
OFF_GF = OFF_GA + D_MODEL
OFF_GH = OFF_GF + D_MODEL
PROJ_W = OFF_GH + D_MODEL

MOD_ROWS = 8
ROW_TILE = 256
ATTN_ROWS = 256
TABLE_SPLIT = 64
VMEM_LIMIT = 56 * 1024 * 1024


def _params(*sem):
    return pltpu.CompilerParams(dimension_semantics=sem, vmem_limit_bytes=VMEM_LIMIT)


def _silu(x):
    return x * jax.nn.sigmoid(x)


def _dot(a, b):
    return jnp.dot(a, b, preferred_element_type=F32)


def _dot_exact(a, b):
    return jnp.dot(a, b, preferred_element_type=F32, precision=lax.Precision.HIGHEST)


def _pair_specs(block, n_ctx_tiles, ctx_tile0=0, lat_tile0=0, col=0, inner=False):
    def ctx(i):
        return (ctx_tile0 + jnp.clip(i, 0, max(n_ctx_tiles - 1, 0)), col)

    def lat(i):
        return (lat_tile0 + jnp.maximum(i - n_ctx_tiles, 0), col)

    if inner:
        return [pl.BlockSpec(block, lambda j, i: ctx(i)), pl.BlockSpec(block, lambda j, i: lat(i))]
    return [pl.BlockSpec(block, lambda i: ctx(i)), pl.BlockSpec(block, lambda i: lat(i))]


def _by_group(i, n_ctx_tiles, body, ctx_refs, lat_refs):
    @pl.when(i < n_ctx_tiles)
    def _():
        body(*ctx_refs)

    @pl.when(i >= n_ctx_tiles)
    def _():
        body(*lat_refs)


def _mod_kernel(c_ref, w_ref, b_ref, o_ref):
    s = _silu(c_ref[...]).astype(BF16)
    o_ref[0] = _dot(s, w_ref[0].astype(BF16)) + b_ref[0]


def _modulation(cond, w_mod, b_mod, tn=512):
    n = w_mod.shape[2]
    return pl.pallas_call(
        _mod_kernel,
        out_shape=jax.ShapeDtypeStruct((DEPTH, MOD_ROWS, n), F32),
        grid=(DEPTH, n // tn),
        in_specs=[
            pl.BlockSpec((MOD_ROWS, D_MODEL), lambda l, j: (0, 0)),
            pl.BlockSpec((1, D_MODEL, tn), lambda l, j: (l, 0, j)),
            pl.BlockSpec((1, 1, tn), lambda l, j: (l, 0, j)),
        ],
        out_specs=pl.BlockSpec((1, MOD_ROWS, tn), lambda l, j: (l, 0, j)),
        compiler_params=_params("arbitrary", "arbitrary"),
        name="modulation",
    )(cond, w_mod, b_mod.reshape(DEPTH, 1, n))


def _mod_spec(chunk, n_ctx_tiles, tiles_per_latent_seq):
    def row(i):
        return jnp.where(i < n_ctx_tiles, 0, 1 + (i - n_ctx_tiles) // tiles_per_latent_seq)
    return pl.BlockSpec((1, 1, D_MODEL), lambda i: (row(i), 0, chunk))


def _modulated_norm(x, g_ref, shift_ref, scale_ref):
    y = x * lax.rsqrt(jnp.mean(x * x, axis=-1, keepdims=True) + EPS) * g_ref[...]
    return (y * (1.0 + scale_ref[0]) + shift_ref[0]).astype(BF16)


def _prenorm_kernel(xc_ref, xl_ref, g_ref, shift_ref, scale_ref, o_ref, *, n_ctx_tiles):
    def body(x_ref):
        o_ref[...] = _modulated_norm(x_ref[...], g_ref, shift_ref, scale_ref)

    _by_group(pl.program_id(0), n_ctx_tiles, body, (xc_ref,), (xl_ref,))


def _prenorm(x_ctx, ctx_row0, x_lat, lat_row0, g, mod3, n_ctx, n_lat, lat_seq):
    tm = 2 * ROW_TILE
    nct = n_ctx // tm
    mod = functools.partial(_mod_spec, n_ctx_tiles=nct, tiles_per_latent_seq=lat_seq // tm)
    return pl.pallas_call(
        functools.partial(_prenorm_kernel, n_ctx_tiles=nct),
        out_shape=jax.ShapeDtypeStruct((n_ctx + n_lat, D_MODEL), BF16),
        grid=((n_ctx + n_lat) // tm,),
        in_specs=_pair_specs((tm, D_MODEL), nct, ctx_row0 // tm, lat_row0 // tm)
        + [pl.BlockSpec((1, D_MODEL), lambda i: (0, 0)), mod(0), mod(1)],
        out_specs=pl.BlockSpec((tm, D_MODEL), lambda i: (i, 0)),
        compiler_params=_params("arbitrary"),
        name="prenorm",
    )(x_ctx, x_lat, g.reshape(1, D_MODEL), mod3, mod3)


def _post_kernel(xc_ref, xl_ref, r_ref, g_ref, gate_ref, *rest, n_ctx_tiles, with_next):
    if with_next:
        g_next_ref, shift_ref, scale_ref, o_ref, h_ref = rest
    else:
        (o_ref,) = rest

    def body(x_ref):
        r = r_ref[...].astype(F32)
        y = r * lax.rsqrt(jnp.mean(r * r, axis=-1, keepdims=True) + EPS) * g_ref[...]
        x = x_ref[...] + gate_ref[0] * y
        o_ref[...] = x
        if with_next:
            h_ref[...] = _modulated_norm(x, g_next_ref, shift_ref, scale_ref)

    _by_group(pl.program_id(0), n_ctx_tiles, body, (xc_ref,), (xl_ref,))


def _post(x_ctx, ctx_row0, x_lat, lat_row0, r, r_row0, g, mod3, n_ctx, n_lat, lat_seq, next_norm=None):
    tm = ROW_TILE
    nct = n_ctx // tm
    r0 = r_row0 // tm
    rows = n_ctx + n_lat
    mod = functools.partial(_mod_spec, n_ctx_tiles=nct, tiles_per_latent_seq=lat_seq // tm)
    vec = pl.BlockSpec((1, D_MODEL), lambda i: (0, 0))
    tile = pl.BlockSpec((tm, D_MODEL), lambda i: (i, 0))
    in_specs = _pair_specs((tm, D_MODEL), nct, ctx_row0 // tm, lat_row0 // tm) + [
        pl.BlockSpec((tm, D_MODEL), lambda i: (r0 + i, 0)), vec, mod(2)]
    args = [x_ctx, x_lat, r, g.reshape(1, D_MODEL), mod3]
    out_shape, out_specs = jax.ShapeDtypeStruct((rows, D_MODEL), F32), tile
    if next_norm is not None:
        g_next, mod3_next = next_norm
        in_specs += [vec, mod(0), mod(1)]
        args += [g_next.reshape(1, D_MODEL), mod3_next, mod3_next]
        out_shape, out_specs = (out_shape, jax.ShapeDtypeStruct((rows, D_MODEL), BF16)), (tile, tile)
    return pl.pallas_call(
        functools.partial(_post_kernel, n_ctx_tiles=nct, with_next=next_norm is not None),
        out_shape=out_shape,
        grid=(rows // tm,),
        in_specs=in_specs,
        out_specs=out_specs,
        compiler_params=_params("arbitrary"),
        name="post",
    )(*args)


def _mm_kernel(a_ref, b_ref, o_ref):
    o_ref[...] = _dot(a_ref[...], b_ref[...]).astype(o_ref.dtype)


def _matmul(a, b, out_dtype, tm, tn, name):
    m, k = a.shape
    n = b.shape[1]
    return pl.pallas_call(
        _mm_kernel,
        out_shape=jax.ShapeDtypeStruct((m, n), out_dtype),
        grid=(m // tm, n // tn),
        in_specs=[pl.BlockSpec((tm, k), lambda i, j: (i, 0)), pl.BlockSpec((k, tn), lambda i, j: (0, j))],
        out_specs=pl.BlockSpec((tm, tn), lambda i, j: (i, j)),
        compiler_params=_params("arbitrary", "arbitrary"),
        name=name,
    )(a, b)


def _mm_wcast_kernel(a_ref, w_ref, o_ref, wb_ref):
    @pl.when(pl.program_id(1) == 0)
    def _():
        wb_ref[...] = w_ref[...].astype(BF16)

    o_ref[...] = _dot(a_ref[...], wb_ref[...]).astype(o_ref.dtype)


def _matmul_wcast(a, w, layer, out_dtype, tm, tn, name, n_out=None, col_map=None):
    m, k = a.shape
    n = n_out or w.shape[-1]
    col = col_map or (lambda j: j)
    return pl.pallas_call(
        _mm_wcast_kernel,
        out_shape=jax.ShapeDtypeStruct((m, n), out_dtype),
        grid=(n // tn, m // tm),
        in_specs=[
            pl.BlockSpec((tm, k), lambda j, i: (i, 0)),
            pl.BlockSpec((None, k, tn), lambda j, i: (layer, 0, col(j))),
        ],
        out_specs=pl.BlockSpec((tm, tn), lambda j, i: (i, j)),
        scratch_shapes=[pltpu.VMEM((k, tn), BF16)],
        compiler_params=_params("arbitrary", "arbitrary"),
        name=name,
    )(a, w)


def _kv_kernel(h_ref, w_ref, kn_ref, k_ref, v_ref):
    acc = _dot(h_ref[...], w_ref[...])
    for hd in range(N_KV_HEADS):
        sl = slice(hd * HEAD_DIM, (hd + 1) * HEAD_DIM)
        k_ref[:, sl] = _norm_head(acc[:, sl], kn_ref[...])
    v_ref[...] = acc[:, KV_W:]


def _kv_proj(h, w_kv, layer, k_norm, tm=1024):
    t = h.shape[0]
    return pl.pallas_call(
        _kv_kernel,
        out_shape=(jax.ShapeDtypeStruct((t, KV_W), F32), jax.ShapeDtypeStruct((t, KV_W), F32)),
        grid=(t // tm,),
        in_specs=[
            pl.BlockSpec((tm, D_MODEL), lambda i: (i, 0)),
            pl.BlockSpec((None, D_MODEL, W_KV), lambda i: (layer, 0, 0)),
            pl.BlockSpec((1, HEAD_DIM), lambda i: (0, 0)),
        ],
        out_specs=(pl.BlockSpec((tm, KV_W), lambda i: (i, 0)), pl.BlockSpec((tm, KV_W), lambda i: (i, 0))),
        compiler_params=_params("arbitrary"),
        name="kv_proj",
    )(h, w_kv, k_norm.reshape(1, HEAD_DIM))


def _cache_kernel(*refs):
    k_refs, v_refs, (ko_ref, vo_ref) = refs[:DEPTH], refs[DEPTH:2 * DEPTH], refs[2 * DEPTH:]
    for l in range(DEPTH):
        @pl.when(pl.program_id(0) == l)
        def _():
            for hd in range(N_KV_HEADS):
                sl = slice(hd * HEAD_DIM, (hd + 1) * HEAD_DIM)
                ko_ref[:, hd, :] = k_refs[l][:, sl]
                vo_ref[:, hd, :] = v_refs[l][:, sl]


def _new_cache(ks, vs, n_seq, seq):
    def layer_spec(l):
        return pl.BlockSpec((seq, KV_W), lambda cur, b: (jnp.where(cur == l, b, 0), 0))

    out_sds = jax.ShapeDtypeStruct((n_seq, DEPTH, seq, N_KV_HEADS, HEAD_DIM), F32)
    out_spec = pl.BlockSpec((None, None, seq, N_KV_HEADS, HEAD_DIM), lambda cur, b: (b, cur, 0, 0, 0))
    return pl.pallas_call(
        _cache_kernel,
        out_shape=(out_sds, out_sds),
        grid=(DEPTH, n_seq),
        in_specs=[layer_spec(l) for l in range(DEPTH)] * 2,
        out_specs=(out_spec, out_spec),
        compiler_params=_params("arbitrary", "arbitrary"),
        name="new_cache",
    )(*ks, *vs)


def _softmax_pv(s, v):
    m = jnp.max(s, axis=-1, keepdims=True)
    p = jnp.exp2(s - m)
    l = jnp.sum(p, axis=-1, keepdims=True)
    return _dot(p.astype(BF16), v) / l


def _qk(q, k):
    return lax.dot_general(q, k, (((1,), (1,)), ((), ())), preferred_element_type=F32)


def _norm_head(q, g):
    return q * lax.rsqrt(jnp.mean(q * q, axis=-1, keepdims=True) + EPS) * g


def _rope(x, cos2, sin2):
    return x * cos2 + pltpu.roll(x, HEAD_DIM // 2, 1) * sin2


def _attn_ctx_kernel(q_ref, k_ref, v_ref, ag_ref, qn_ref, o_ref):
    seq = q_ref.shape[0]
    for hd in range(N_KV_HEADS):
        kv = slice(hd * HEAD_DIM, (hd + 1) * HEAD_DIM)
        cols = [slice((hd * KV_GROUP + g) * HEAD_DIM, (hd * KV_GROUP + g + 1) * HEAD_DIM) for g in range(KV_GROUP)]
        q = jnp.concatenate([(_norm_head(q_ref[:, c].astype(F32), qn_ref[...]) * Q_SCALE).astype(BF16)
                             for c in cols], axis=0)
        o = _softmax_pv(_qk(q, k_ref[:, kv].astype(BF16)), v_ref[:, kv].astype(BF16))
        for g, c in enumerate(cols):
            o_ref[:, c] = (o[g * seq:(g + 1) * seq] * _silu(ag_ref[:, c].astype(F32))).astype(BF16)


def _attention_ctx(proj, k, v, q_norm, n_seq, seq):
    return pl.pallas_call(
        _attn_ctx_kernel,
        out_shape=jax.ShapeDtypeStruct((n_seq * seq, ATTN_W), BF16),
        grid=(n_seq,),
        in_specs=[
            pl.BlockSpec((seq, ATTN_W), lambda b: (b, OFF_Q // ATTN_W)),
            pl.BlockSpec((seq, KV_W), lambda b: (b, 0)),
            pl.BlockSpec((seq, KV_W), lambda b: (b, 0)),
            pl.BlockSpec((seq, ATTN_W), lambda b: (b, OFF_AGATE // ATTN_W)),
            pl.BlockSpec((1, HEAD_DIM), lambda b: (0, 0)),
        ],
        out_specs=pl.BlockSpec((seq, ATTN_W), lambda b: (b, 0)),
        compiler_params=_params("arbitrary"),
        name="attn_ctx",
    )(proj, k, v, proj, q_norm.reshape(1, HEAD_DIM))


def _attn_lat_kernel(q_ref, k_ref, v_ref, ck_ref, cv_ref, ag_ref, qn_ref, cosq_ref, sinq_ref,
                     cosk_ref, sink_ref, o_ref, k_s, v_s, *, seq):
    @pl.when(pl.program_id(2) == 0)
    def _():
        k_s[:seq] = _rope(k_ref[...], cosk_ref[...], sink_ref[...]).astype(BF16)
        k_s[seq:] = ck_ref[...].astype(BF16)
        v_s[:seq] = v_ref[...].astype(BF16)
        v_s[seq:] = cv_ref[...].astype(BF16)

    k = k_s[...]
    v = v_s[...]
    for g in range(KV_GROUP):
        sl = slice(g * HEAD_DIM, (g + 1) * HEAD_DIM)
        for r0 in range(0, q_ref.shape[0], ATTN_ROWS):
            rows = slice(r0, r0 + ATTN_ROWS)
            q = _norm_head(q_ref[rows, sl].astype(F32), qn_ref[...])
            q = _rope(q, cosq_ref[rows, :], sinq_ref[rows, :]) * Q_SCALE
            o = _softmax_pv(_qk(q.astype(BF16), k), v)
            o_ref[rows, sl] = (o * _silu(ag_ref[rows, sl].astype(F32))).astype(BF16)


def _attention_lat(proj, k, v, cache_k, cache_v, layer, q_norm, cos2, sin2, n_ctx, n_seq, seq, tq=1024):
    past = cache_k.shape[2]
    ck = cache_k.reshape(cache_k.shape[0], DEPTH, past, KV_W)
    cv = cache_v.reshape(cache_v.shape[0], DEPTH, past, KV_W)
    nq = seq // tq
    row0, seq0 = n_ctx // tq, n_ctx // seq
    return pl.pallas_call(
        functools.partial(_attn_lat_kernel, seq=seq),
        out_shape=jax.ShapeDtypeStruct((n_seq * seq, ATTN_W), BF16),
        grid=(n_seq, N_KV_HEADS, nq),
        in_specs=[
            pl.BlockSpec((tq, GROUP_W), lambda b, h, i: (row0 + b * nq + i, OFF_Q // GROUP_W + h)),
            pl.BlockSpec((seq, HEAD_DIM), lambda b, h, i: (seq0 + b, h)),
            pl.BlockSpec((seq, HEAD_DIM), lambda b, h, i: (seq0 + b, h)),
            pl.BlockSpec((None, None, past, HEAD_DIM), lambda b, h, i: (b, layer, 0, h)),
            pl.BlockSpec((None, None, past, HEAD_DIM), lambda b, h, i: (b, layer, 0, h)),
            pl.BlockSpec((tq, GROUP_W), lambda b, h, i: (row0 + b * nq + i, OFF_AGATE // GROUP_W + h)),
            pl.BlockSpec((1, HEAD_DIM), lambda b, h, i: (0, 0)),
            pl.BlockSpec((tq, HEAD_DIM), lambda b, h, i: (i, 0)),
            pl.BlockSpec((tq, HEAD_DIM), lambda b, h, i: (i, 0)),
            pl.BlockSpec((seq, HEAD_DIM), lambda b, h, i: (0, 0)),
            pl.BlockSpec((seq, HEAD_DIM), lambda b, h, i: (0, 0)),
        ],
        out_specs=pl.BlockSpec((tq, GROUP_W), lambda b, h, i: (b * nq + i, h)),
        scratch_shapes=[pltpu.VMEM((seq + past, HEAD_DIM), BF16), pltpu.VMEM((seq + past, HEAD_DIM), BF16)],
        compiler_params=_params("arbitrary", "arbitrary", "arbitrary"),
        name="attn_lat",
    )(proj, k, v, ck, cv, proj, q_norm.reshape(1, HEAD_DIM), cos2, sin2, cos2, sin2)


def _rope_tables(seq):
    rows = seq // GRID_W
    row = jnp.repeat(jnp.arange(rows, dtype=F32), GRID_W)
    col = jnp.tile(jnp.arange(GRID_W, dtype=F32), rows)
    inv = ROPE_THETA ** (-jnp.arange(ROPE_PAIRS_AXIS, dtype=F32) / ROPE_PAIRS_AXIS)
    ang = jnp.concatenate([row[:, None] * inv, col[:, None] * inv], axis=-1)
    cos, sin = jnp.cos(ang), jnp.sin(ang)
    return jnp.concatenate([cos, cos], axis=-1), jnp.concatenate([-sin, sin], axis=-1)


def _dft_angle(i, j, n):
    return (2.0 * math.pi / n) * ((i[:, None] * j[None, :]) % n).astype(F32)


def _cos_sin_table(rows, cols, n):
    c = jnp.arange(cols, dtype=jnp.int32)
    hi = _dft_angle(TABLE_SPLIT * jnp.arange(rows // TABLE_SPLIT, dtype=jnp.int32), c, n)
    lo = _dft_angle(jnp.arange(TABLE_SPLIT, dtype=jnp.int32), c, n)
    ch, sh = jnp.cos(hi)[:, None, :], jnp.sin(hi)[:, None, :]
    cl, sl = jnp.cos(lo)[None, :, :], jnp.sin(lo)[None, :, :]
    return (ch * cl - sh * sl).reshape(rows, cols), (sh * cl + ch * sl).reshape(rows, cols)


def _fnet_kernel(u_ref, fg_ref, csc_ref, cos_ref, sin_ref, o_ref, tc_s, ts_s, *, scale):
    @pl.when(pl.program_id(1) == 0)
    def _():
        for g in range(FNET_GROUPS):
            sl = slice(g * FNET_GROUP_W, (g + 1) * FNET_GROUP_W)
            t = _dot(u_ref[:, sl], csc_ref[...])
            tc_s[:, sl] = t[:, :FNET_GROUP_W].astype(BF16)
            ts_s[:, sl] = t[:, FNET_GROUP_W:].astype(BF16)

    y = (_dot(cos_ref[...], tc_s[...]) - _dot(sin_ref[...], ts_s[...])) * scale
    o_ref[...] = (y * _silu(fg_ref[...].astype(F32))).astype(BF16)


def _fnet_tables(seq):
    j = jnp.arange(FNET_GROUP_W, dtype=jnp.int32)
    ang = _dft_angle(j, j, FNET_GROUP_W)
    csc = jnp.concatenate([jnp.cos(ang), jnp.sin(ang)], axis=1).astype(BF16)
    cos, sin = _cos_sin_table(seq, seq, seq)
    return csc, cos.astype(BF16), sin.astype(BF16)


def _fnet(proj, tables, row_start, n_seq, seq, tr=1024):
    csc, cos, sin = tables
    tr = min(tr, seq)
    nr = seq // tr
    s0, r0 = row_start // seq, row_start // tr
    return pl.pallas_call(
        functools.partial(_fnet_kernel, scale=1.0 / math.sqrt(seq * FNET_GROUP_W)),
        out_shape=jax.ShapeDtypeStruct((n_seq * seq, FNET_W), BF16),
        grid=(n_seq, nr),
        in_specs=[
            pl.BlockSpec((seq, FNET_W), lambda b, r: (s0 + b, OFF_FIN // FNET_W)),
            pl.BlockSpec((tr, FNET_W), lambda b, r: (r0 + b * nr + r, OFF_FGATE // FNET_W)),
            pl.BlockSpec((FNET_GROUP_W, 2 * FNET_GROUP_W), lambda b, r: (0, 0)),
            pl.BlockSpec((tr, seq), lambda b, r: (r, 0)),
            pl.BlockSpec((tr, seq), lambda b, r: (r, 0)),
        ],
        out_specs=pl.BlockSpec((tr, FNET_W), lambda b, r: (b * nr + r, 0)),
        scratch_shapes=[pltpu.VMEM((seq, FNET_W), BF16), pltpu.VMEM((seq, FNET_W), BF16)],
        compiler_params=_params("arbitrary", "arbitrary"),
        name=f"fnet_{seq}",
    )(proj, proj, csc, cos, sin)


def _short_kernel(x_ref, w_ref, b_ref, o_ref, *, seq):
    x = x_ref[...].astype(F32)
    n = x.shape[0]
    pos = lax.broadcasted_iota(jnp.int32, x.shape, 0) % seq
    prev = jnp.where(pos == 0, 0.0, pltpu.roll(x, 1, 0))
    nxt = jnp.where(pos == seq - 1, 0.0, pltpu.roll(x, n - 1, 0))
    y = b_ref[...] + prev * w_ref[0:1, :] + x * w_ref[1:2, :] + nxt * w_ref[2:3, :]
    o_ref[...] = y.astype(BF16)


def _short_conv(proj, w, b, row_start, n_rows, seq, tr=2048, tw=512):
    width = 3 * HYENA_W
    r0 = row_start // tr
    return pl.pallas_call(
        functools.partial(_short_kernel, seq=seq),
        out_shape=jax.ShapeDtypeStruct((n_rows, width), BF16),
        grid=(n_rows // tr, width // tw),
        in_specs=[
            pl.BlockSpec((tr, tw), lambda s, c: (r0 + s, OFF_HV // tw + c)),
            pl.BlockSpec((HYENA_SHORT, tw), lambda s, c: (0, c)),
            pl.BlockSpec((1, tw), lambda s, c: (0, c)),
        ],
        out_specs=pl.BlockSpec((tr, tw), lambda s, c: (s, c)),
        compiler_params=_params("arbitrary", "arbitrary"),
        name=f"short_conv_{seq}",
    )(proj, w, b.reshape(1, width))


def _filt_mlp_kernel(feats_ref, w1_ref, b1_ref, w2_ref, b2_ref, fr_ref, o_ref):
    fr = fr_ref[...]
    h = jnp.sin(fr * (_dot_exact(feats_ref[...], w1_ref[...]) + b1_ref[...]))
    o_ref[...] = jnp.sin(fr * (_dot_exact(h, w2_ref[...]) + b2_ref[...]))


def _filt_kernel(h_ref, t_ref, w3f_ref, w3b_ref, b3f_ref, b3b_ref, dl_ref, sum_ref, dif_ref, nyq_ref):
    h = h_ref[...]
    decay = jnp.exp(-t_ref[...] * dl_ref[...])
    hf = (_dot_exact(h, w3f_ref[...]) + b3f_ref[...]) * decay
    hb = (_dot_exact(h, w3b_ref[...]) + b3b_ref[...]) * decay
    den = jnp.sum(jnp.abs(hf), axis=0, keepdims=True) + jnp.sum(jnp.abs(hb), axis=0, keepdims=True) + EPS
    hs = (hf + hb) / den
    sum_ref[...] = hs.astype(BF16)
    dif_ref[...] = ((hf - hb) / den).astype(BF16)
    sign = (1 - 2 * (lax.broadcasted_iota(jnp.int32, hs.shape, 0) % 2)).astype(F32)
    nyq_ref[...] = jnp.sum(hs * sign, axis=0, keepdims=True)


def _hyena_filters(seq, w1, b1, w2, b2, w3, b3, freq, tw=512):
    t = jnp.arange(seq, dtype=F32)[:, None] / seq
    bands = jnp.arange(1, HYENA_BANDS + 1, dtype=F32)[None, :]
    feats = jnp.concatenate([t, jnp.cos(2 * math.pi * t * bands), jnp.sin(2 * math.pi * t * bands),
                             jnp.zeros((seq, HYENA_POS_PAD - HYENA_POS_DIM), F32)], axis=-1)
    w1p = jnp.concatenate([w1, jnp.zeros((HYENA_POS_PAD - HYENA_POS_DIM, HYENA_FFN_W), F32)], axis=0)
    deltas = jnp.abs(jnp.linspace(HYENA_MIN_DECAY, HYENA_MAX_DECAY, HYENA_W, dtype=F32))[None, :]
    hidden = pl.pallas_call(
        _filt_mlp_kernel,
        out_shape=jax.ShapeDtypeStruct((seq, HYENA_FFN_W), F32),
        name=f"hyena_filter_mlp_{seq}",
    )(feats, w1p, b1.reshape(1, -1), w2, b2.reshape(1, -1), freq.reshape(1, -1))
    nw = HYENA_W // tw
    small = lambda shape: pl.BlockSpec(shape, lambda o, c: (0, 0))
    out_sds = jax.ShapeDtypeStruct((seq, HYENA_ORDER * HYENA_W), BF16)
    fwd = lambda o, c: (0, (2 * o) * nw + c)
    bwd = lambda o, c: (0, (2 * o + 1) * nw + c)
    b3r = b3.reshape(1, -1)
    return pl.pallas_call(
        _filt_kernel,
        out_shape=(out_sds, out_sds, jax.ShapeDtypeStruct((1, HYENA_ORDER * HYENA_W), F32)),
        grid=(HYENA_ORDER, nw),
        in_specs=[
            small((seq, HYENA_FFN_W)), small((seq, 1)),
            pl.BlockSpec((HYENA_FFN_W, tw), fwd), pl.BlockSpec((HYENA_FFN_W, tw), bwd),
            pl.BlockSpec((1, tw), fwd), pl.BlockSpec((1, tw), bwd),
            pl.BlockSpec((1, tw), lambda o, c: (0, c)),
        ],
        out_specs=(pl.BlockSpec((seq, tw), lambda o, c: (0, o * nw + c)),
                   pl.BlockSpec((seq, tw), lambda o, c: (0, o * nw + c)),
                   pl.BlockSpec((1, tw), lambda o, c: (0, o * nw + c))),
        compiler_params=_params("arbitrary", "arbitrary"),
        name=f"hyena_filters_{seq}",
    )(hidden, t, w3, w3, b3r, b3r, deltas)


def _conv_tables(seq):
    i = jnp.arange(seq, dtype=jnp.int32)
    nyq = (1 - 2 * (i % 2)).astype(F32)
    cos, sin = _cos_sin_table(seq, seq, 2 * seq)
    sin_f = jnp.where(i[:, None] == 0, nyq[None, :], sin)
    sin_t = jnp.where(i[None, :] == 0, nyq[:, None], sin)
    return cos.astype(BF16), sin_f.astype(BF16), sin_t.astype(BF16)


def _filter_bins(uc_ref, us_ref, nyq_ref, first_chunk):
    gr = uc_ref[...]
    packed = jnp.logical_and(lax.broadcasted_iota(jnp.int32, gr.shape, 0) == 0, first_chunk)
    gi = jnp.where(packed, 0.0, us_ref[...])
    gn = jnp.where(packed, nyq_ref[...], gr)
    wgt = jnp.where(packed, 1.0, 2.0)
    return gr * wgt, gi * wgt, gn * wgt


def _spectrum_product(cos_ref, sin_ref, z, bins):
    gr, gi, gn = bins
    zr = _dot(cos_ref[...], z)
    zi = _dot(sin_ref[...], z)
    return (zr * gr - zi * gi).astype(BF16), (zr * gi + zi * gn).astype(BF16)


def _conv_epilogue(y, z, bias_ref, xm, gate, n):
    y = (y * (1.0 / n) + z.astype(F32) * bias_ref[...]) * xm.astype(F32)
    if gate is not None:
        y = y * _silu(gate.astype(F32))
    return y.astype(BF16)


def _lconv_short_kernel(*refs, seq, n_seq_blk, gated):
    refs = list(refs)
    z_ref, xm_ref = refs[:2]
    gate_ref = refs[2] if gated else None
    bias_ref, cos_ref, sin_ref, sin_t_ref, uc_ref, us_ref, nyq_ref, o_ref = refs[2 + gated:]
    bins = _filter_bins(uc_ref, us_ref, nyq_ref, True)
    for s in range(n_seq_blk):
        rows = slice(s * seq, (s + 1) * seq)
        z = z_ref[rows, :]
        pr, pi = _spectrum_product(cos_ref, sin_ref, z, bins)
        y = _dot(cos_ref[...], pr) + _dot(sin_t_ref[...], pi)
        o_ref[rows, :] = _conv_epilogue(y, z, bias_ref, xm_ref[rows, :],
                                        gate_ref[rows, :] if gated else None, 2 * seq)


def _lconv_long_kernel(*refs, seq, chunk, gated):
    refs = list(refs)
    z_ref, xm_ref = refs[:2]
    gate_ref = refs[2] if gated else None
    bias_ref, cos_ref, sin_ref, sin_t_ref, uc_ref, us_ref, nyq_ref, o_ref, pr_s, pi_s = refs[2 + gated:]
    n_chunks = seq // chunk
    p = pl.program_id(2)

    @pl.when(p < n_chunks)
    def _():
        bins = _filter_bins(uc_ref, us_ref, nyq_ref, p == 0)
        pr, pi = _spectrum_product(cos_ref, sin_ref, z_ref[...], bins)
        rows = pl.ds(pl.multiple_of(p * chunk, chunk), chunk)
        pr_s[rows, :] = pr
        pi_s[rows, :] = pi

    @pl.when(p >= n_chunks)
    def _():
        y = _dot(cos_ref[...], pr_s[...]) + _dot(sin_t_ref[...], pi_s[...])
        rows = pl.ds(pl.multiple_of((p - n_chunks) * chunk, chunk), chunk)
        o_ref[...] = _conv_epilogue(y, z_ref[rows, :], bias_ref, xm_ref[...],
                                    gate_ref[...] if gated else None, 2 * seq)


def _long_conv(z, z_col, xm, xm_col, gate, gate_row0, bias, idx, order, tables, spectrum,
               n_seq, seq, tw, fc, seq_blk):
    nw, nf = HYENA_W // tw, seq // fc
    gated = gate is not None
    out_shape = jax.ShapeDtypeStruct((n_seq * seq, HYENA_W), BF16)
    bias_spec = pl.BlockSpec((None, 1, tw), lambda s, w, p: (idx, 0, w))
    nyq_spec = pl.BlockSpec((1, tw), lambda s, w, p: (0, order * nw + w))
    if nf == 1:
        tr = seq_blk * seq
        tok = lambda col, r0=0: pl.BlockSpec((tr, tw), lambda s, w, p: (r0 + s, col * nw + w))
        taps = pl.BlockSpec((seq, tw), lambda s, w, p: (0, order * nw + w))
        table = pl.BlockSpec((seq, seq), lambda s, w, p: (0, 0))
        in_specs = [tok(z_col), tok(xm_col)] + ([tok(OFF_HGATE // HYENA_W, gate_row0 // tr)] if gated else [])
        in_specs += [bias_spec, table, table, table, taps, taps, nyq_spec]
        kern = functools.partial(_lconv_short_kernel, seq=seq, n_seq_blk=seq_blk, gated=gated)
        grid, scratch = (n_seq // seq_blk, nw, 1), []
        out_spec = pl.BlockSpec((tr, tw), lambda s, w, p: (s, w))
    else:
        assert seq_blk == 1
        fwd_chunk = lambda p: jnp.minimum(p, nf - 1)
        out_chunk = lambda p: jnp.maximum(p - nf, 0)
        rows = lambda col, r0=0: pl.BlockSpec((fc, tw), lambda s, w, p: (r0 + s * nf + out_chunk(p), col * nw + w))
        taps = pl.BlockSpec((fc, tw), lambda s, w, p: (fwd_chunk(p), order * nw + w))
        in_specs = [pl.BlockSpec((seq, tw), lambda s, w, p: (s, z_col * nw + w)), rows(xm_col)]
        in_specs += [rows(OFF_HGATE // HYENA_W, gate_row0 // fc)] if gated else []
        in_specs += [
            bias_spec,
            pl.BlockSpec((fc, seq), lambda s, w, p: (p % nf, 0)),
            pl.BlockSpec((fc, seq), lambda s, w, p: (fwd_chunk(p), 0)),
            pl.BlockSpec((fc, seq), lambda s, w, p: (out_chunk(p), 0)),
            taps, taps, nyq_spec,
        ]
        kern = functools.partial(_lconv_long_kernel, seq=seq, chunk=fc, gated=gated)
        grid, scratch = (n_seq, nw, 2 * nf), [pltpu.VMEM((seq, tw), BF16), pltpu.VMEM((seq, tw), BF16)]
        out_spec = pl.BlockSpec((fc, tw), lambda s, w, p: (s * nf + out_chunk(p), w))
    args = [z, xm] + ([gate] if gated else []) + [bias, *tables, *spectrum]
    return pl.pallas_call(
        kern,
        out_shape=out_shape,
        grid=grid,
        in_specs=in_specs,
        out_specs=out_spec,
        scratch_shapes=scratch,
        compiler_params=_params("arbitrary", "arbitrary", "arbitrary"),
        name=f"long_conv_{seq}_{order}",
    )(*args)


def _merge_kernel(ac_ref, al_ref, fc_ref, fl_ref, hc_ref, hl_ref, wa_ref, wf_ref, wh_ref,
                  ga_ref, gf_ref, gh_ref, o_ref, wab_ref, wfb_ref, whb_ref, *, n_ctx_tiles):
    i = pl.program_id(1)

    @pl.when(i == 0)
    def _():
        wab_ref[...] = wa_ref[...].astype(BF16)
        wfb_ref[...] = wf_ref[...].astype(BF16)
        whb_ref[...] = wh_ref[...].astype(BF16)

    def body(a_ref, f_ref, h_ref):
        m = jax.nn.sigmoid(ga_ref[...].astype(F32)) * _dot(a_ref[...], wab_ref[...])
        m = m + jax.nn.sigmoid(gf_ref[...].astype(F32)) * _dot(f_ref[...], wfb_ref[...])
        m = m + jax.nn.sigmoid(gh_ref[...].astype(F32)) * _dot(h_ref[...], whb_ref[...])
        o_ref[...] = m.astype(BF16)

    _by_group(i, n_ctx_tiles, body, (ac_ref, fc_ref, hc_ref), (al_ref, fl_ref, hl_ref))


def _merge(attn, fnet, hy, w_a, w_f, w_h, layer, proj, n_ctx, tm=512, tn=1024):
    t = proj.shape[0]
    nct = n_ctx // tm
    act = lambda w: _pair_specs((tm, w), nct, inner=True)
    wgt = lambda k: pl.BlockSpec((None, k, tn), lambda j, i: (layer, 0, j), pipeline_mode=pl.Buffered(1))
    gate = lambda off: pl.BlockSpec((tm, tn), lambda j, i: (i, off // tn + j))
    return pl.pallas_call(
        functools.partial(_merge_kernel, n_ctx_tiles=nct),
        out_shape=jax.ShapeDtypeStruct((t, D_MODEL), BF16),
        grid=(D_MODEL // tn, t // tm),
        in_specs=act(ATTN_W) + act(FNET_W) + act(HYENA_W) + [wgt(ATTN_W), wgt(FNET_W), wgt(HYENA_W),
                                                             gate(OFF_GA), gate(OFF_GF), gate(OFF_GH)],
        out_specs=pl.BlockSpec((tm, tn), lambda j, i: (i, j)),
        scratch_shapes=[pltpu.VMEM((ATTN_W, tn), BF16), pltpu.VMEM((FNET_W, tn), BF16),
                        pltpu.VMEM((HYENA_W, tn), BF16)],
        compiler_params=_params("arbitrary", "arbitrary"),
        name="merge",
    )(*attn, *fnet, *hy, w_a, w_f, w_h, proj, proj, proj)


def kernel(x_prompt, x_sample, cache_k, cache_v, c, c_ctx, w_mod, b_mod, g_pre, w_in, q_norm, k_norm, hy_short_w, hy_short_b, hy_ffn_w1, hy_ffn_b1, hy_ffn_w2, hy_ffn_b2, hy_ffn_w3, hy_ffn_b3, hy_sin_freq, hy_bias, w_attn_o, w_fnet_o, w_hy_o, w_out, g_post):
    nb, seq, _ = x_prompt.shape
    db, dseq, _ = x_sample.shape
    assert 1 + db <= MOD_ROWS
    n_ctx, n_lat = nb * seq, db * dseq

    cond = jnp.concatenate([c_ctx[None, :], c, jnp.zeros((MOD_ROWS - 1 - db, D_MODEL), F32)], axis=0)
    mod = _modulation(cond, w_mod, b_mod)

    w_kv_b = w_in[:, :, W_OFF_K:W_OFF_K + W_KV].astype(BF16)
    cos2, sin2 = _rope_tables(dseq)
    fnet_tables = {s: _fnet_tables(s) for s in (seq, dseq)}
    conv_tables = {s: _conv_tables(s) for s in (seq, dseq)}
    bias = hy_bias.reshape(DEPTH * HYENA_ORDER, 1, HYENA_W)
    conv_tiles = {seq: (HYENA_W, seq, 8), dseq: (HYENA_W, 512, 1)}
    proj_tm, proj_tn = 512, 1024
    skip_kv = lambda j: jnp.where(j >= W_OFF_K // proj_tn, j + W_KV // proj_tn, j)

    xs = (x_prompt.reshape(n_ctx, D_MODEL), 0, x_sample.reshape(n_lat, D_MODEL), 0)
    new_k, new_v = [], []
    mod3s = [mod[l].reshape(MOD_ROWS, 1, 3 * D_MODEL) for l in range(DEPTH)]
    h = _prenorm(*xs, g_pre[0], mod3s[0], n_ctx, n_lat, dseq)
    for l in range(DEPTH):
        mod3 = mod3s[l]
        proj = _matmul_wcast(h, w_in, l, BF16, proj_tm, proj_tn, "in_proj", n_out=PROJ_W, col_map=skip_kv)
        k, v = _kv_proj(h, w_kv_b, l, k_norm[l])
        new_k.append(k)
        new_v.append(v)

        attn = (_attention_ctx(proj, k, v, q_norm[l], nb, seq),
                _attention_lat(proj, k, v, cache_k, cache_v, l, q_norm[l], cos2, sin2, n_ctx, db, dseq))

        fnet, hy = [], []
        for row_start, n_seq, s in ((0, nb, seq), (n_ctx, db, dseq)):
            fnet.append(_fnet(proj, fnet_tables[s], row_start, n_seq, s))
            u = _short_conv(proj, hy_short_w[l], hy_short_b[l], row_start, n_seq * s, s)
            tap_sum, tap_dif, tap_nyq = _hyena_filters(s, hy_ffn_w1[l], hy_ffn_b1[l], hy_ffn_w2[l], hy_ffn_b2[l],
                                                       hy_ffn_w3[l], hy_ffn_b3[l], hy_sin_freq[l])
            cos, sin_f, _ = conv_tables[s]
            spec_tm = min(1024, s)
            spectrum = (_matmul(cos, tap_sum, F32, spec_tm, 1024, f"filter_spectrum_cos_{s}"),
                        _matmul(sin_f, tap_dif, F32, spec_tm, 1024, f"filter_spectrum_sin_{s}"), tap_nyq)
            tw, fc, blk = conv_tiles[s]
            z1 = _long_conv(u, 0, u, 1, None, 0, bias, l * HYENA_ORDER, 0, conv_tables[s],
                            spectrum, n_seq, s, tw, fc, blk)
            hy.append(_long_conv(z1, 0, u, 2, proj, row_start, bias, l * HYENA_ORDER + 1, 1, conv_tables[s],
                                 spectrum, n_seq, s, tw, fc, blk))

        merged = _merge(attn, fnet, hy, w_attn_o, w_fnet_o, w_hy_o, l, proj, n_ctx)
        r = _matmul_wcast(merged, w_out, l, BF16, proj_tm, proj_tn, "out_proj")
        if l < DEPTH - 1:
            x_new, h = _post(*xs, r, 0, g_post[l], mod3, n_ctx, n_lat, dseq, next_norm=(g_pre[l + 1], mod3s[l + 1]))
            xs = (x_new, 0, x_new, n_ctx)
        else:
            y_prompt = _post(*xs, r, 0, g_post[l], mod3, n_ctx, 0, dseq)
            y_sample = _post(*xs, r, n_ctx, g_post[l], mod3, 0, n_lat, dseq)

    cache_k_new, cache_v_new = _new_cache(new_k, new_v, nb, seq)
    return (y_prompt.reshape(nb, seq, D_MODEL), y_sample.reshape(db, dseq, D_MODEL), cache_k_new, cache_v_new)
```

```python
import functools
import math

import jax
import jax.numpy as jnp
from jax import lax
from jax.experimental import pallas as pl
from jax.experimental.pallas import tpu as pltpu

F32 = jnp.float32
BF16 = jnp.bfloat16

D_MODEL = 4096
DEPTH = 2
GRID_W = 64
HEAD_DIM = 128
N_HEADS = 16
N_KV_HEADS = 4
KV_GROUP = N_HEADS // N_KV_HEADS
ATTN_W = N_HEADS * HEAD_DIM
KV_W = N_KV_HEADS * HEAD_DIM
GROUP_W = KV_GROUP * HEAD_DIM
ATTN_SCALE = HEAD_DIM ** -0.5
Q_SCALE = ATTN_SCALE * math.log2(math.e)
ROPE_THETA = 10000.0
ROPE_PAIRS_AXIS = HEAD_DIM // 4
FNET_W = D_MODEL // 4
FNET_GROUPS = 4
FNET_GROUP_W = FNET_W // FNET_GROUPS
HYENA_W = D_MODEL // 4
HYENA_ORDER = 2
HYENA_SHORT = 3
HYENA_BANDS = 16
HYENA_POS_DIM = 1 + 2 * HYENA_BANDS
HYENA_POS_PAD = 128
HYENA_FFN_W = 64
HYENA_MIN_DECAY = math.log(1e-2) / 0.3
HYENA_MAX_DECAY = math.log(1e-2) / 1.5
EPS = 1e-6

W_OFF_K = ATTN_W
W_KV = 2 * KV_W
OFF_Q = 0
OFF_AGATE = OFF_Q + ATTN_W
OFF_FIN = OFF_AGATE + ATTN_W
OFF_FGATE = OFF_FIN + FNET_W
OFF_HV = OFF_FGATE + FNET_W
OFF_HX1 = OFF_HV + HYENA_W
OFF_HX2 = OFF_HX1 + HYENA_W
OFF_HGATE = OFF_HX2 + HYENA_W
OFF_GA = OFF_HGATE + HYENA_W
OFF_GF = OFF_GA + D_MODEL
OFF_GH = OFF_GF + D_MODEL
PROJ_W = OFF_GH + D_MODEL

MOD_ROWS = 8
ROW_TILE = 256
ATTN_ROWS = 256
TABLE_SPLIT = 64
VMEM_LIMIT = 56 * 1024 * 1024


def _params(*sem):
    return pltpu.CompilerParams(dimension_semantics=sem, vmem_limit_bytes=VMEM_LIMIT)


def _silu(x):
    return x * jax.nn.sigmoid(x)


def _dot(a, b):
    return jnp.dot(a, b, preferred_element_type=F32)


def _dot_exact(a, b):
    return jnp.dot(a, b, preferred_element_type=F32, precision=lax.Precision.HIGHEST)


def _pair_specs(block, n_ctx_tiles, ctx_tile0=0, lat_tile0=0, col=0, inner=False):
    def ctx(i):
        return (ctx_tile0 + jnp.clip(i, 0, max(n_ctx_tiles - 1, 0)), col)

    def lat(i):
        return (lat_tile0 + jnp.maximum(i - n_ctx_tiles, 0), col)

    if inner:
        return [pl.BlockSpec(block, lambda j, i: ctx(i)), pl.BlockSpec(block, lambda j, i: lat(i))]
    return [pl.BlockSpec(block, lambda i: ctx(i)), pl.BlockSpec(block, lambda i: lat(i))]


def _by_group(i, n_ctx_tiles, body, ctx_refs, lat_refs):
    @pl.when(i < n_ctx_tiles)
    def _():
        body(*ctx_refs)

    @pl.when(i >= n_ctx_tiles)
    def _():
        body(*lat_refs)


def _mod_kernel(c_ref, w_ref, b_ref, o_ref):
    s = _silu(c_ref[...]).astype(BF16)
    o_ref[0] = _dot(s, w_ref[0].astype(BF16)) + b_ref[0]


def _modulation(cond, w_mod, b_mod, tn=512):
    n = w_mod.shape[2]
    return pl.pallas_call(
        _mod_kernel,
        out_shape=jax.ShapeDtypeStruct((DEPTH, MOD_ROWS, n), F32),
        grid=(DEPTH, n // tn),
        in_specs=[
            pl.BlockSpec((MOD_ROWS, D_MODEL), lambda l, j: (0, 0)),
            pl.BlockSpec((1, D_MODEL, tn), lambda l, j: (l, 0, j)),
            pl.BlockSpec((1, 1, tn), lambda l, j: (l, 0, j)),
        ],
        out_specs=pl.BlockSpec((1, MOD_ROWS, tn), lambda l, j: (l, 0, j)),
        compiler_params=_params("arbitrary", "arbitrary"),
        name="modulation",
    )(cond, w_mod, b_mod.reshape(DEPTH, 1, n))


def _mod_spec(chunk, n_ctx_tiles, tiles_per_latent_seq):
    def row(i):
        return jnp.where(i < n_ctx_tiles, 0, 1 + (i - n_ctx_tiles) // tiles_per_latent_seq)
    return pl.BlockSpec((1, 1, D_MODEL), lambda i: (row(i), 0, chunk))


def _modulated_norm(x, g_ref, shift_ref, scale_ref):
    y = x * lax.rsqrt(jnp.mean(x * x, axis=-1, keepdims=True) + EPS) * g_ref[...]
    return (y * (1.0 + scale_ref[0]) + shift_ref[0]).astype(BF16)


def _prenorm_kernel(xc_ref, xl_ref, g_ref, shift_ref, scale_ref, o_ref, *, n_ctx_tiles):
    def body(x_ref):
        o_ref[...] = _modulated_norm(x_ref[...], g_ref, shift_ref, scale_ref)

    _by_group(pl.program_id(0), n_ctx_tiles, body, (xc_ref,), (xl_ref,))


def _prenorm(x_ctx, ctx_row0, x_lat, lat_row0, g, mod3, n_ctx, n_lat, lat_seq):
    tm = 2 * ROW_TILE
    nct = n_ctx // tm
    mod = functools.partial(_mod_spec, n_ctx_tiles=nct, tiles_per_latent_seq=lat_seq // tm)
    return pl.pallas_call(
        functools.partial(_prenorm_kernel, n_ctx_tiles=nct),
        out_shape=jax.ShapeDtypeStruct((n_ctx + n_lat, D_MODEL), BF16),
        grid=((n_ctx + n_lat) // tm,),
        in_specs=_pair_specs((tm, D_MODEL), nct, ctx_row0 // tm, lat_row0 // tm)
        + [pl.BlockSpec((1, D_MODEL), lambda i: (0, 0)), mod(0), mod(1)],
        out_specs=pl.BlockSpec((tm, D_MODEL), lambda i: (i, 0)),
        compiler_params=_params("arbitrary"),
        name="prenorm",
    )(x_ctx, x_lat, g.reshape(1, D_MODEL), mod3, mod3)


def _post_kernel(xc_ref, xl_ref, r_ref, g_ref, gate_ref, *rest, n_ctx_tiles, with_next):
    if with_next:
        g_next_ref, shift_ref, scale_ref, o_ref, h_ref = rest
    else:
        (o_ref,) = rest

    def body(x_ref):
        r = r_ref[...].astype(F32)
        y = r * lax.rsqrt(jnp.mean(r * r, axis=-1, keepdims=True) + EPS) * g_ref[...]
        x = x_ref[...] + gate_ref[0] * y
        o_ref[...] = x
        if with_next:
            h_ref[...] = _modulated_norm(x, g_next_ref, shift_ref, scale_ref)

    _by_group(pl.program_id(0), n_ctx_tiles, body, (xc_ref,), (xl_ref,))


def _post(x_ctx, ctx_row0, x_lat, lat_row0, r, r_row0, g, mod3, n_ctx, n_lat, lat_seq, next_norm=None):
    tm = ROW_TILE
    nct = n_ctx // tm
    r0 = r_row0 // tm
    rows = n_ctx + n_lat
    mod = functools.partial(_mod_spec, n_ctx_tiles=nct, tiles_per_latent_seq=lat_seq // tm)
    vec = pl.BlockSpec((1, D_MODEL), lambda i: (0, 0))
    tile = pl.BlockSpec((tm, D_MODEL), lambda i: (i, 0))
    in_specs = _pair_specs((tm, D_MODEL), nct, ctx_row0 // tm, lat_row0 // tm) + [
        pl.BlockSpec((tm, D_MODEL), lambda i: (r0 + i, 0)), vec, mod(2)]
    args = [x_ctx, x_lat, r, g.reshape(1, D_MODEL), mod3]
    out_shape, out_specs = jax.ShapeDtypeStruct((rows, D_MODEL), F32), tile
    if next_norm is not None:
        g_next, mod3_next = next_norm
        in_specs += [vec, mod(0), mod(1)]
        args += [g_next.reshape(1, D_MODEL), mod3_next, mod3_next]
        out_shape, out_specs = (out_shape, jax.ShapeDtypeStruct((rows, D_MODEL), BF16)), (tile, tile)
    return pl.pallas_call(
        functools.partial(_post_kernel, n_ctx_tiles=nct, with_next=next_norm is not None),
        out_shape=out_shape,
        grid=(rows // tm,),
        in_specs=in_specs,
        out_specs=out_specs,
        compiler_params=_params("arbitrary"),
        name="post",
    )(*args)


def _mm_kernel(a_ref, b_ref, o_ref):
    o_ref[...] = _dot(a_ref[...], b_ref[...]).astype(o_ref.dtype)


def _matmul(a, b, out_dtype, tm, tn, name):
    m, k = a.shape
    n = b.shape[1]
    return pl.pallas_call(
        _mm_kernel,
        out_shape=jax.ShapeDtypeStruct((m, n), out_dtype),
        grid=(m // tm, n // tn),
        in_specs=[pl.BlockSpec((tm, k), lambda i, j: (i, 0)), pl.BlockSpec((k, tn), lambda i, j: (0, j))],
        out_specs=pl.BlockSpec((tm, tn), lambda i, j: (i, j)),
        compiler_params=_params("arbitrary", "arbitrary"),
        name=name,
    )(a, b)


def _mm_wcast_kernel(a_ref, w_ref, o_ref, wb_ref):
    @pl.when(pl.program_id(1) == 0)
    def _():
        wb_ref[...] = w_ref[...].astype(BF16)

    o_ref[...] = _dot(a_ref[...], wb_ref[...]).astype(o_ref.dtype)


def _matmul_wcast(a, w, layer, out_dtype, tm, tn, name, n_out=None, col_map=None):
    m, k = a.shape
    n = n_out or w.shape[-1]
    col = col_map or (lambda j: j)
    return pl.pallas_call(
        _mm_wcast_kernel,
        out_shape=jax.ShapeDtypeStruct((m, n), out_dtype),
        grid=(n // tn, m // tm),
        in_specs=[
            pl.BlockSpec((tm, k), lambda j, i: (i, 0)),
            pl.BlockSpec((None, k, tn), lambda j, i: (layer, 0, col(j))),
        ],
        out_specs=pl.BlockSpec((tm, tn), lambda j, i: (i, j)),
        scratch_shapes=[pltpu.VMEM((k, tn), BF16)],
        compiler_params=_params("arbitrary", "arbitrary"),
        name=name,
    )(a, w)


def _kv_kernel(h_ref, w_ref, kn_ref, k_ref, v_ref):
    acc = _dot(h_ref[...], w_ref[...])
    for hd in range(N_KV_HEADS):
        sl = slice(hd * HEAD_DIM, (hd + 1) * HEAD_DIM)
        k_ref[:, sl] = _norm_head(acc[:, sl], kn_ref[...])
    v_ref[...] = acc[:, KV_W:]


def _kv_proj(h, w_kv, layer, k_norm, tm=512):
    t = h.shape[0]
    return pl.pallas_call(
        _kv_kernel,
        out_shape=(jax.ShapeDtypeStruct((t, KV_W), F32), jax.ShapeDtypeStruct((t, KV_W), F32)),
        grid=(t // tm,),
        in_specs=[
            pl.BlockSpec((tm, D_MODEL), lambda i: (i, 0)),
            pl.BlockSpec((None, D_MODEL, W_KV), lambda i: (layer, 0, 0)),
            pl.BlockSpec((1, HEAD_DIM), lambda i: (0, 0)),
        ],
        out_specs=(pl.BlockSpec((tm, KV_W), lambda i: (i, 0)), pl.BlockSpec((tm, KV_W), lambda i: (i, 0))),
        compiler_params=_params("arbitrary"),
        name="kv_proj",
    )(h, w_kv, k_norm.reshape(1, HEAD_DIM))


def _cache_kernel(*refs):
    k_refs, v_refs, (ko_ref, vo_ref) = refs[:DEPTH], refs[DEPTH:2 * DEPTH], refs[2 * DEPTH:]
    n_blk, seq = ko_ref.shape[:2]
    for l in range(DEPTH):
        @pl.when(pl.program_id(0) == l)
        def _():
            for s in range(n_blk):
                rows = slice(s * seq, (s + 1) * seq)
                for hd in range(N_KV_HEADS):
                    sl = slice(hd * HEAD_DIM, (hd + 1) * HEAD_DIM)
                    ko_ref[s, :, hd, :] = k_refs[l][rows, sl]
                    vo_ref[s, :, hd, :] = v_refs[l][rows, sl]


def _new_cache(ks, vs, n_seq, seq, seq_blk=4):
    def layer_spec(l):
        return pl.BlockSpec((seq_blk * seq, KV_W), lambda cur, b: (jnp.where(cur == l, b, 0), 0))

    out_sds = jax.ShapeDtypeStruct((n_seq, DEPTH, seq, N_KV_HEADS, HEAD_DIM), F32)
    out_spec = pl.BlockSpec((seq_blk, None, seq, N_KV_HEADS, HEAD_DIM), lambda cur, b: (b, cur, 0, 0, 0))
    return pl.pallas_call(
        _cache_kernel,
        out_shape=(out_sds, out_sds),
        grid=(DEPTH, n_seq // seq_blk),
        in_specs=[layer_spec(l) for l in range(DEPTH)] * 2,
        out_specs=(out_spec, out_spec),
        compiler_params=_params("arbitrary", "arbitrary"),
        name="new_cache",
    )(*ks, *vs)


def _softmax_pv(s, v):
    m = jnp.max(s, axis=-1, keepdims=True)
    p = jnp.exp2(s - m)
    l = jnp.sum(p, axis=-1, keepdims=True)
    return _dot(p.astype(BF16), v) / l


def _qk(q, k):
    return lax.dot_general(q, k, (((1,), (1,)), ((), ())), preferred_element_type=F32)


def _norm_head(q, g):
    return q * lax.rsqrt(jnp.mean(q * q, axis=-1, keepdims=True) + EPS) * g


def _rope(x, cos2, sin2):
    return x * cos2 + pltpu.roll(x, HEAD_DIM // 2, 1) * sin2


def _attn_ctx_kernel(q_ref, k_ref, v_ref, ag_ref, qn_ref, o_ref):
    seq = q_ref.shape[0]
    for hd in range(N_KV_HEADS):
        kv = slice(hd * HEAD_DIM, (hd + 1) * HEAD_DIM)
        cols = [slice((hd * KV_GROUP + g) * HEAD_DIM, (hd * KV_GROUP + g + 1) * HEAD_DIM) for g in range(KV_GROUP)]
        q = jnp.concatenate([(_norm_head(q_ref[:, c].astype(F32), qn_ref[...]) * Q_SCALE).astype(BF16)
                             for c in cols], axis=0)
        o = _softmax_pv(_qk(q, k_ref[:, kv].astype(BF16)), v_ref[:, kv].astype(BF16))
        for g, c in enumerate(cols):
            o_ref[:, c] = (o[g * seq:(g + 1) * seq] * _silu(ag_ref[:, c].astype(F32))).astype(BF16)


def _attention_ctx(proj, k, v, q_norm, n_seq, seq):
    return pl.pallas_call(
        _attn_ctx_kernel,
        out_shape=jax.ShapeDtypeStruct((n_seq * seq, ATTN_W), BF16),
        grid=(n_seq,),
        in_specs=[
            pl.BlockSpec((seq, ATTN_W), lambda b: (b, OFF_Q // ATTN_W)),
            pl.BlockSpec((seq, KV_W), lambda b: (b, 0)),
            pl.BlockSpec((seq, KV_W), lambda b: (b, 0)),
            pl.BlockSpec((seq, ATTN_W), lambda b: (b, OFF_AGATE // ATTN_W)),
            pl.BlockSpec((1, HEAD_DIM), lambda b: (0, 0)),
        ],
        out_specs=pl.BlockSpec((seq, ATTN_W), lambda b: (b, 0)),
        compiler_params=_params("arbitrary"),
        name="attn_ctx",
    )(proj, k, v, proj, q_norm.reshape(1, HEAD_DIM))


def _attn_lat_kernel(q_ref, k_ref, v_ref, ck_ref, cv_ref, ag_ref, qn_ref, cosq_ref, sinq_ref,
                     cosk_ref, sink_ref, o_ref, k_s, v_s, *, seq):
    @pl.when(pl.program_id(2) == 0)
    def _():
        k_s[:seq] = _rope(k_ref[...], cosk_ref[...], sink_ref[...]).astype(BF16)
        k_s[seq:] = ck_ref[...].astype(BF16)
        v_s[:seq] = v_ref[...].astype(BF16)
        v_s[seq:] = cv_ref[...].astype(BF16)

    k = k_s[...]
    v = v_s[...]
    for g in range(KV_GROUP):
        sl = slice(g * HEAD_DIM, (g + 1) * HEAD_DIM)
        for r0 in range(0, q_ref.shape[0], ATTN_ROWS):
            rows = slice(r0, r0 + ATTN_ROWS)
            q = _norm_head(q_ref[rows, sl].astype(F32), qn_ref[...])
            q = _rope(q, cosq_ref[rows, :], sinq_ref[rows, :]) * Q_SCALE
            o = _softmax_pv(_qk(q.astype(BF16), k), v)
            o_ref[rows, sl] = (o * _silu(ag_ref[rows, sl].astype(F32))).astype(BF16)


def _attention_lat(proj, k, v, cache_k, cache_v, layer, q_norm, cos2, sin2, n_ctx, n_seq, seq, tq=1024):
    past = cache_k.shape[2]
    ck = cache_k.reshape(cache_k.shape[0], DEPTH, past, KV_W)
    cv = cache_v.reshape(cache_v.shape[0], DEPTH, past, KV_W)
    nq = seq // tq
    row0, seq0 = n_ctx // tq, n_ctx // seq
    return pl.pallas_call(
        functools.partial(_attn_lat_kernel, seq=seq),
        out_shape=jax.ShapeDtypeStruct((n_seq * seq, ATTN_W), BF16),
        grid=(n_seq, N_KV_HEADS, nq),
        in_specs=[
            pl.BlockSpec((tq, GROUP_W), lambda b, h, i: (row0 + b * nq + i, OFF_Q // GROUP_W + h)),
            pl.BlockSpec((seq, HEAD_DIM), lambda b, h, i: (seq0 + b, h)),
            pl.BlockSpec((seq, HEAD_DIM), lambda b, h, i: (seq0 + b, h)),
            pl.BlockSpec((None, None, past, HEAD_DIM), lambda b, h, i: (b, layer, 0, h)),
            pl.BlockSpec((None, None, past, HEAD_DIM), lambda b, h, i: (b, layer, 0, h)),
            pl.BlockSpec((tq, GROUP_W), lambda b, h, i: (row0 + b * nq + i, OFF_AGATE // GROUP_W + h)),
            pl.BlockSpec((1, HEAD_DIM), lambda b, h, i: (0, 0)),
            pl.BlockSpec((tq, HEAD_DIM), lambda b, h, i: (i, 0)),
            pl.BlockSpec((tq, HEAD_DIM), lambda b, h, i: (i, 0)),
            pl.BlockSpec((seq, HEAD_DIM), lambda b, h, i: (0, 0)),
            pl.BlockSpec((seq, HEAD_DIM), lambda b, h, i: (0, 0)),
        ],
        out_specs=pl.BlockSpec((tq, GROUP_W), lambda b, h, i: (b * nq + i, h)),
        scratch_shapes=[pltpu.VMEM((seq + past, HEAD_DIM), BF16), pltpu.VMEM((seq + past, HEAD_DIM), BF16)],
        compiler_params=_params("arbitrary", "arbitrary", "arbitrary"),
        name="attn_lat",
    )(proj, k, v, ck, cv, proj, q_norm.reshape(1, HEAD_DIM), cos2, sin2, cos2, sin2)


def _rope_tables(seq):
    rows = seq // GRID_W
    row = jnp.repeat(jnp.arange(rows, dtype=F32), GRID_W)
    col = jnp.tile(jnp.arange(GRID_W, dtype=F32), rows)
    inv = ROPE_THETA ** (-jnp.arange(ROPE_PAIRS_AXIS, dtype=F32) / ROPE_PAIRS_AXIS)
    ang = jnp.concatenate([row[:, None] * inv, col[:, None] * inv], axis=-1)
    cos, sin = jnp.cos(ang), jnp.sin(ang)
    return jnp.concatenate([cos, cos], axis=-1), jnp.concatenate([-sin, sin], axis=-1)


def _dft_angle(i, j, n):
    return (2.0 * math.pi / n) * ((i[:, None] * j[None, :]) % n).astype(F32)


def _cos_sin_table(rows, cols, n):
    c = jnp.arange(cols, dtype=jnp.int32)
    hi = _dft_angle(TABLE_SPLIT * jnp.arange(rows // TABLE_SPLIT, dtype=jnp.int32), c, n)
    lo = _dft_angle(jnp.arange(TABLE_SPLIT, dtype=jnp.int32), c, n)
    ch, sh = jnp.cos(hi)[:, None, :], jnp.sin(hi)[:, None, :]
    cl, sl = jnp.cos(lo)[None, :, :], jnp.sin(lo)[None, :, :]
    return (ch * cl - sh * sl).reshape(rows, cols), (sh * cl + ch * sl).reshape(rows, cols)


def _fnet_kernel(u_ref, fg_ref, csc_ref, cos_ref, sin_ref, o_ref, tc_s, ts_s, *, scale):
    @pl.when(pl.program_id(1) == 0)
    def _():
        for g in range(FNET_GROUPS):
            sl = slice(g * FNET_GROUP_W, (g + 1) * FNET_GROUP_W)
            t = _dot(u_ref[:, sl], csc_ref[...])
            tc_s[:, sl] = t[:, :FNET_GROUP_W].astype(BF16)
            ts_s[:, sl] = t[:, FNET_GROUP_W:].astype(BF16)

    y = (_dot(cos_ref[...], tc_s[...]) - _dot(sin_ref[...], ts_s[...])) * scale
    o_ref[...] = (y * _silu(fg_ref[...].astype(F32))).astype(BF16)


def _fnet_tables(seq):
    j = jnp.arange(FNET_GROUP_W, dtype=jnp.int32)
    ang = _dft_angle(j, j, FNET_GROUP_W)
    csc = jnp.concatenate([jnp.cos(ang), jnp.sin(ang)], axis=1).astype(BF16)
    cos, sin = _cos_sin_table(seq, seq, seq)
    return csc, cos.astype(BF16), sin.astype(BF16)


def _fnet(proj, tables, row_start, n_seq, seq, tr=256):
    csc, cos, sin = tables
    nr = seq // tr
    s0, r0 = row_start // seq, row_start // tr
    return pl.pallas_call(
        functools.partial(_fnet_kernel, scale=1.0 / math.sqrt(seq * FNET_GROUP_W)),
        out_shape=jax.ShapeDtypeStruct((n_seq * seq, FNET_W), BF16),
        grid=(n_seq, nr),
        in_specs=[
            pl.BlockSpec((seq, FNET_W), lambda b, r: (s0 + b, OFF_FIN // FNET_W)),
            pl.BlockSpec((tr, FNET_W), lambda b, r: (r0 + b * nr + r, OFF_FGATE // FNET_W)),
            pl.BlockSpec((FNET_GROUP_W, 2 * FNET_GROUP_W), lambda b, r: (0, 0)),
            pl.BlockSpec((tr, seq), lambda b, r: (r, 0)),
            pl.BlockSpec((tr, seq), lambda b, r: (r, 0)),
        ],
        out_specs=pl.BlockSpec((tr, FNET_W), lambda b, r: (b * nr + r, 0)),
        scratch_shapes=[pltpu.VMEM((seq, FNET_W), BF16), pltpu.VMEM((seq, FNET_W), BF16)],
        compiler_params=_params("arbitrary", "arbitrary"),
        name=f"fnet_{seq}",
    )(proj, proj, csc, cos, sin)


def _short_kernel(x_ref, w_ref, b_ref, o_ref, *, seq):
    x = x_ref[...].astype(F32)
    n = x.shape[0]
    pos = lax.broadcasted_iota(jnp.int32, x.shape, 0) % seq
    prev = jnp.where(pos == 0, 0.0, pltpu.roll(x, 1, 0))
    nxt = jnp.where(pos == seq - 1, 0.0, pltpu.roll(x, n - 1, 0))
    y = b_ref[...] + prev * w_ref[0:1, :] + x * w_ref[1:2, :] + nxt * w_ref[2:3, :]
    o_ref[...] = y.astype(BF16)


def _short_conv(proj, w, b, row_start, n_rows, seq, tr=2048, tw=512):
    width = 3 * HYENA_W
    r0 = row_start // tr
    return pl.pallas_call(
        functools.partial(_short_kernel, seq=seq),
        out_shape=jax.ShapeDtypeStruct((n_rows, width), BF16),
        grid=(n_rows // tr, width // tw),
        in_specs=[
            pl.BlockSpec((tr, tw), lambda s, c: (r0 + s, OFF_HV // tw + c)),
            pl.BlockSpec((HYENA_SHORT, tw), lambda s, c: (0, c)),
            pl.BlockSpec((1, tw), lambda s, c: (0, c)),
        ],
        out_specs=pl.BlockSpec((tr, tw), lambda s, c: (s, c)),
        compiler_params=_params("arbitrary", "arbitrary"),
        name=f"short_conv_{seq}",
    )(proj, w, b.reshape(1, width))


def _filt_mlp_kernel(feats_ref, w1_ref, b1_ref, w2_ref, b2_ref, fr_ref, o_ref):
    fr = fr_ref[...]
    h = jnp.sin(fr * (_dot_exact(feats_ref[...], w1_ref[...]) + b1_ref[...]))
    o_ref[...] = jnp.sin(fr * (_dot_exact(h, w2_ref[...]) + b2_ref[...]))


def _filt_kernel(h_ref, t_ref, w3f_ref, w3b_ref, b3f_ref, b3b_ref, dl_ref, sum_ref, dif_ref, nyq_ref):
    h = h_ref[...]
    decay = jnp.exp(-t_ref[...] * dl_ref[...])
    hf = (_dot_exact(h, w3f_ref[...]) + b3f_ref[...]) * decay
    hb = (_dot_exact(h, w3b_ref[...]) + b3b_ref[...]) * decay
    den = jnp.sum(jnp.abs(hf), axis=0, keepdims=True) + jnp.sum(jnp.abs(hb), axis=0, keepdims=True) + EPS
    hs = (hf + hb) / den
    sum_ref[...] = hs.astype(BF16)
    dif_ref[...] = ((hf - hb) / den).astype(BF16)
    sign = (1 - 2 * (lax.broadcasted_iota(jnp.int32, hs.shape, 0) % 2)).astype(F32)
    nyq_ref[...] = jnp.sum(hs * sign, axis=0, keepdims=True)


def _hyena_filters(seq, w1, b1, w2, b2, w3, b3, freq, tw=512):
    t = jnp.arange(seq, dtype=F32)[:, None] / seq
    bands = jnp.arange(1, HYENA_BANDS + 1, dtype=F32)[None, :]
    feats = jnp.concatenate([t, jnp.cos(2 * math.pi * t * bands), jnp.sin(2 * math.pi * t * bands),
                             jnp.zeros((seq, HYENA_POS_PAD - HYENA_POS_DIM), F32)], axis=-1)
    w1p = jnp.concatenate([w1, jnp.zeros((HYENA_POS_PAD - HYENA_POS_DIM, HYENA_FFN_W), F32)], axis=0)
    deltas = jnp.abs(jnp.linspace(HYENA_MIN_DECAY, HYENA_MAX_DECAY, HYENA_W, dtype=F32))[None, :]
    hidden = pl.pallas_call(
        _filt_mlp_kernel,
        out_shape=jax.ShapeDtypeStruct((seq, HYENA_FFN_W), F32),
        name=f"hyena_filter_mlp_{seq}",
    )(feats, w1p, b1.reshape(1, -1), w2, b2.reshape(1, -1), freq.reshape(1, -1))
    nw = HYENA_W // tw
    small = lambda shape: pl.BlockSpec(shape, lambda o, c: (0, 0))
    out_sds = jax.ShapeDtypeStruct((seq, HYENA_ORDER * HYENA_W), BF16)
    fwd = lambda o, c: (0, (2 * o) * nw + c)
    bwd = lambda o, c: (0, (2 * o + 1) * nw + c)
    b3r = b3.reshape(1, -1)
    return pl.pallas_call(
        _filt_kernel,
        out_shape=(out_sds, out_sds, jax.ShapeDtypeStruct((1, HYENA_ORDER * HYENA_W), F32)),
        grid=(HYENA_ORDER, nw),
        in_specs=[
            small((seq, HYENA_FFN_W)), small((seq, 1)),
            pl.BlockSpec((HYENA_FFN_W, tw), fwd), pl.BlockSpec((HYENA_FFN_W, tw), bwd),
            pl.BlockSpec((1, tw), fwd), pl.BlockSpec((1, tw), bwd),
            pl.BlockSpec((1, tw), lambda o, c: (0, c)),
        ],
        out_specs=(pl.BlockSpec((seq, tw), lambda o, c: (0, o * nw + c)),
                   pl.BlockSpec((seq, tw), lambda o, c: (0, o * nw + c)),
                   pl.BlockSpec((1, tw), lambda o, c: (0, o * nw + c))),
        compiler_params=_params("arbitrary", "arbitrary"),
        name=f"hyena_filters_{seq}",
    )(hidden, t, w3, w3, b3r, b3r, deltas)


def _conv_tables(seq):
    i = jnp.arange(seq, dtype=jnp.int32)
    nyq = (1 - 2 * (i % 2)).astype(F32)
    cos, sin = _cos_sin_table(seq, seq, 2 * seq)
    sin_f = jnp.where(i[:, None] == 0, nyq[None, :], sin)
    sin_t = jnp.where(i[None, :] == 0, nyq[:, None], sin)
    return cos.astype(BF16), sin_f.astype(BF16), sin_t.astype(BF16)


def _filter_bins(uc_ref, us_ref, nyq_ref, first_chunk):
    gr = uc_ref[...]
    packed = jnp.logical_and(lax.broadcasted_iota(jnp.int32, gr.shape, 0) == 0, first_chunk)
    gi = jnp.where(packed, 0.0, us_ref[...])
    gn = jnp.where(packed, nyq_ref[...], gr)
    wgt = jnp.where(packed, 1.0, 2.0)
    return gr * wgt, gi * wgt, gn * wgt


def _spectrum_product(cos_ref, sin_ref, z, bins):
    gr, gi, gn = bins
    zr = _dot(cos_ref[...], z)
    zi = _dot(sin_ref[...], z)
    return (zr * gr - zi * gi).astype(BF16), (zr * gi + zi * gn).astype(BF16)


def _conv_epilogue(y, z, bias_ref, xm, gate, n):
    y = (y * (1.0 / n) + z.astype(F32) * bias_ref[...]) * xm.astype(F32)
    if gate is not None:
        y = y * _silu(gate.astype(F32))
    return y.astype(BF16)


def _lconv_short_kernel(*refs, seq, n_seq_blk, gated):
    refs = list(refs)
    z_ref, xm_ref = refs[:2]
    gate_ref = refs[2] if gated else None
    bias_ref, cos_ref, sin_ref, sin_t_ref, uc_ref, us_ref, nyq_ref, o_ref = refs[2 + gated:]
    bins = _filter_bins(uc_ref, us_ref, nyq_ref, True)
    for s in range(n_seq_blk):
        rows = slice(s * seq, (s + 1) * seq)
        z = z_ref[rows, :]
        pr, pi = _spectrum_product(cos_ref, sin_ref, z, bins)
        y = _dot(cos_ref[...], pr) + _dot(sin_t_ref[...], pi)
        o_ref[rows, :] = _conv_epilogue(y, z, bias_ref, xm_ref[rows, :],
                                        gate_ref[rows, :] if gated else None, 2 * seq)


def _lconv_long_kernel(*refs, seq, chunk, gated):
    refs = list(refs)
    z_ref, xm_ref = refs[:2]
    gate_ref = refs[2] if gated else None
    bias_ref, cos_ref, sin_ref, sin_t_ref, uc_ref, us_ref, nyq_ref, o_ref, pr_s, pi_s = refs[2 + gated:]
    n_chunks = seq // chunk
    p = pl.program_id(2)

    @pl.when(p < n_chunks)
    def _():
        bins = _filter_bins(uc_ref, us_ref, nyq_ref, p == 0)
        pr, pi = _spectrum_product(cos_ref, sin_ref, z_ref[...], bins)
        rows = pl.ds(pl.multiple_of(p * chunk, chunk), chunk)
        pr_s[rows, :] = pr
        pi_s[rows, :] = pi

    @pl.when(p >= n_chunks)
    def _():
        y = _dot(cos_ref[...], pr_s[...]) + _dot(sin_t_ref[...], pi_s[...])
        rows = pl.ds(pl.multiple_of((p - n_chunks) * chunk, chunk), chunk)
        o_ref[...] = _conv_epilogue(y, z_ref[rows, :], bias_ref, xm_ref[...],
                                    gate_ref[...] if gated else None, 2 * seq)


def _long_conv(z, z_col, xm, xm_col, gate, gate_row0, bias, idx, order, tables, spectrum,
               n_seq, seq, tw, fc, seq_blk):
    nw, nf = HYENA_W // tw, seq // fc
    gated = gate is not None
    out_shape = jax.ShapeDtypeStruct((n_seq * seq, HYENA_W), BF16)
    bias_spec = pl.BlockSpec((None, 1, tw), lambda s, w, p: (idx, 0, w))
    nyq_spec = pl.BlockSpec((1, tw), lambda s, w, p: (0, order * nw + w))
    if nf == 1:
        tr = seq_blk * seq
        tok = lambda col, r0=0: pl.BlockSpec((tr, tw), lambda s, w, p: (r0 + s, col * nw + w))
        taps = pl.BlockSpec((seq, tw), lambda s, w, p: (0, order * nw + w))
        table = pl.BlockSpec((seq, seq), lambda s, w, p: (0, 0))
        in_specs = [tok(z_col), tok(xm_col)] + ([tok(OFF_HGATE // HYENA_W, gate_row0 // tr)] if gated else [])
        in_specs += [bias_spec, table, table, table, taps, taps, nyq_spec]
        kern = functools.partial(_lconv_short_kernel, seq=seq, n_seq_blk=seq_blk, gated=gated)
        grid, scratch = (n_seq // seq_blk, nw, 1), []
        out_spec = pl.BlockSpec((tr, tw), lambda s, w, p: (s, w))
    else:
        assert seq_blk == 1
        fwd_chunk = lambda p: jnp.minimum(p, nf - 1)
        out_chunk = lambda p: jnp.maximum(p - nf, 0)
        rows = lambda col, r0=0: pl.BlockSpec((fc, tw), lambda s, w, p: (r0 + s * nf + out_chunk(p), col * nw + w))
        taps = pl.BlockSpec((fc, tw), lambda s, w, p: (fwd_chunk(p), order * nw + w))
        in_specs = [pl.BlockSpec((seq, tw), lambda s, w, p: (s, z_col * nw + w)), rows(xm_col)]
        in_specs += [rows(OFF_HGATE // HYENA_W, gate_row0 // fc)] if gated else []
        in_specs += [
            bias_spec,
            pl.BlockSpec((fc, seq), lambda s, w, p: (p % nf, 0)),
            pl.BlockSpec((fc, seq), lambda s, w, p: (fwd_chunk(p), 0)),
            pl.BlockSpec((fc, seq), lambda s, w, p: (out_chunk(p), 0)),
            taps, taps, nyq_spec,
        ]
        kern = functools.partial(_lconv_long_kernel, seq=seq, chunk=fc, gated=gated)
        grid, scratch = (n_seq, nw, 2 * nf), [pltpu.VMEM((seq, tw), BF16), pltpu.VMEM((seq, tw), BF16)]
        out_spec = pl.BlockSpec((fc, tw), lambda s, w, p: (s * nf + out_chunk(p), w))
    args = [z, xm] + ([gate] if gated else []) + [bias, *tables, *spectrum]
    return pl.pallas_call(
        kern,
        out_shape=out_shape,
        grid=grid,
        in_specs=in_specs,
        out_specs=out_spec,
        scratch_shapes=scratch,
        compiler_params=_params("arbitrary", "arbitrary", "arbitrary"),
        name=f"long_conv_{seq}_{order}",
    )(*args)


def _merge_kernel(ac_ref, al_ref, fc_ref, fl_ref, hc_ref, hl_ref, wa_ref, wf_ref, wh_ref,
                  ga_ref, gf_ref, gh_ref, o_ref, wab_ref, wfb_ref, whb_ref, acc_ref, *, n_ctx_tiles):
    i = pl.program_id(1)

    @pl.when(i == 0)
    def _():
        wab_ref[...] = wa_ref[...].astype(BF16)
        wfb_ref[...] = wf_ref[...].astype(BF16)
        whb_ref[...] = wh_ref[...].astype(BF16)

    def body(a_ref, f_ref, h_ref):
        acc_ref[...] = jax.nn.sigmoid(ga_ref[...].astype(F32)) * _dot(a_ref[...], wab_ref[...])
        acc_ref[...] += jax.nn.sigmoid(gf_ref[...].astype(F32)) * _dot(f_ref[...], wfb_ref[...])
        m = acc_ref[...] + jax.nn.sigmoid(gh_ref[...].astype(F32)) * _dot(h_ref[...], whb_ref[...])
        o_ref[...] = m.astype(BF16)

    _by_group(i, n_ctx_tiles, body, (ac_ref, fc_ref, hc_ref), (al_ref, fl_ref, hl_ref))


def _merge(attn, fnet, hy, w_a, w_f, w_h, layer, proj, n_ctx, tm=512, tn=1024):
    t = proj.shape[0]
    nct = n_ctx // tm
    act = lambda w: _pair_specs((tm, w), nct, inner=True)
    wgt = lambda k: pl.BlockSpec((None, k, tn), lambda j, i: (layer, 0, j), pipeline_mode=pl.Buffered(1))
    gate = lambda off: pl.BlockSpec((tm, tn), lambda j, i: (i, off // tn + j))
    return pl.pallas_call(
        functools.partial(_merge_kernel, n_ctx_tiles=nct),
        out_shape=jax.ShapeDtypeStruct((t, D_MODEL), BF16),
        grid=(D_MODEL // tn, t // tm),
        in_specs=act(ATTN_W) + act(FNET_W) + act(HYENA_W) + [wgt(ATTN_W), wgt(FNET_W), wgt(HYENA_W),
                                                             gate(OFF_GA), gate(OFF_GF), gate(OFF_GH)],
        out_specs=pl.BlockSpec((tm, tn), lambda j, i: (i, j)),
        scratch_shapes=[pltpu.VMEM((ATTN_W, tn), BF16), pltpu.VMEM((FNET_W, tn), BF16),
                        pltpu.VMEM((HYENA_W, tn), BF16), pltpu.VMEM((tm, tn), F32)],
        compiler_params=_params("arbitrary", "arbitrary"),
        name="merge",
    )(*attn, *fnet, *hy, w_a, w_f, w_h, proj, proj, proj)


def kernel(x_prompt, x_sample, cache_k, cache_v, c, c_ctx, w_mod, b_mod, g_pre, w_in, q_norm, k_norm, hy_short_w, hy_short_b, hy_ffn_w1, hy_ffn_b1, hy_ffn_w2, hy_ffn_b2, hy_ffn_w3, hy_ffn_b3, hy_sin_freq, hy_bias, w_attn_o, w_fnet_o, w_hy_o, w_out, g_post):
    nb, seq, _ = x_prompt.shape
    db, dseq, _ = x_sample.shape
    assert 1 + db <= MOD_ROWS
    n_ctx, n_lat = nb * seq, db * dseq

    cond = jnp.concatenate([c_ctx[None, :], c, jnp.zeros((MOD_ROWS - 1 - db, D_MODEL), F32)], axis=0)
    mod = _modulation(cond, w_mod, b_mod)

    w_kv_b = w_in[:, :, W_OFF_K:W_OFF_K + W_KV].astype(BF16)
    cos2, sin2 = _rope_tables(dseq)
    fnet_tables = {s: _fnet_tables(s) for s in (seq, dseq)}
    conv_tables = {s: _conv_tables(s) for s in (seq, dseq)}
    bias = hy_bias.reshape(DEPTH * HYENA_ORDER, 1, HYENA_W)
    conv_tiles = {seq: (HYENA_W, seq, 4), dseq: (HYENA_W, 512, 1)}
    proj_tm, proj_tn = 512, 1024
    skip_kv = lambda j: jnp.where(j >= W_OFF_K // proj_tn, j + W_KV // proj_tn, j)

    xs = (x_prompt.reshape(n_ctx, D_MODEL), 0, x_sample.reshape(n_lat, D_MODEL), 0)
    new_k, new_v = [], []
    mod3s = [mod[l].reshape(MOD_ROWS, 1, 3 * D_MODEL) for l in range(DEPTH)]
    h = _prenorm(*xs, g_pre[0], mod3s[0], n_ctx, n_lat, dseq)
    for l in range(DEPTH):
        mod3 = mod3s[l]
        proj = _matmul_wcast(h, w_in, l, BF16, proj_tm, proj_tn, "in_proj", n_out=PROJ_W, col_map=skip_kv)
        k, v = _kv_proj(h, w_kv_b, l, k_norm[l])
        new_k.append(k)
        new_v.append(v)

        attn = (_attention_ctx(proj, k, v, q_norm[l], nb, seq),
                _attention_lat(proj, k, v, cache_k, cache_v, l, q_norm[l], cos2, sin2, n_ctx, db, dseq))

        fnet, hy = [], []
        for row_start, n_seq, s in ((0, nb, seq), (n_ctx, db, dseq)):
            fnet.append(_fnet(proj, fnet_tables[s], row_start, n_seq, s))
            u = _short_conv(proj, hy_short_w[l], hy_short_b[l], row_start, n_seq * s, s)
            tap_sum, tap_dif, tap_nyq = _hyena_filters(s, hy_ffn_w1[l], hy_ffn_b1[l], hy_ffn_w2[l], hy_ffn_b2[l],
                                                       hy_ffn_w3[l], hy_ffn_b3[l], hy_sin_freq[l])
            cos, sin_f, _ = conv_tables[s]
            spec_tm = min(1024, s)
            spectrum = (_matmul(cos, tap_sum, F32, spec_tm, 1024, f"filter_spectrum_cos_{s}"),
                        _matmul(sin_f, tap_dif, F32, spec_tm, 1024, f"filter_spectrum_sin_{s}"), tap_nyq)
            tw, fc, blk = conv_tiles[s]
            z1 = _long_conv(u, 0, u, 1, None, 0, bias, l * HYENA_ORDER, 0, conv_tables[s],
                            spectrum, n_seq, s, tw, fc, blk)
            hy.append(_long_conv(z1, 0, u, 2, proj, row_start, bias, l * HYENA_ORDER + 1, 1, conv_tables[s],
                                 spectrum, n_seq, s, tw, fc, blk))

        merged = _merge(attn, fnet, hy, w_attn_o, w_fnet_o, w_hy_o, l, proj, n_ctx)
        r = _matmul_wcast(merged, w_out, l, BF16, proj_tm, proj_tn, "out_proj")
        if l < DEPTH - 1:
            x_new, h = _post(*xs, r, 0, g_post[l], mod3, n_ctx, n_lat, dseq, next_norm=(g_pre[l + 1], mod3s[l + 1]))
            xs = (x_new, 0, x_new, n_ctx)
        else:
            y_prompt = _post(*xs, r, 0, g_post[l], mod3, n_ctx, 0, dseq)
            y_sample = _post(*xs, r, n_ctx, g_post[l], mod3, 0, n_lat, dseq)

    cache_k_new, cache_v_new = _new_cache(new_k, new_v, nb, seq)
    return (y_prompt.reshape(nb, seq, D_MODEL), y_sample.reshape(db, dseq, D_MODEL), cache_k_new, cache_v_new)
```

```python
import functools
import math

import jax
import jax.numpy as jnp
from jax import lax
from jax.experimental import pallas as pl
from jax.experimental.pallas import tpu as pltpu

F32 = jnp.float32
BF16 = jnp.bfloat16

D_MODEL = 4096
DEPTH = 2
GRID_W = 64
HEAD_DIM = 128
N_HEADS = 16
N_KV_HEADS = 4
KV_GROUP = N_HEADS // N_KV_HEADS
ATTN_W = N_HEADS * HEAD_DIM
KV_W = N_KV_HEADS * HEAD_DIM
GROUP_W = KV_GROUP * HEAD_DIM
ATTN_SCALE = HEAD_DIM ** -0.5
Q_SCALE = ATTN_SCALE * math.log2(math.e)
ROPE_THETA = 10000.0
ROPE_PAIRS_AXIS = HEAD_DIM // 4
FNET_W = D_MODEL // 4
FNET_GROUPS = 4
FNET_GROUP_W = FNET_W // FNET_GROUPS
HYENA_W = D_MODEL // 4
HYENA_ORDER = 2
HYENA_SHORT = 3
HYENA_BANDS = 16
HYENA_POS_DIM = 1 + 2 * HYENA_BANDS
HYENA_POS_PAD = 128
HYENA_FFN_W = 64
HYENA_MIN_DECAY = math.log(1e-2) / 0.3
HYENA_MAX_DECAY = math.log(1e-2) / 1.5
EPS = 1e-6

W_OFF_K = ATTN_W
W_KV = 2 * KV_W
OFF_Q = 0
OFF_AGATE = OFF_Q + ATTN_W
OFF_FIN = OFF_AGATE + ATTN_W
OFF_FGATE = OFF_FIN + FNET_W
OFF_HV = OFF_FGATE + FNET_W
OFF_HX1 = OFF_HV + HYENA_W
OFF_HX2 = OFF_HX1 + HYENA_W
OFF_HGATE = OFF_HX2 + HYENA_W
OFF_GA = OFF_HGATE + HYENA_W
OFF_GF = OFF_GA + D_MODEL
OFF_GH = OFF_GF + D_MODEL
PROJ_W = OFF_GH + D_MODEL

MOD_ROWS = 8
ROW_TILE = 256
ATTN_ROWS = 256
TABLE_SPLIT = 64
VMEM_LIMIT = 56 * 1024 * 1024


def _params(*sem):
    return pltpu.CompilerParams(dimension_semantics=sem, vmem_limit_bytes=VMEM_LIMIT)


def _silu(x):
    return x * jax.nn.sigmoid(x)


def _dot(a, b):
    return jnp.dot(a, b, preferred_element_type=F32)


def _dot_exact(a, b):
    return jnp.dot(a, b, preferred_element_type=F32, precision=lax.Precision.HIGHEST)


def _pair_specs(block, n_ctx_tiles, ctx_tile0=0, lat_tile0=0, col=0, inner=False):
    def ctx(i):
        return (ctx_tile0 + jnp.clip(i, 0, max(n_ctx_tiles - 1, 0)), col)

    def lat(i):
        return (lat_tile0 + jnp.maximum(i - n_ctx_tiles, 0), col)

    if inner:
        return [pl.BlockSpec(block, lambda j, i: ctx(i)), pl.BlockSpec(block, lambda j, i: lat(i))]
    return [pl.BlockSpec(block, lambda i: ctx(i)), pl.BlockSpec(block, lambda i: lat(i))]


def _by_group(i, n_ctx_tiles, body, ctx_refs, lat_refs):
    @pl.when(i < n_ctx_tiles)
    def _():
        body(*ctx_refs)

    @pl.when(i >= n_ctx_tiles)
    def _():
        body(*lat_refs)


def _mod_kernel(c_ref, w_ref, b_ref, o_ref):
    s = _silu(c_ref[...]).astype(BF16)
    o_ref[0] = _dot(s, w_ref[0].astype(BF16)) + b_ref[0]


def _modulation(cond, w_mod, b_mod, tn=512):
    n = w_mod.shape[2]
    return pl.pallas_call(
        _mod_kernel,
        out_shape=jax.ShapeDtypeStruct((DEPTH, MOD_ROWS, n), F32),
        grid=(DEPTH, n // tn),
        in_specs=[
            pl.BlockSpec((MOD_ROWS, D_MODEL), lambda l, j: (0, 0)),
            pl.BlockSpec((1, D_MODEL, tn), lambda l, j: (l, 0, j)),
            pl.BlockSpec((1, 1, tn), lambda l, j: (l, 0, j)),
        ],
        out_specs=pl.BlockSpec((1, MOD_ROWS, tn), lambda l, j: (l, 0, j)),
        compiler_params=_params("arbitrary", "arbitrary"),
        name="modulation",
    )(cond, w_mod, b_mod.reshape(DEPTH, 1, n))


def _mod_spec(chunk, n_ctx_tiles, tiles_per_latent_seq):
    def row(i):
        return jnp.where(i < n_ctx_tiles, 0, 1 + (i - n_ctx_tiles) // tiles_per_latent_seq)
    return pl.BlockSpec((1, 1, D_MODEL), lambda i: (row(i), 0, chunk))


def _modulated_norm(x, g_ref, shift_ref, scale_ref):
    y = x * lax.rsqrt(jnp.mean(x * x, axis=-1, keepdims=True) + EPS) * g_ref[...]
    return (y * (1.0 + scale_ref[0]) + shift_ref[0]).astype(BF16)


def _prenorm_kernel(xc_ref, xl_ref, g_ref, shift_ref, scale_ref, o_ref, *, n_ctx_tiles):
    def body(x_ref):
        o_ref[...] = _modulated_norm(x_ref[...], g_ref, shift_ref, scale_ref)

    _by_group(pl.program_id(0), n_ctx_tiles, body, (xc_ref,), (xl_ref,))


def _prenorm(x_ctx, ctx_row0, x_lat, lat_row0, g, mod3, n_ctx, n_lat, lat_seq):
    tm = 2 * ROW_TILE
    nct = n_ctx // tm
    mod = functools.partial(_mod_spec, n_ctx_tiles=nct, tiles_per_latent_seq=lat_seq // tm)
    return pl.pallas_call(
        functools.partial(_prenorm_kernel, n_ctx_tiles=nct),
        out_shape=jax.ShapeDtypeStruct((n_ctx + n_lat, D_MODEL), BF16),
        grid=((n_ctx + n_lat) // tm,),
        in_specs=_pair_specs((tm, D_MODEL), nct, ctx_row0 // tm, lat_row0 // tm)
        + [pl.BlockSpec((1, D_MODEL), lambda i: (0, 0)), mod(0), mod(1)],
        out_specs=pl.BlockSpec((tm, D_MODEL), lambda i: (i, 0)),
        compiler_params=_params("arbitrary"),
        name="prenorm",
    )(x_ctx, x_lat, g.reshape(1, D_MODEL), mod3, mod3)


def _post_kernel(xc_ref, xl_ref, r_ref, g_ref, gate_ref, *rest, n_ctx_tiles, with_next):
    if with_next:
        g_next_ref, shift_ref, scale_ref, o_ref, h_ref = rest
    else:
        (o_ref,) = rest

    def body(x_ref):
        r = r_ref[...].astype(F32)
        y = r * lax.rsqrt(jnp.mean(r * r, axis=-1, keepdims=True) + EPS) * g_ref[...]
        x = x_ref[...] + gate_ref[0] * y
        o_ref[...] = x
        if with_next:
            h_ref[...] = _modulated_norm(x, g_next_ref, shift_ref, scale_ref)

    _by_group(pl.program_id(0), n_ctx_tiles, body, (xc_ref,), (xl_ref,))


def _post(x_ctx, ctx_row0, x_lat, lat_row0, r, r_row0, g, mod3, n_ctx, n_lat, lat_seq, next_norm=None):
    tm = ROW_TILE
    nct = n_ctx // tm
    r0 = r_row0 // tm
    rows = n_ctx + n_lat
    mod = functools.partial(_mod_spec, n_ctx_tiles=nct, tiles_per_latent_seq=lat_seq // tm)
    vec = pl.BlockSpec((1, D_MODEL), lambda i: (0, 0))
    tile = pl.BlockSpec((tm, D_MODEL), lambda i: (i, 0))
    in_specs = _pair_specs((tm, D_MODEL), nct, ctx_row0 // tm, lat_row0 // tm) + [
        pl.BlockSpec((tm, D_MODEL), lambda i: (r0 + i, 0)), vec, mod(2)]
    args = [x_ctx, x_lat, r, g.reshape(1, D_MODEL), mod3]
    out_shape, out_specs = jax.ShapeDtypeStruct((rows, D_MODEL), F32), tile
    if next_norm is not None:
        g_next, mod3_next = next_norm
        in_specs += [vec, mod(0), mod(1)]
        args += [g_next.reshape(1, D_MODEL), mod3_next, mod3_next]
        out_shape, out_specs = (out_shape, jax.ShapeDtypeStruct((rows, D_MODEL), BF16)), (tile, tile)
    return pl.pallas_call(
        functools.partial(_post_kernel, n_ctx_tiles=nct, with_next=next_norm is not None),
        out_shape=out_shape,
        grid=(rows // tm,),
        in_specs=in_specs,
        out_specs=out_specs,
        compiler_params=_params("arbitrary"),
        name="post",
    )(*args)


def _mm_kernel(a_ref, b_ref, o_ref):
    o_ref[...] = _dot(a_ref[...], b_ref[...]).astype(o_ref.dtype)


def _matmul(a, b, out_dtype, tm, tn, name):
    m, k = a.shape
    n = b.shape[1]
    return pl.pallas_call(
        _mm_kernel,
        out_shape=jax.ShapeDtypeStruct((m, n), out_dtype),
        grid=(m // tm, n // tn),
        in_specs=[pl.BlockSpec((tm, k), lambda i, j: (i, 0)), pl.BlockSpec((k, tn), lambda i, j: (0, j))],
        out_specs=pl.BlockSpec((tm, tn), lambda i, j: (i, j)),
        compiler_params=_params("arbitrary", "arbitrary"),
        name=name,
    )(a, b)


def _mm_wcast_kernel(a_ref, w_ref, o_ref, wb_ref):
    @pl.when(pl.program_id(1) == 0)
    def _():
        wb_ref[...] = w_ref[...].astype(BF16)

    o_ref[...] = _dot(a_ref[...], wb_ref[...]).astype(o_ref.dtype)


def _matmul_wcast(a, w, layer, out_dtype, tm, tn, name, n_out=None, col_map=None):
    m, k = a.shape
    n = n_out or w.shape[-1]
    col = col_map or (lambda j: j)
    return pl.pallas_call(
        _mm_wcast_kernel,
        out_shape=jax.ShapeDtypeStruct((m, n), out_dtype),
        grid=(n // tn, m // tm),
        in_specs=[
            pl.BlockSpec((tm, k), lambda j, i: (i, 0)),
            pl.BlockSpec((None, k, tn), lambda j, i: (layer, 0, col(j))),
        ],
        out_specs=pl.BlockSpec((tm, tn), lambda j, i: (i, j)),
        scratch_shapes=[pltpu.VMEM((k, tn), BF16)],
        compiler_params=_params("arbitrary", "arbitrary"),
        name=name,
    )(a, w)


def _kv_kernel(h_ref, w_ref, kn_ref, k_ref, v_ref):
    acc = _dot(h_ref[...], w_ref[...])
    for hd in range(N_KV_HEADS):
        sl = slice(hd * HEAD_DIM, (hd + 1) * HEAD_DIM)
        k_ref[:, sl] = _norm_head(acc[:, sl], kn_ref[...])
    v_ref[...] = acc[:, KV_W:]


def _kv_proj(h, w_kv, layer, k_norm, tm=512):
    t = h.shape[0]
    return pl.pallas_call(
        _kv_kernel,
        out_shape=(jax.ShapeDtypeStruct((t, KV_W), F32), jax.ShapeDtypeStruct((t, KV_W), F32)),
        grid=(t // tm,),
        in_specs=[
            pl.BlockSpec((tm, D_MODEL), lambda i: (i, 0)),
            pl.BlockSpec((None, D_MODEL, W_KV), lambda i: (layer, 0, 0)),
            pl.BlockSpec((1, HEAD_DIM), lambda i: (0, 0)),
        ],
        out_specs=(pl.BlockSpec((tm, KV_W), lambda i: (i, 0)), pl.BlockSpec((tm, KV_W), lambda i: (i, 0))),
        compiler_params=_params("arbitrary"),
        name="kv_proj",
    )(h, w_kv, k_norm.reshape(1, HEAD_DIM))


def _cache_kernel(*refs):
    k_refs, v_refs, (ko_ref, vo_ref) = refs[:DEPTH], refs[DEPTH:2 * DEPTH], refs[2 * DEPTH:]
    n_blk, seq = ko_ref.shape[:2]
    for l in range(DEPTH):
        @pl.when(pl.program_id(0) == l)
        def _():
            for s in range(n_blk):
                rows = slice(s * seq, (s + 1) * seq)
                for hd in range(N_KV_HEADS):
                    sl = slice(hd * HEAD_DIM, (hd + 1) * HEAD_DIM)
                    ko_ref[s, :, hd, :] = k_refs[l][rows, sl]
                    vo_ref[s, :, hd, :] = v_refs[l][rows, sl]


def _new_cache(ks, vs, n_seq, seq, seq_blk=4):
    def layer_spec(l):
        return pl.BlockSpec((seq_blk * seq, KV_W), lambda cur, b: (jnp.where(cur == l, b, 0), 0))

    out_sds = jax.ShapeDtypeStruct((n_seq, DEPTH, seq, N_KV_HEADS, HEAD_DIM), F32)
    out_spec = pl.BlockSpec((seq_blk, None, seq, N_KV_HEADS, HEAD_DIM), lambda cur, b: (b, cur, 0, 0, 0))
    return pl.pallas_call(
        _cache_kernel,
        out_shape=(out_sds, out_sds),
        grid=(DEPTH, n_seq // seq_blk),
        in_specs=[layer_spec(l) for l in range(DEPTH)] * 2,
        out_specs=(out_spec, out_spec),
        compiler_params=_params("arbitrary", "arbitrary"),
        name="new_cache",
    )(*ks, *vs)


def _softmax_pv(s, v):
    m = jnp.max(s, axis=-1, keepdims=True)
    p = jnp.exp2(s - m)
    l = jnp.sum(p, axis=-1, keepdims=True)
    return _dot(p.astype(BF16), v) / l


def _qk(q, k):
    return lax.dot_general(q, k, (((1,), (1,)), ((), ())), preferred_element_type=F32)


def _norm_head(q, g):
    return q * lax.rsqrt(jnp.mean(q * q, axis=-1, keepdims=True) + EPS) * g


def _rope(x, cos2, sin2):
    return x * cos2 + pltpu.roll(x, HEAD_DIM // 2, 1) * sin2


def _attn_ctx_kernel(q_ref, k_ref, v_ref, ag_ref, qn_ref, o_ref):
    seq = q_ref.shape[0]
    for hd in range(N_KV_HEADS):
        kv = slice(hd * HEAD_DIM, (hd + 1) * HEAD_DIM)
        cols = [slice((hd * KV_GROUP + g) * HEAD_DIM, (hd * KV_GROUP + g + 1) * HEAD_DIM) for g in range(KV_GROUP)]
        q = jnp.concatenate([(_norm_head(q_ref[:, c].astype(F32), qn_ref[...]) * Q_SCALE).astype(BF16)
                             for c in cols], axis=0)
        o = _softmax_pv(_qk(q, k_ref[:, kv].astype(BF16)), v_ref[:, kv].astype(BF16))
        for g, c in enumerate(cols):
            o_ref[:, c] = (o[g * seq:(g + 1) * seq] * _silu(ag_ref[:, c].astype(F32))).astype(BF16)


def _attention_ctx(proj, k, v, q_norm, n_seq, seq):
    return pl.pallas_call(
        _attn_ctx_kernel,
        out_shape=jax.ShapeDtypeStruct((n_seq * seq, ATTN_W), BF16),
        grid=(n_seq,),
        in_specs=[
            pl.BlockSpec((seq, ATTN_W), lambda b: (b, OFF_Q // ATTN_W)),
            pl.BlockSpec((seq, KV_W), lambda b: (b, 0)),
            pl.BlockSpec((seq, KV_W), lambda b: (b, 0)),
            pl.BlockSpec((seq, ATTN_W), lambda b: (b, OFF_AGATE // ATTN_W)),
            pl.BlockSpec((1, HEAD_DIM), lambda b: (0, 0)),
        ],
        out_specs=pl.BlockSpec((seq, ATTN_W), lambda b: (b, 0)),
        compiler_params=_params("arbitrary"),
        name="attn_ctx",
    )(proj, k, v, proj, q_norm.reshape(1, HEAD_DIM))


def _attn_lat_kernel(q_ref, k_ref, v_ref, ck_ref, cv_ref, ag_ref, qn_ref, cosq_ref, sinq_ref,
                     cosk_ref, sink_ref, o_ref, k_s, v_s, *, seq):
    @pl.when(pl.program_id(2) == 0)
    def _():
        k_s[:seq] = _rope(k_ref[...], cosk_ref[...], sink_ref[...]).astype(BF16)
        k_s[seq:] = ck_ref[...].astype(BF16)
        v_s[:seq] = v_ref[...].astype(BF16)
        v_s[seq:] = cv_ref[...].astype(BF16)

    k = k_s[...]
    v = v_s[...]
    for g in range(KV_GROUP):
        sl = slice(g * HEAD_DIM, (g + 1) * HEAD_DIM)
        for r0 in range(0, q_ref.shape[0], ATTN_ROWS):
            rows = slice(r0, r0 + ATTN_ROWS)
            q = _norm_head(q_ref[rows, sl].astype(F32), qn_ref[...])
            q = _rope(q, cosq_ref[rows, :], sinq_ref[rows, :]) * Q_SCALE
            o = _softmax_pv(_qk(q.astype(BF16), k), v)
            o_ref[rows, sl] = (o * _silu(ag_ref[rows, sl].astype(F32))).astype(BF16)


def _attention_lat(proj, k, v, cache_k, cache_v, layer, q_norm, cos2, sin2, n_ctx, n_seq, seq, tq=1024):
    past = cache_k.shape[2]
    ck = cache_k.reshape(cache_k.shape[0], DEPTH, past, KV_W)
    cv = cache_v.reshape(cache_v.shape[0], DEPTH, past, KV_W)
    nq = seq // tq
    row0, seq0 = n_ctx // tq, n_ctx // seq
    return pl.pallas_call(
        functools.partial(_attn_lat_kernel, seq=seq),
        out_shape=jax.ShapeDtypeStruct((n_seq * seq, ATTN_W), BF16),
        grid=(n_seq, N_KV_HEADS, nq),
        in_specs=[
            pl.BlockSpec((tq, GROUP_W), lambda b, h, i: (row0 + b * nq + i, OFF_Q // GROUP_W + h)),
            pl.BlockSpec((seq, HEAD_DIM), lambda b, h, i: (seq0 + b, h)),
            pl.BlockSpec((seq, HEAD_DIM), lambda b, h, i: (seq0 + b, h)),
            pl.BlockSpec((None, None, past, HEAD_DIM), lambda b, h, i: (b, layer, 0, h)),
            pl.BlockSpec((None, None, past, HEAD_DIM), lambda b, h, i: (b, layer, 0, h)),
            pl.BlockSpec((tq, GROUP_W), lambda b, h, i: (row0 + b * nq + i, OFF_AGATE // GROUP_W + h)),
            pl.BlockSpec((1, HEAD_DIM), lambda b, h, i: (0, 0)),
            pl.BlockSpec((tq, HEAD_DIM), lambda b, h, i: (i, 0)),
            pl.BlockSpec((tq, HEAD_DIM), lambda b, h, i: (i, 0)),
            pl.BlockSpec((seq, HEAD_DIM), lambda b, h, i: (0, 0)),
            pl.BlockSpec((seq, HEAD_DIM), lambda b, h, i: (0, 0)),
        ],
        out_specs=pl.BlockSpec((tq, GROUP_W), lambda b, h, i: (b * nq + i, h)),
        scratch_shapes=[pltpu.VMEM((seq + past, HEAD_DIM), BF16), pltpu.VMEM((seq + past, HEAD_DIM), BF16)],
        compiler_params=_params("arbitrary", "arbitrary", "arbitrary"),
        name="attn_lat",
    )(proj, k, v, ck, cv, proj, q_norm.reshape(1, HEAD_DIM), cos2, sin2, cos2, sin2)


def _rope_tables(seq):
    rows = seq // GRID_W
    row = jnp.repeat(jnp.arange(rows, dtype=F32), GRID_W)
    col = jnp.tile(jnp.arange(GRID_W, dtype=F32), rows)
    inv = ROPE_THETA ** (-jnp.arange(ROPE_PAIRS_AXIS, dtype=F32) / ROPE_PAIRS_AXIS)
    ang = jnp.concatenate([row[:, None] * inv, col[:, None] * inv], axis=-1)
    cos, sin = jnp.cos(ang), jnp.sin(ang)
    return jnp.concatenate([cos, cos], axis=-1), jnp.concatenate([-sin, sin], axis=-1)


def _dft_angle(i, j, n):
    return (2.0 * math.pi / n) * ((i[:, None] * j[None, :]) % n).astype(F32)


def _cos_sin_table(rows, cols, n):
    c = jnp.arange(cols, dtype=jnp.int32)
    hi = _dft_angle(TABLE_SPLIT * jnp.arange(rows // TABLE_SPLIT, dtype=jnp.int32), c, n)
    lo = _dft_angle(jnp.arange(TABLE_SPLIT, dtype=jnp.int32), c, n)
    ch, sh = jnp.cos(hi)[:, None, :], jnp.sin(hi)[:, None, :]
    cl, sl = jnp.cos(lo)[None, :, :], jnp.sin(lo)[None, :, :]
    return (ch * cl - sh * sl).reshape(rows, cols), (sh * cl + ch * sl).reshape(rows, cols)


def _fnet_kernel(u_ref, fg_ref, csc_ref, cos_ref, sin_ref, o_ref, tc_s, ts_s, *, scale):
    @pl.when(pl.program_id(1) == 0)
    def _():
        for g in range(FNET_GROUPS):
            sl = slice(g * FNET_GROUP_W, (g + 1) * FNET_GROUP_W)
            t = _dot(u_ref[:, sl], csc_ref[...])
            tc_s[:, sl] = t[:, :FNET_GROUP_W].astype(BF16)
            ts_s[:, sl] = t[:, FNET_GROUP_W:].astype(BF16)

    y = (_dot(cos_ref[...], tc_s[...]) - _dot(sin_ref[...], ts_s[...])) * scale
    o_ref[...] = (y * _silu(fg_ref[...].astype(F32))).astype(BF16)


def _fnet_tables(seq):
    j = jnp.arange(FNET_GROUP_W, dtype=jnp.int32)
    ang = _dft_angle(j, j, FNET_GROUP_W)
    csc = jnp.concatenate([jnp.cos(ang), jnp.sin(ang)], axis=1).astype(BF16)
    cos, sin = _cos_sin_table(seq, seq, seq)
    return csc, cos.astype(BF16), sin.astype(BF16)


def _fnet(proj, tables, row_start, n_seq, seq, tr=256):
    csc, cos, sin = tables
    nr = seq // tr
    s0, r0 = row_start // seq, row_start // tr
    return pl.pallas_call(
        functools.partial(_fnet_kernel, scale=1.0 / math.sqrt(seq * FNET_GROUP_W)),
        out_shape=jax.ShapeDtypeStruct((n_seq * seq, FNET_W), BF16),
        grid=(n_seq, nr),
        in_specs=[
            pl.BlockSpec((seq, FNET_W), lambda b, r: (s0 + b, OFF_FIN // FNET_W)),
            pl.BlockSpec((tr, FNET_W), lambda b, r: (r0 + b * nr + r, OFF_FGATE // FNET_W)),
            pl.BlockSpec((FNET_GROUP_W, 2 * FNET_GROUP_W), lambda b, r: (0, 0)),
            pl.BlockSpec((tr, seq), lambda b, r: (r, 0)),
            pl.BlockSpec((tr, seq), lambda b, r: (r, 0)),
        ],
        out_specs=pl.BlockSpec((tr, FNET_W), lambda b, r: (b * nr + r, 0)),
        scratch_shapes=[pltpu.VMEM((seq, FNET_W), BF16), pltpu.VMEM((seq, FNET_W), BF16)],
        compiler_params=_params("arbitrary", "arbitrary"),
        name=f"fnet_{seq}",
    )(proj, proj, csc, cos, sin)


def _short_kernel(x_ref, w_ref, b_ref, o_ref, *, seq):
    x = x_ref[...].astype(F32)
    n = x.shape[0]
    pos = lax.broadcasted_iota(jnp.int32, x.shape, 0) % seq
    prev = jnp.where(pos == 0, 0.0, pltpu.roll(x, 1, 0))
    nxt = jnp.where(pos == seq - 1, 0.0, pltpu.roll(x, n - 1, 0))
    y = b_ref[...] + prev * w_ref[0:1, :] + x * w_ref[1:2, :] + nxt * w_ref[2:3, :]
    o_ref[...] = y.astype(BF16)


def _short_conv(proj, w, b, row_start, n_rows, seq, tr=2048, tw=512):
    width = 3 * HYENA_W
    r0 = row_start // tr
    return pl.pallas_call(
        functools.partial(_short_kernel, seq=seq),
        out_shape=jax.ShapeDtypeStruct((n_rows, width), BF16),
        grid=(n_rows // tr, width // tw),
        in_specs=[
            pl.BlockSpec((tr, tw), lambda s, c: (r0 + s, OFF_HV // tw + c)),
            pl.BlockSpec((HYENA_SHORT, tw), lambda s, c: (0, c)),
            pl.BlockSpec((1, tw), lambda s, c: (0, c)),
        ],
        out_specs=pl.BlockSpec((tr, tw), lambda s, c: (s, c)),
        compiler_params=_params("arbitrary", "arbitrary"),
        name=f"short_conv_{seq}",
    )(proj, w, b.reshape(1, width))


def _filt_mlp_kernel(feats_ref, w1_ref, b1_ref, w2_ref, b2_ref, fr_ref, o_ref):
    fr = fr_ref[...]
    h = jnp.sin(fr * (_dot_exact(feats_ref[...], w1_ref[...]) + b1_ref[...]))
    o_ref[...] = jnp.sin(fr * (_dot_exact(h, w2_ref[...]) + b2_ref[...]))


def _filt_kernel(h_ref, t_ref, w3f_ref, w3b_ref, b3f_ref, b3b_ref, dl_ref, sum_ref, dif_ref, nyq_ref):
    h = h_ref[...]
    decay = jnp.exp(-t_ref[...] * dl_ref[...])
    hf = (_dot_exact(h, w3f_ref[...]) + b3f_ref[...]) * decay
    hb = (_dot_exact(h, w3b_ref[...]) + b3b_ref[...]) * decay
    den = jnp.sum(jnp.abs(hf), axis=0, keepdims=True) + jnp.sum(jnp.abs(hb), axis=0, keepdims=True) + EPS
    hs = (hf + hb) / den
    sum_ref[...] = hs.astype(BF16)
    dif_ref[...] = ((hf - hb) / den).astype(BF16)
    sign = (1 - 2 * (lax.broadcasted_iota(jnp.int32, hs.shape, 0) % 2)).astype(F32)
    nyq_ref[...] = jnp.sum(hs * sign, axis=0, keepdims=True)


def _hyena_filters(seq, w1, b1, w2, b2, w3, b3, freq, tw=512):
    t = jnp.arange(seq, dtype=F32)[:, None] / seq
    bands = jnp.arange(1, HYENA_BANDS + 1, dtype=F32)[None, :]
    feats = jnp.concatenate([t, jnp.cos(2 * math.pi * t * bands), jnp.sin(2 * math.pi * t * bands),
                             jnp.zeros((seq, HYENA_POS_PAD - HYENA_POS_DIM), F32)], axis=-1)
    w1p = jnp.concatenate([w1, jnp.zeros((HYENA_POS_PAD - HYENA_POS_DIM, HYENA_FFN_W), F32)], axis=0)
    deltas = jnp.abs(jnp.linspace(HYENA_MIN_DECAY, HYENA_MAX_DECAY, HYENA_W, dtype=F32))[None, :]
    hidden = pl.pallas_call(
        _filt_mlp_kernel,
        out_shape=jax.ShapeDtypeStruct((seq, HYENA_FFN_W), F32),
        name=f"hyena_filter_mlp_{seq}",
    )(feats, w1p, b1.reshape(1, -1), w2, b2.reshape(1, -1), freq.reshape(1, -1))
    nw = HYENA_W // tw
    small = lambda shape: pl.BlockSpec(shape, lambda o, c: (0, 0))
    out_sds = jax.ShapeDtypeStruct((seq, HYENA_ORDER * HYENA_W), BF16)
    fwd = lambda o, c: (0, (2 * o) * nw + c)
    bwd = lambda o, c: (0, (2 * o + 1) * nw + c)
    b3r = b3.reshape(1, -1)
    return pl.pallas_call(
        _filt_kernel,
        out_shape=(out_sds, out_sds, jax.ShapeDtypeStruct((1, HYENA_ORDER * HYENA_W), F32)),
        grid=(HYENA_ORDER, nw),
        in_specs=[
            small((seq, HYENA_FFN_W)), small((seq, 1)),
            pl.BlockSpec((HYENA_FFN_W, tw), fwd), pl.BlockSpec((HYENA_FFN_W, tw), bwd),
            pl.BlockSpec((1, tw), fwd), pl.BlockSpec((1, tw), bwd),
            pl.BlockSpec((1, tw), lambda o, c: (0, c)),
        ],
        out_specs=(pl.BlockSpec((seq, tw), lambda o, c: (0, o * nw + c)),
                   pl.BlockSpec((seq, tw), lambda o, c: (0, o * nw + c)),
                   pl.BlockSpec((1, tw), lambda o, c: (0, o * nw + c))),
        compiler_params=_params("arbitrary", "arbitrary"),
        name=f"hyena_filters_{seq}",
    )(hidden, t, w3, w3, b3r, b3r, deltas)


def _conv_tables(seq):
    i = jnp.arange(seq, dtype=jnp.int32)
    nyq = (1 - 2 * (i % 2)).astype(F32)
    cos, sin = _cos_sin_table(seq, seq, 2 * seq)
    sin_f = jnp.where(i[:, None] == 0, nyq[None, :], sin)
    sin_t = jnp.where(i[None, :] == 0, nyq[:, None], sin)
    return cos.astype(BF16), sin_f.astype(BF16), sin_t.astype(BF16)


def _filter_bins(uc_ref, us_ref, nyq_ref, first_chunk):
    gr = uc_ref[...]
    packed = jnp.logical_and(lax.broadcasted_iota(jnp.int32, gr.shape, 0) == 0, first_chunk)
    gi = jnp.where(packed, 0.0, us_ref[...])
    gn = jnp.where(packed, nyq_ref[...], gr)
    wgt = jnp.where(packed, 1.0, 2.0)
    return gr * wgt, gi * wgt, gn * wgt


def _spectrum_product(cos_ref, sin_ref, z, bins):
    gr, gi, gn = bins
    zr = _dot(cos_ref[...], z)
    zi = _dot(sin_ref[...], z)
    return (zr * gr - zi * gi).astype(BF16), (zr * gi + zi * gn).astype(BF16)


def _conv_epilogue(y, z, bias_ref, xm, gate, n):
    y = (y * (1.0 / n) + z.astype(F32) * bias_ref[...]) * xm.astype(F32)
    if gate is not None:
        y = y * _silu(gate.astype(F32))
    return y.astype(BF16)


def _lconv_short_kernel(*refs, seq, n_seq_blk, gated):
    refs = list(refs)
    z_ref, xm_ref = refs[:2]
    gate_ref = refs[2] if gated else None
    bias_ref, cos_ref, sin_ref, sin_t_ref, uc_ref, us_ref, nyq_ref, o_ref = refs[2 + gated:]
    bins = _filter_bins(uc_ref, us_ref, nyq_ref, True)
    for s in range(n_seq_blk):
        rows = slice(s * seq, (s + 1) * seq)
        z = z_ref[rows, :]
        pr, pi = _spectrum_product(cos_ref, sin_ref, z, bins)
        y = _dot(cos_ref[...], pr) + _dot(sin_t_ref[...], pi)
        o_ref[rows, :] = _conv_epilogue(y, z, bias_ref, xm_ref[rows, :],
                                        gate_ref[rows, :] if gated else None, 2 * seq)


def _lconv_long_kernel(*refs, seq, chunk, gated):
    refs = list(refs)
    z_ref, xm_ref = refs[:2]
    gate_ref = refs[2] if gated else None
    bias_ref, cos_ref, sin_ref, sin_t_ref, uc_ref, us_ref, nyq_ref, o_ref, pr_s, pi_s = refs[2 + gated:]
    n_chunks = seq // chunk
    p = pl.program_id(2)

    @pl.when(p < n_chunks)
    def _():
        bins = _filter_bins(uc_ref, us_ref, nyq_ref, p == 0)
        pr, pi = _spectrum_product(cos_ref, sin_ref, z_ref[...], bins)
        rows = pl.ds(pl.multiple_of(p * chunk, chunk), chunk)
        pr_s[rows, :] = pr
        pi_s[rows, :] = pi

    @pl.when(p >= n_chunks)
    def _():
        y = _dot(cos_ref[...], pr_s[...]) + _dot(sin_t_ref[...], pi_s[...])
        rows = pl.ds(pl.multiple_of((p - n_chunks) * chunk, chunk), chunk)
        o_ref[...] = _conv_epilogue(y, z_ref[rows, :], bias_ref, xm_ref[...],
                                    gate_ref[...] if gated else None, 2 * seq)


def _long_conv(z, z_col, xm, xm_col, gate, gate_row0, bias, idx, order, tables, spectrum,
               n_seq, seq, tw, fc, seq_blk):
    nw, nf = HYENA_W // tw, seq // fc
    gated = gate is not None
    out_shape = jax.ShapeDtypeStruct((n_seq * seq, HYENA_W), BF16)
    bias_spec = pl.BlockSpec((None, 1, tw), lambda s, w, p: (idx, 0, w))
    nyq_spec = pl.BlockSpec((1, tw), lambda s, w, p: (0, order * nw + w))
    if nf == 1:
        tr = seq_blk * seq
        tok = lambda col, r0=0: pl.BlockSpec((tr, tw), lambda s, w, p: (r0 + s, col * nw + w))
        taps = pl.BlockSpec((seq, tw), lambda s, w, p: (0, order * nw + w))
        table = pl.BlockSpec((seq, seq), lambda s, w, p: (0, 0))
        in_specs = [tok(z_col), tok(xm_col)] + ([tok(OFF_HGATE // HYENA_W, gate_row0 // tr)] if gated else [])
        in_specs += [bias_spec, table, table, table, taps, taps, nyq_spec]
        kern = functools.partial(_lconv_short_kernel, seq=seq, n_seq_blk=seq_blk, gated=gated)
        grid, scratch = (n_seq // seq_blk, nw, 1), []
        out_spec = pl.BlockSpec((tr, tw), lambda s, w, p: (s, w))
    else:
        assert seq_blk == 1
        fwd_chunk = lambda p: jnp.minimum(p, nf - 1)
        out_chunk = lambda p: jnp.maximum(p - nf, 0)
        rows = lambda col, r0=0: pl.BlockSpec((fc, tw), lambda s, w, p: (r0 + s * nf + out_chunk(p), col * nw + w))
        taps = pl.BlockSpec((fc, tw), lambda s, w, p: (fwd_chunk(p), order * nw + w))
        in_specs = [pl.BlockSpec((seq, tw), lambda s, w, p: (s, z_col * nw + w)), rows(xm_col)]
        in_specs += [rows(OFF_HGATE // HYENA_W, gate_row0 // fc)] if gated else []
        in_specs += [
            bias_spec,
            pl.BlockSpec((fc, seq), lambda s, w, p: (p % nf, 0)),
            pl.BlockSpec((fc, seq), lambda s, w, p: (fwd_chunk(p), 0)),
            pl.BlockSpec((fc, seq), lambda s, w, p: (out_chunk(p), 0)),
            taps, taps, nyq_spec,
        ]
        kern = functools.partial(_lconv_long_kernel, seq=seq, chunk=fc, gated=gated)
        grid, scratch = (n_seq, nw, 2 * nf), [pltpu.VMEM((seq, tw), BF16), pltpu.VMEM((seq, tw), BF16)]
        out_spec = pl.BlockSpec((fc, tw), lambda s, w, p: (s * nf + out_chunk(p), w))
    args = [z, xm] + ([gate] if gated else []) + [bias, *tables, *spectrum]
    return pl.pallas_call(
        kern,
        out_shape=out_shape,
        grid=grid,
        in_specs=in_specs,
        out_specs=out_spec,
        scratch_shapes=scratch,
        compiler_params=_params("arbitrary", "arbitrary", "arbitrary"),
        name=f"long_conv_{seq}_{order}",
    )(*args)


def _merge_kernel(ac_ref, al_ref, fc_ref, fl_ref, hc_ref, hl_ref, wa_ref, wf_ref, wh_ref,
                  ga_ref, gf_ref, gh_ref, o_ref, wab_ref, wfb_ref, whb_ref, *, n_ctx_tiles):
    i = pl.program_id(1)

    @pl.when(i == 0)
    def _():
        wab_ref[...] = wa_ref[...].astype(BF16)
        wfb_ref[...] = wf_ref[...].astype(BF16)
        whb_ref[...] = wh_ref[...].astype(BF16)

    def body(a_ref, f_ref, h_ref):
        m = jax.nn.sigmoid(ga_ref[...].astype(F32)) * _dot(a_ref[...], wab_ref[...])
        m = m + jax.nn.sigmoid(gf_ref[...].astype(F32)) * _dot(f_ref[...], wfb_ref[...])
        m = m + jax.nn.sigmoid(gh_ref[...].astype(F32)) * _dot(h_ref[...], whb_ref[...])
        o_ref[...] = m.astype(BF16)

    _by_group(i, n_ctx_tiles, body, (ac_ref, fc_ref, hc_ref), (al_ref, fl_ref, hl_ref))


def _merge(attn, fnet, hy, w_a, w_f, w_h, layer, proj, n_ctx, tm=512, tn=1024):
    t = proj.shape[0]
    nct = n_ctx // tm
    act = lambda w: _pair_specs((tm, w), nct, inner=True)
    wgt = lambda k: pl.BlockSpec((None, k, tn), lambda j, i: (layer, 0, j), pipeline_mode=pl.Buffered(1))
    gate = lambda off: pl.BlockSpec((tm, tn), lambda j, i: (i, off // tn + j))
    return pl.pallas_call(
        functools.partial(_merge_kernel, n_ctx_tiles=nct),
        out_shape=jax.ShapeDtypeStruct((t, D_MODEL), BF16),
        grid=(D_MODEL // tn, t // tm),
        in_specs=act(ATTN_W) + act(FNET_W) + act(HYENA_W) + [wgt(ATTN_W), wgt(FNET_W), wgt(HYENA_W),
                                                             gate(OFF_GA), gate(OFF_GF), gate(OFF_GH)],
        out_specs=pl.BlockSpec((tm, tn), lambda j, i: (i, j)),
        scratch_shapes=[pltpu.VMEM((ATTN_W, tn), BF16), pltpu.VMEM((FNET_W, tn), BF16),
                        pltpu.VMEM((HYENA_W, tn), BF16)],
        compiler_params=_params("arbitrary", "arbitrary"),
        name="merge",
    )(*attn, *fnet, *hy, w_a, w_f, w_h, proj, proj, proj)


def kernel(x_prompt, x_sample, cache_k, cache_v, c, c_ctx, w_mod, b_mod, g_pre, w_in, q_norm, k_norm, hy_short_w, hy_short_b, hy_ffn_w1, hy_ffn_b1, hy_ffn_w2, hy_ffn_b2, hy_ffn_w3, hy_ffn_b3, hy_sin_freq, hy_bias, w_attn_o, w_fnet_o, w_hy_o, w_out, g_post):
    nb, seq, _ = x_prompt.shape
    db, dseq, _ = x_sample.shape
    assert 1 + db <= MOD_ROWS
    n_ctx, n_lat = nb * seq, db * dseq

    cond = jnp.concatenate([c_ctx[None, :], c, jnp.zeros((MOD_ROWS - 1 - db, D_MODEL), F32)], axis=0)
    mod = _modulation(cond, w_mod, b_mod)

    w_kv_b = w_in[:, :, W_OFF_K:W_OFF_K + W_KV].astype(BF16)
    cos2, sin2 = _rope_tables(dseq)
    fnet_tables = {s: _fnet_tables(s) for s in (seq, dseq)}
    conv_tables = {s: _conv_tables(s) for s in (seq, dseq)}
    bias = hy_bias.reshape(DEPTH * HYENA_ORDER, 1, HYENA_W)
    conv_tiles = {seq: (HYENA_W, seq, 4), dseq: (HYENA_W, 512, 1)}
    proj_tm, proj_tn = 512, 1024
    skip_kv = lambda j: jnp.where(j >= W_OFF_K // proj_tn, j + W_KV // proj_tn, j)

    xs = (x_prompt.reshape(n_ctx, D_MODEL), 0, x_sample.reshape(n_lat, D_MODEL), 0)
    new_k, new_v = [], []
    mod3s = [mod[l].reshape(MOD_ROWS, 1, 3 * D_MODEL) for l in range(DEPTH)]
    h = _prenorm(*xs, g_pre[0], mod3s[0], n_ctx, n_lat, dseq)
    for l in range(DEPTH):
        mod3 = mod3s[l]
        proj = _matmul_wcast(h, w_in, l, BF16, proj_tm, proj_tn, "in_proj", n_out=PROJ_W, col_map=skip_kv)
        k, v = _kv_proj(h, w_kv_b, l, k_norm[l])
        new_k.append(k)
        new_v.append(v)

        attn = (_attention_ctx(proj, k, v, q_norm[l], nb, seq),
                _attention_lat(proj, k, v, cache_k, cache_v, l, q_norm[l], cos2, sin2, n_ctx, db, dseq))

        fnet, hy = [], []
        for row_start, n_seq, s in ((0, nb, seq), (n_ctx, db, dseq)):
            fnet.append(_fnet(proj, fnet_tables[s], row_start, n_seq, s))
            u = _short_conv(proj, hy_short_w[l], hy_short_b[l], row_start, n_seq * s, s)
            tap_sum, tap_dif, tap_nyq = _hyena_filters(s, hy_ffn_w1[l], hy_ffn_b1[l], hy_ffn_w2[l], hy_ffn_b2[l],
                                                       hy_ffn_w3[l], hy_ffn_b3[l], hy_sin_freq[l])
            cos, sin_f, _ = conv_tables[s]
            spec_tm = min(1024, s)
            spectrum = (_matmul(cos, tap_sum, F32, spec_tm, 1024, f"filter_spectrum_cos_{s}"),
                        _matmul(sin_f, tap_dif, F32, spec_tm, 1024, f"filter_spectrum_sin_{s}"), tap_nyq)
            tw, fc, blk = conv_tiles[s]
            z1 = _long_conv(u, 0, u, 1, None, 0, bias, l * HYENA_ORDER, 0, conv_tables[s],
                            spectrum, n_seq, s, tw, fc, blk)
            hy.append(_long_conv(z1, 0, u, 2, proj, row_start, bias, l * HYENA_ORDER + 1, 1, conv_tables[s],
                                 spectrum, n_seq, s, tw, fc, blk))

        merged = _merge(attn, fnet, hy, w_attn_o, w_fnet_o, w_hy_o, l, proj, n_ctx)
        r = _matmul_wcast(merged, w_out, l, BF16, proj_tm, proj_tn, "out_proj")
        if l < DEPTH - 1:
            x_new, h = _post(*xs, r, 0, g_post[l], mod3, n_ctx, n_lat, dseq, next_norm=(g_pre[l + 1], mod3s[l + 1]))
            xs = (x_new, 0, x_new, n_ctx)
        else:
            y_prompt = _post(*xs, r, 0, g_post[l], mod3, n_ctx, 0, dseq)
            y_sample = _post(*xs, r, n_ctx, g_post[l], mod3, 0, n_lat, dseq)

    cache_k_new, cache_v_new = _new_cache(new_k, new_v, nb, seq)
    return (y_prompt.reshape(nb, seq, D_MODEL), y_sample.reshape(db, dseq, D_MODEL), cache_k_new, cache_v_new)
```

```python
import functools
import math

import jax
import jax.numpy as jnp
from jax import lax
from jax.experimental import pallas as pl
from jax.experimental.pallas import tpu as pltpu

F32 = jnp.float32
BF16 = jnp.bfloat16

D_MODEL = 4096
DEPTH = 2
GRID_W = 64
HEAD_DIM = 128
N_HEADS = 16
N_KV_HEADS = 4
KV_GROUP = N_HEADS // N_KV_HEADS
ATTN_W = N_HEADS * HEAD_DIM
KV_W = N_KV_HEADS * HEAD_DIM
GROUP_W = KV_GROUP * HEAD_DIM
ATTN_SCALE = HEAD_DIM ** -0.5
Q_SCALE = ATTN_SCALE * math.log2(math.e)
ROPE_THETA = 10000.0
ROPE_PAIRS_AXIS = HEAD_DIM // 4
FNET_W = D_MODEL // 4
FNET_GROUPS = 4
FNET_GROUP_W = FNET_W // FNET_GROUPS
HYENA_W = D_MODEL // 4
HYENA_ORDER = 2
HYENA_SHORT = 3
HYENA_BANDS = 16
HYENA_POS_DIM = 1 + 2 * HYENA_BANDS
HYENA_POS_PAD = 128
HYENA_FFN_W = 64
HYENA_MIN_DECAY = math.log(1e-2) / 0.3
HYENA_MAX_DECAY = math.log(1e-2) / 1.5
EPS = 1e-6

W_OFF_K = ATTN_W
W_KV = 2 * KV_W
OFF_Q = 0
OFF_AGATE = OFF_Q + ATTN_W
OFF_FIN = OFF_AGATE + ATTN_W
OFF_FGATE = OFF_FIN + FNET_W
OFF_HV = OFF_FGATE + FNET_W
OFF_HX1 = OFF_HV + HYENA_W
OFF_HX2 = OFF_HX1 + HYENA_W
OFF_HGATE = OFF_HX2 + HYENA_W
OFF_GA = OFF_HGATE + HYENA_W
OFF_GF = OFF_GA + D_MODEL
OFF_GH = OFF_GF + D_MODEL
PROJ_W = OFF_GH + D_MODEL

MOD_ROWS = 8
ROW_TILE = 256
ATTN_ROWS = 256
TABLE_SPLIT = 64
VMEM_LIMIT = 56 * 1024 * 1024


def _params(*sem):
    return pltpu.CompilerParams(dimension_semantics=sem, vmem_limit_bytes=VMEM_LIMIT)


def _silu(x):
    return x * jax.nn.sigmoid(x)


def _dot(a, b):
    return jnp.dot(a, b, preferred_element_type=F32)


def _dot_exact(a, b):
    return jnp.dot(a, b, preferred_element_type=F32, precision=lax.Precision.HIGHEST)


def _pair_specs(block, n_ctx_tiles, ctx_tile0=0, lat_tile0=0, col=0, inner=False):
    def ctx(i):
        return (ctx_tile0 + jnp.clip(i, 0, max(n_ctx_tiles - 1, 0)), col)

    def lat(i):
        return (lat_tile0 + jnp.maximum(i - n_ctx_tiles, 0), col)

    if inner:
        return [pl.BlockSpec(block, lambda j, i: ctx(i)), pl.BlockSpec(block, lambda j, i: lat(i))]
    return [pl.BlockSpec(block, lambda i: ctx(i)), pl.BlockSpec(block, lambda i: lat(i))]


def _by_group(i, n_ctx_tiles, body, ctx_refs, lat_refs):
    @pl.when(i < n_ctx_tiles)
    def _():
        body(*ctx_refs)

    @pl.when(i >= n_ctx_tiles)
    def _():
        body(*lat_refs)


def _mod_kernel(c_ref, w_ref, b_ref, o_ref):
    s = _silu(c_ref[...]).astype(BF16)
    o_ref[0] = _dot(s, w_ref[0].astype(BF16)) + b_ref[0]


def _modulation(cond, w_mod, b_mod, tn=512):
    n = w_mod.shape[2]
    return pl.pallas_call(
        _mod_kernel,
        out_shape=jax.ShapeDtypeStruct((DEPTH, MOD_ROWS, n), F32),
        grid=(DEPTH, n // tn),
        in_specs=[
            pl.BlockSpec((MOD_ROWS, D_MODEL), lambda l, j: (0, 0)),
            pl.BlockSpec((1, D_MODEL, tn), lambda l, j: (l, 0, j)),
            pl.BlockSpec((1, 1, tn), lambda l, j: (l, 0, j)),
        ],
        out_specs=pl.BlockSpec((1, MOD_ROWS, tn), lambda l, j: (l, 0, j)),
        compiler_params=_params("arbitrary", "arbitrary"),
        name="modulation",
    )(cond, w_mod, b_mod.reshape(DEPTH, 1, n))


def _mod_spec(chunk, n_ctx_tiles, tiles_per_latent_seq):
    def row(i):
        return jnp.where(i < n_ctx_tiles, 0, 1 + (i - n_ctx_tiles) // tiles_per_latent_seq)
    return pl.BlockSpec((1, 1, D_MODEL), lambda i: (row(i), 0, chunk))


def _modulated_norm(x, g_ref, shift_ref, scale_ref):
    y = x * lax.rsqrt(jnp.mean(x * x, axis=-1, keepdims=True) + EPS) * g_ref[...]
    return (y * (1.0 + scale_ref[0]) + shift_ref[0]).astype(BF16)


def _prenorm_kernel(xc_ref, xl_ref, g_ref, shift_ref, scale_ref, o_ref, *, n_ctx_tiles):
    def body(x_ref):
        o_ref[...] = _modulated_norm(x_ref[...], g_ref, shift_ref, scale_ref)

    _by_group(pl.program_id(0), n_ctx_tiles, body, (xc_ref,), (xl_ref,))


def _prenorm(x_ctx, ctx_row0, x_lat, lat_row0, g, mod3, n_ctx, n_lat, lat_seq):
    tm = 2 * ROW_TILE
    nct = n_ctx // tm
    mod = functools.partial(_mod_spec, n_ctx_tiles=nct, tiles_per_latent_seq=lat_seq // tm)
    return pl.pallas_call(
        functools.partial(_prenorm_kernel, n_ctx_tiles=nct),
        out_shape=jax.ShapeDtypeStruct((n_ctx + n_lat, D_MODEL), BF16),
        grid=((n_ctx + n_lat) // tm,),
        in_specs=_pair_specs((tm, D_MODEL), nct, ctx_row0 // tm, lat_row0 // tm)
        + [pl.BlockSpec((1, D_MODEL), lambda i: (0, 0)), mod(0), mod(1)],
        out_specs=pl.BlockSpec((tm, D_MODEL), lambda i: (i, 0)),
        compiler_params=_params("arbitrary"),
        name="prenorm",
    )(x_ctx, x_lat, g.reshape(1, D_MODEL), mod3, mod3)


def _post_kernel(xc_ref, xl_ref, r_ref, g_ref, gate_ref, *rest, n_ctx_tiles, with_next):
    if with_next:
        g_next_ref, shift_ref, scale_ref, o_ref, h_ref = rest
    else:
        (o_ref,) = rest

    def body(x_ref):
        r = r_ref[...].astype(F32)
        y = r * lax.rsqrt(jnp.mean(r * r, axis=-1, keepdims=True) + EPS) * g_ref[...]
        x = x_ref[...] + gate_ref[0] * y
        o_ref[...] = x
        if with_next:
            h_ref[...] = _modulated_norm(x, g_next_ref, shift_ref, scale_ref)

    _by_group(pl.program_id(0), n_ctx_tiles, body, (xc_ref,), (xl_ref,))


def _post(x_ctx, ctx_row0, x_lat, lat_row0, r, r_row0, g, mod3, n_ctx, n_lat, lat_seq, next_norm=None):
    tm = ROW_TILE
    nct = n_ctx // tm
    r0 = r_row0 // tm
    rows = n_ctx + n_lat
    mod = functools.partial(_mod_spec, n_ctx_tiles=nct, tiles_per_latent_seq=lat_seq // tm)
    vec = pl.BlockSpec((1, D_MODEL), lambda i: (0, 0))
    tile = pl.BlockSpec((tm, D_MODEL), lambda i: (i, 0))
    in_specs = _pair_specs((tm, D_MODEL), nct, ctx_row0 // tm, lat_row0 // tm) + [
        pl.BlockSpec((tm, D_MODEL), lambda i: (r0 + i, 0)), vec, mod(2)]
    args = [x_ctx, x_lat, r, g.reshape(1, D_MODEL), mod3]
    out_shape, out_specs = jax.ShapeDtypeStruct((rows, D_MODEL), F32), tile
    if next_norm is not None:
        g_next, mod3_next = next_norm
        in_specs += [vec, mod(0), mod(1)]
        args += [g_next.reshape(1, D_MODEL), mod3_next, mod3_next]
        out_shape, out_specs = (out_shape, jax.ShapeDtypeStruct((rows, D_MODEL), BF16)), (tile, tile)
    return pl.pallas_call(
        functools.partial(_post_kernel, n_ctx_tiles=nct, with_next=next_norm is not None),
        out_shape=out_shape,
        grid=(rows // tm,),
        in_specs=in_specs,
        out_specs=out_specs,
        compiler_params=_params("arbitrary"),
        name="post",
    )(*args)


def _mm_kernel(a_ref, b_ref, o_ref):
    o_ref[...] = _dot(a_ref[...], b_ref[...]).astype(o_ref.dtype)


def _matmul(a, b, out_dtype, tm, tn, name):
    m, k = a.shape
    n = b.shape[1]
    return pl.pallas_call(
        _mm_kernel,
        out_shape=jax.ShapeDtypeStruct((m, n), out_dtype),
        grid=(m // tm, n // tn),
        in_specs=[pl.BlockSpec((tm, k), lambda i, j: (i, 0)), pl.BlockSpec((k, tn), lambda i, j: (0, j))],
        out_specs=pl.BlockSpec((tm, tn), lambda i, j: (i, j)),
        compiler_params=_params("arbitrary", "arbitrary"),
        name=name,
    )(a, b)


def _mm_wcast_kernel(a_ref, w_ref, o_ref, wb_ref):
    @pl.when(pl.program_id(1) == 0)
    def _():
        wb_ref[...] = w_ref[...].astype(BF16)

    o_ref[...] = _dot(a_ref[...], wb_ref[...]).astype(o_ref.dtype)


def _matmul_wcast(a, w, layer, out_dtype, tm, tn, name, n_out=None, col_map=None):
    m, k = a.shape
    n = n_out or w.shape[-1]
    col = col_map or (lambda j: j)
    return pl.pallas_call(
        _mm_wcast_kernel,
        out_shape=jax.ShapeDtypeStruct((m, n), out_dtype),
        grid=(n // tn, m // tm),
        in_specs=[
            pl.BlockSpec((tm, k), lambda j, i: (i, 0)),
            pl.BlockSpec((None, k, tn), lambda j, i: (layer, 0, col(j))),
        ],
        out_specs=pl.BlockSpec((tm, tn), lambda j, i: (i, j)),
        scratch_shapes=[pltpu.VMEM((k, tn), BF16)],
        compiler_params=_params("arbitrary", "arbitrary"),
        name=name,
    )(a, w)


def _kv_kernel(h_ref, w_ref, kn_ref, k_ref, v_ref, wb_ref):
    @pl.when(pl.program_id(0) == 0)
    def _():
        wb_ref[...] = w_ref[...].astype(BF16)

    acc = _dot(h_ref[...], wb_ref[...])
    for hd in range(N_KV_HEADS):
        sl = slice(hd * HEAD_DIM, (hd + 1) * HEAD_DIM)
        k_ref[:, sl] = _norm_head(acc[:, sl], kn_ref[...])
    v_ref[...] = acc[:, KV_W:]


def _kv_proj(h, w_in, layer, k_norm, tm=512):
    t = h.shape[0]
    return pl.pallas_call(
        _kv_kernel,
        out_shape=(jax.ShapeDtypeStruct((t, KV_W), F32), jax.ShapeDtypeStruct((t, KV_W), F32)),
        grid=(t // tm,),
        in_specs=[
            pl.BlockSpec((tm, D_MODEL), lambda i: (i, 0)),
            pl.BlockSpec((None, D_MODEL, W_KV), lambda i: (layer, 0, W_OFF_K // W_KV), pipeline_mode=pl.Buffered(1)),
            pl.BlockSpec((1, HEAD_DIM), lambda i: (0, 0)),
        ],
        out_specs=(pl.BlockSpec((tm, KV_W), lambda i: (i, 0)), pl.BlockSpec((tm, KV_W), lambda i: (i, 0))),
        scratch_shapes=[pltpu.VMEM((D_MODEL, W_KV), BF16)],
        compiler_params=_params("arbitrary"),
        name="kv_proj",
    )(h, w_in, k_norm.reshape(1, HEAD_DIM))


def _cache_kernel(*refs):
    k_refs, v_refs, (ko_ref, vo_ref) = refs[:DEPTH], refs[DEPTH:2 * DEPTH], refs[2 * DEPTH:]
    n_blk, seq = ko_ref.shape[:2]
    for l in range(DEPTH):
        @pl.when(pl.program_id(0) == l)
        def _():
            for s in range(n_blk):
                rows = slice(s * seq, (s + 1) * seq)
                for hd in range(N_KV_HEADS):
                    sl = slice(hd * HEAD_DIM, (hd + 1) * HEAD_DIM)
                    ko_ref[s, :, hd, :] = k_refs[l][rows, sl]
                    vo_ref[s, :, hd, :] = v_refs[l][rows, sl]


def _new_cache(ks, vs, n_seq, seq, seq_blk=4):
    def layer_spec(l):
        return pl.BlockSpec((seq_blk * seq, KV_W), lambda cur, b: (jnp.where(cur == l, b, 0), 0))

    out_sds = jax.ShapeDtypeStruct((n_seq, DEPTH, seq, N_KV_HEADS, HEAD_DIM), F32)
    out_spec = pl.BlockSpec((seq_blk, None, seq, N_KV_HEADS, HEAD_DIM), lambda cur, b: (b, cur, 0, 0, 0))
    return pl.pallas_call(
        _cache_kernel,
        out_shape=(out_sds, out_sds),
        grid=(DEPTH, n_seq // seq_blk),
        in_specs=[layer_spec(l) for l in range(DEPTH)] * 2,
        out_specs=(out_spec, out_spec),
        compiler_params=_params("arbitrary", "arbitrary"),
        name="new_cache",
    )(*ks, *vs)


def _softmax_pv(s, v):
    m = jnp.max(s, axis=-1, keepdims=True)
    p = jnp.exp2(s - m)
    l = jnp.sum(p, axis=-1, keepdims=True)
    return _dot(p.astype(BF16), v) / l


def _qk(q, k):
    return lax.dot_general(q, k, (((1,), (1,)), ((), ())), preferred_element_type=F32)


def _norm_head(q, g):
    return q * lax.rsqrt(jnp.mean(q * q, axis=-1, keepdims=True) + EPS) * g


def _rope(x, cos2, sin2):
    return x * cos2 + pltpu.roll(x, HEAD_DIM // 2, 1) * sin2


def _attn_ctx_kernel(q_ref, k_ref, v_ref, ag_ref, qn_ref, o_ref):
    seq = q_ref.shape[0]
    for hd in range(N_KV_HEADS):
        kv = slice(hd * HEAD_DIM, (hd + 1) * HEAD_DIM)
        cols = [slice((hd * KV_GROUP + g) * HEAD_DIM, (hd * KV_GROUP + g + 1) * HEAD_DIM) for g in range(KV_GROUP)]
        q = jnp.concatenate([(_norm_head(q_ref[:, c].astype(F32), qn_ref[...]) * Q_SCALE).astype(BF16)
                             for c in cols], axis=0)
        o = _softmax_pv(_qk(q, k_ref[:, kv].astype(BF16)), v_ref[:, kv].astype(BF16))
        for g, c in enumerate(cols):
            o_ref[:, c] = (o[g * seq:(g + 1) * seq] * _silu(ag_ref[:, c].astype(F32))).astype(BF16)


def _attention_ctx(proj, k, v, q_norm, n_seq, seq):
    return pl.pallas_call(
        _attn_ctx_kernel,
        out_shape=jax.ShapeDtypeStruct((n_seq * seq, ATTN_W), BF16),
        grid=(n_seq,),
        in_specs=[
            pl.BlockSpec((seq, ATTN_W), lambda b: (b, OFF_Q // ATTN_W)),
            pl.BlockSpec((seq, KV_W), lambda b: (b, 0)),
            pl.BlockSpec((seq, KV_W), lambda b: (b, 0)),
            pl.BlockSpec((seq, ATTN_W), lambda b: (b, OFF_AGATE // ATTN_W)),
            pl.BlockSpec((1, HEAD_DIM), lambda b: (0, 0)),
        ],
        out_specs=pl.BlockSpec((seq, ATTN_W), lambda b: (b, 0)),
        compiler_params=_params("arbitrary"),
        name="attn_ctx",
    )(proj, k, v, proj, q_norm.reshape(1, HEAD_DIM))


def _attn_lat_kernel(q_ref, k_ref, v_ref, ck_ref, cv_ref, ag_ref, qn_ref, cosq_ref, sinq_ref,
                     cosk_ref, sink_ref, o_ref, k_s, v_s, *, seq):
    @pl.when(pl.program_id(2) == 0)
    def _():
        k_s[:seq] = _rope(k_ref[...], cosk_ref[...], sink_ref[...]).astype(BF16)
        k_s[seq:] = ck_ref[...].astype(BF16)
        v_s[:seq] = v_ref[...].astype(BF16)
        v_s[seq:] = cv_ref[...].astype(BF16)

    k = k_s[...]
    v = v_s[...]
    for g in range(KV_GROUP):
        sl = slice(g * HEAD_DIM, (g + 1) * HEAD_DIM)
        for r0 in range(0, q_ref.shape[0], ATTN_ROWS):
            rows = slice(r0, r0 + ATTN_ROWS)
            q = _norm_head(q_ref[rows, sl].astype(F32), qn_ref[...])
            q = _rope(q, cosq_ref[rows, :], sinq_ref[rows, :]) * Q_SCALE
            o = _softmax_pv(_qk(q.astype(BF16), k), v)
            o_ref[rows, sl] = (o * _silu(ag_ref[rows, sl].astype(F32))).astype(BF16)


def _attention_lat(proj, k, v, cache_k, cache_v, layer, q_norm, cos2, sin2, n_ctx, n_seq, seq, tq=1024):
    past = cache_k.shape[2]
    ck = cache_k.reshape(cache_k.shape[0], DEPTH, past, KV_W)
    cv = cache_v.reshape(cache_v.shape[0], DEPTH, past, KV_W)
    nq = seq // tq
    row0, seq0 = n_ctx // tq, n_ctx // seq
    return pl.pallas_call(
        functools.partial(_attn_lat_kernel, seq=seq),
        out_shape=jax.ShapeDtypeStruct((n_seq * seq, ATTN_W), BF16),
        grid=(n_seq, N_KV_HEADS, nq),
        in_specs=[
            pl.BlockSpec((tq, GROUP_W), lambda b, h, i: (row0 + b * nq + i, OFF_Q // GROUP_W + h)),
            pl.BlockSpec((seq, HEAD_DIM), lambda b, h, i: (seq0 + b, h)),
            pl.BlockSpec((seq, HEAD_DIM), lambda b, h, i: (seq0 + b, h)),
            pl.BlockSpec((None, None, past, HEAD_DIM), lambda b, h, i: (b, layer, 0, h)),
            pl.BlockSpec((None, None, past, HEAD_DIM), lambda b, h, i: (b, layer, 0, h)),
            pl.BlockSpec((tq, GROUP_W), lambda b, h, i: (row0 + b * nq + i, OFF_AGATE // GROUP_W + h)),
            pl.BlockSpec((1, HEAD_DIM), lambda b, h, i: (0, 0)),
            pl.BlockSpec((tq, HEAD_DIM), lambda b, h, i: (i, 0)),
            pl.BlockSpec((tq, HEAD_DIM), lambda b, h, i: (i, 0)),
            pl.BlockSpec((seq, HEAD_DIM), lambda b, h, i: (0, 0)),
            pl.BlockSpec((seq, HEAD_DIM), lambda b, h, i: (0, 0)),
        ],
        out_specs=pl.BlockSpec((tq, GROUP_W), lambda b, h, i: (b * nq + i, h)),
        scratch_shapes=[pltpu.VMEM((seq + past, HEAD_DIM), BF16), pltpu.VMEM((seq + past, HEAD_DIM), BF16)],
        compiler_params=_params("arbitrary", "arbitrary", "arbitrary"),
        name="attn_lat",
    )(proj, k, v, ck, cv, proj, q_norm.reshape(1, HEAD_DIM), cos2, sin2, cos2, sin2)


def _rope_tables(seq):
    rows = seq // GRID_W
    row = jnp.repeat(jnp.arange(rows, dtype=F32), GRID_W)
    col = jnp.tile(jnp.arange(GRID_W, dtype=F32), rows)
    inv = ROPE_THETA ** (-jnp.arange(ROPE_PAIRS_AXIS, dtype=F32) / ROPE_PAIRS_AXIS)
    ang = jnp.concatenate([row[:, None] * inv, col[:, None] * inv], axis=-1)
    cos, sin = jnp.cos(ang), jnp.sin(ang)
    return jnp.concatenate([cos, cos], axis=-1), jnp.concatenate([-sin, sin], axis=-1)


def _dft_angle(i, j, n):
    return (2.0 * math.pi / n) * ((i[:, None] * j[None, :]) % n).astype(F32)


def _cos_sin_table(rows, cols, n):
    c = jnp.arange(cols, dtype=jnp.int32)
    hi = _dft_angle(TABLE_SPLIT * jnp.arange(rows // TABLE_SPLIT, dtype=jnp.int32), c, n)
    lo = _dft_angle(jnp.arange(TABLE_SPLIT, dtype=jnp.int32), c, n)
    ch, sh = jnp.cos(hi)[:, None, :], jnp.sin(hi)[:, None, :]
    cl, sl = jnp.cos(lo)[None, :, :], jnp.sin(lo)[None, :, :]
    return (ch * cl - sh * sl).reshape(rows, cols), (sh * cl + ch * sl).reshape(rows, cols)


def _fnet_kernel(u_ref, fg_ref, csc_ref, cos_ref, sin_ref, o_ref, tc_s, ts_s, *, scale):
    @pl.when(pl.program_id(1) == 0)
    def _():
        for g in range(FNET_GROUPS):
            sl = slice(g * FNET_GROUP_W, (g + 1) * FNET_GROUP_W)
            t = _dot(u_ref[:, sl], csc_ref[...])
            tc_s[:, sl] = t[:, :FNET_GROUP_W].astype(BF16)
            ts_s[:, sl] = t[:, FNET_GROUP_W:].astype(BF16)

    y = (_dot(cos_ref[...], tc_s[...]) - _dot(sin_ref[...], ts_s[...])) * scale
    o_ref[...] = (y * _silu(fg_ref[...].astype(F32))).astype(BF16)


def _fnet_tables(seq):
    j = jnp.arange(FNET_GROUP_W, dtype=jnp.int32)
    ang = _dft_angle(j, j, FNET_GROUP_W)
    csc = jnp.concatenate([jnp.cos(ang), jnp.sin(ang)], axis=1).astype(BF16)
    cos, sin = _cos_sin_table(seq, seq, seq)
    return csc, cos.astype(BF16), sin.astype(BF16)


def _fnet(proj, tables, row_start, n_seq, seq, tr=256):
    csc, cos, sin = tables
    nr = seq // tr
    s0, r0 = row_start // seq, row_start // tr
    return pl.pallas_call(
        functools.partial(_fnet_kernel, scale=1.0 / math.sqrt(seq * FNET_GROUP_W)),
        out_shape=jax.ShapeDtypeStruct((n_seq * seq, FNET_W), BF16),
        grid=(n_seq, nr),
        in_specs=[
            pl.BlockSpec((seq, FNET_W), lambda b, r: (s0 + b, OFF_FIN // FNET_W)),
            pl.BlockSpec((tr, FNET_W), lambda b, r: (r0 + b * nr + r, OFF_FGATE // FNET_W)),
            pl.BlockSpec((FNET_GROUP_W, 2 * FNET_GROUP_W), lambda b, r: (0, 0)),
            pl.BlockSpec((tr, seq), lambda b, r: (r, 0)),
            pl.BlockSpec((tr, seq), lambda b, r: (r, 0)),
        ],
        out_specs=pl.BlockSpec((tr, FNET_W), lambda b, r: (b * nr + r, 0)),
        scratch_shapes=[pltpu.VMEM((seq, FNET_W), BF16), pltpu.VMEM((seq, FNET_W), BF16)],
        compiler_params=_params("arbitrary", "arbitrary"),
        name=f"fnet_{seq}",
    )(proj, proj, csc, cos, sin)


def _short_kernel(x_ref, w_ref, b_ref, o_ref, *, seq):
    x = x_ref[...].astype(F32)
    n = x.shape[0]
    pos = lax.broadcasted_iota(jnp.int32, x.shape, 0) % seq
    prev = jnp.where(pos == 0, 0.0, pltpu.roll(x, 1, 0))
    nxt = jnp.where(pos == seq - 1, 0.0, pltpu.roll(x, n - 1, 0))
    y = b_ref[...] + prev * w_ref[0:1, :] + x * w_ref[1:2, :] + nxt * w_ref[2:3, :]
    o_ref[...] = y.astype(BF16)


def _short_conv(proj, w, b, row_start, n_rows, seq, tr=2048, tw=512):
    width = 3 * HYENA_W
    r0 = row_start // tr
    return pl.pallas_call(
        functools.partial(_short_kernel, seq=seq),
        out_shape=jax.ShapeDtypeStruct((n_rows, width), BF16),
        grid=(n_rows // tr, width // tw),
        in_specs=[
            pl.BlockSpec((tr, tw), lambda s, c: (r0 + s, OFF_HV // tw + c)),
            pl.BlockSpec((HYENA_SHORT, tw), lambda s, c: (0, c)),
            pl.BlockSpec((1, tw), lambda s, c: (0, c)),
        ],
        out_specs=pl.BlockSpec((tr, tw), lambda s, c: (s, c)),
        compiler_params=_params("arbitrary", "arbitrary"),
        name=f"short_conv_{seq}",
    )(proj, w, b.reshape(1, width))


def _filt_mlp_kernel(feats_ref, w1_ref, b1_ref, w2_ref, b2_ref, fr_ref, o_ref):
    fr = fr_ref[...]
    h = jnp.sin(fr * (_dot_exact(feats_ref[...], w1_ref[...]) + b1_ref[...]))
    o_ref[...] = jnp.sin(fr * (_dot_exact(h, w2_ref[...]) + b2_ref[...]))


def _filt_kernel(h_ref, t_ref, w3f_ref, w3b_ref, b3f_ref, b3b_ref, dl_ref, sum_ref, dif_ref, nyq_ref):
    h = h_ref[...]
    decay = jnp.exp(-t_ref[...] * dl_ref[...])
    hf = (_dot_exact(h, w3f_ref[...]) + b3f_ref[...]) * decay
    hb = (_dot_exact(h, w3b_ref[...]) + b3b_ref[...]) * decay
    den = jnp.sum(jnp.abs(hf), axis=0, keepdims=True) + jnp.sum(jnp.abs(hb), axis=0, keepdims=True) + EPS
    hs = (hf + hb) / den
    sum_ref[...] = hs.astype(BF16)
    dif_ref[...] = ((hf - hb) / den).astype(BF16)
    sign = (1 - 2 * (lax.broadcasted_iota(jnp.int32, hs.shape, 0) % 2)).astype(F32)
    nyq_ref[...] = jnp.sum(hs * sign, axis=0, keepdims=True)


def _hyena_filters(seq, w1, b1, w2, b2, w3, b3, freq, tw=512):
    t = jnp.arange(seq, dtype=F32)[:, None] / seq
    bands = jnp.arange(1, HYENA_BANDS + 1, dtype=F32)[None, :]
    feats = jnp.concatenate([t, jnp.cos(2 * math.pi * t * bands), jnp.sin(2 * math.pi * t * bands),
                             jnp.zeros((seq, HYENA_POS_PAD - HYENA_POS_DIM), F32)], axis=-1)
    w1p = jnp.concatenate([w1, jnp.zeros((HYENA_POS_PAD - HYENA_POS_DIM, HYENA_FFN_W), F32)], axis=0)
    deltas = jnp.abs(jnp.linspace(HYENA_MIN_DECAY, HYENA_MAX_DECAY, HYENA_W, dtype=F32))[None, :]
    hidden = pl.pallas_call(
        _filt_mlp_kernel,
        out_shape=jax.ShapeDtypeStruct((seq, HYENA_FFN_W), F32),
        name=f"hyena_filter_mlp_{seq}",
    )(feats, w1p, b1.reshape(1, -1), w2, b2.reshape(1, -1), freq.reshape(1, -1))
    nw = HYENA_W // tw
    small = lambda shape: pl.BlockSpec(shape, lambda o, c: (0, 0))
    out_sds = jax.ShapeDtypeStruct((seq, HYENA_ORDER * HYENA_W), BF16)
    fwd = lambda o, c: (0, (2 * o) * nw + c)
    bwd = lambda o, c: (0, (2 * o + 1) * nw + c)
    b3r = b3.reshape(1, -1)
    return pl.pallas_call(
        _filt_kernel,
        out_shape=(out_sds, out_sds, jax.ShapeDtypeStruct((1, HYENA_ORDER * HYENA_W), F32)),
        grid=(HYENA_ORDER, nw),
        in_specs=[
            small((seq, HYENA_FFN_W)), small((seq, 1)),
            pl.BlockSpec((HYENA_FFN_W, tw), fwd), pl.BlockSpec((HYENA_FFN_W, tw), bwd),
            pl.BlockSpec((1, tw), fwd), pl.BlockSpec((1, tw), bwd),
            pl.BlockSpec((1, tw), lambda o, c: (0, c)),
        ],
        out_specs=(pl.BlockSpec((seq, tw), lambda o, c: (0, o * nw + c)),
                   pl.BlockSpec((seq, tw), lambda o, c: (0, o * nw + c)),
                   pl.BlockSpec((1, tw), lambda o, c: (0, o * nw + c))),
        compiler_params=_params("arbitrary", "arbitrary"),
        name=f"hyena_filters_{seq}",
    )(hidden, t, w3, w3, b3r, b3r, deltas)


def _conv_tables(seq):
    i = jnp.arange(seq, dtype=jnp.int32)
    nyq = (1 - 2 * (i % 2)).astype(F32)
    cos, sin = _cos_sin_table(seq, seq, 2 * seq)
    sin_f = jnp.where(i[:, None] == 0, nyq[None, :], sin)
    sin_t = jnp.where(i[None, :] == 0, nyq[:, None], sin)
    return cos.astype(BF16), sin_f.astype(BF16), sin_t.astype(BF16)


def _filter_bins(uc_ref, us_ref, nyq_ref, first_chunk):
    gr = uc_ref[...]
    packed = jnp.logical_and(lax.broadcasted_iota(jnp.int32, gr.shape, 0) == 0, first_chunk)
    gi = jnp.where(packed, 0.0, us_ref[...])
    gn = jnp.where(packed, nyq_ref[...], gr)
    wgt = jnp.where(packed, 1.0, 2.0)
    return gr * wgt, gi * wgt, gn * wgt


def _spectrum_product(cos_ref, sin_ref, z, bins):
    gr, gi, gn = bins
    zr = _dot(cos_ref[...], z)
    zi = _dot(sin_ref[...], z)
    return (zr * gr - zi * gi).astype(BF16), (zr * gi + zi * gn).astype(BF16)


def _conv_epilogue(y, z, bias_ref, xm, gate, n):
    y = (y * (1.0 / n) + z.astype(F32) * bias_ref[...]) * xm.astype(F32)
    if gate is not None:
        y = y * _silu(gate.astype(F32))
    return y.astype(BF16)


def _lconv_short_kernel(*refs, seq, n_seq_blk, gated):
    refs = list(refs)
    z_ref, xm_ref = refs[:2]
    gate_ref = refs[2] if gated else None
    bias_ref, cos_ref, sin_ref, sin_t_ref, uc_ref, us_ref, nyq_ref, o_ref = refs[2 + gated:]
    bins = _filter_bins(uc_ref, us_ref, nyq_ref, True)
    for s in range(n_seq_blk):
        rows = slice(s * seq, (s + 1) * seq)
        z = z_ref[rows, :]
        pr, pi = _spectrum_product(cos_ref, sin_ref, z, bins)
        y = _dot(cos_ref[...], pr) + _dot(sin_t_ref[...], pi)
        o_ref[rows, :] = _conv_epilogue(y, z, bias_ref, xm_ref[rows, :],
                                        gate_ref[rows, :] if gated else None, 2 * seq)


def _lconv_long_kernel(*refs, seq, chunk, gated):
    refs = list(refs)
    z_ref, xm_ref = refs[:2]
    gate_ref = refs[2] if gated else None
    bias_ref, cos_ref, sin_ref, sin_t_ref, uc_ref, us_ref, nyq_ref, o_ref, pr_s, pi_s = refs[2 + gated:]
    n_chunks = seq // chunk
    p = pl.program_id(2)

    @pl.when(p < n_chunks)
    def _():
        bins = _filter_bins(uc_ref, us_ref, nyq_ref, p == 0)
        pr, pi = _spectrum_product(cos_ref, sin_ref, z_ref[...], bins)
        rows = pl.ds(pl.multiple_of(p * chunk, chunk), chunk)
        pr_s[rows, :] = pr
        pi_s[rows, :] = pi

    @pl.when(p >= n_chunks)
    def _():
        y = _dot(cos_ref[...], pr_s[...]) + _dot(sin_t_ref[...], pi_s[...])
        rows = pl.ds(pl.multiple_of((p - n_chunks) * chunk, chunk), chunk)
        o_ref[...] = _conv_epilogue(y, z_ref[rows, :], bias_ref, xm_ref[...],
                                    gate_ref[...] if gated else None, 2 * seq)


def _long_conv(z, z_col, xm, xm_col, gate, gate_row0, bias, idx, order, tables, spectrum,
               n_seq, seq, tw, fc, seq_blk):
    nw, nf = HYENA_W // tw, seq // fc
    gated = gate is not None
    out_shape = jax.ShapeDtypeStruct((n_seq * seq, HYENA_W), BF16)
    bias_spec = pl.BlockSpec((None, 1, tw), lambda s, w, p: (idx, 0, w))
    nyq_spec = pl.BlockSpec((1, tw), lambda s, w, p: (0, order * nw + w))
    if nf == 1:
        tr = seq_blk * seq
        tok = lambda col, r0=0: pl.BlockSpec((tr, tw), lambda s, w, p: (r0 + s, col * nw + w))
        taps = pl.BlockSpec((seq, tw), lambda s, w, p: (0, order * nw + w))
        table = pl.BlockSpec((seq, seq), lambda s, w, p: (0, 0))
        in_specs = [tok(z_col), tok(xm_col)] + ([tok(OFF_HGATE // HYENA_W, gate_row0 // tr)] if gated else [])
        in_specs += [bias_spec, table, table, table, taps, taps, nyq_spec]
        kern = functools.partial(_lconv_short_kernel, seq=seq, n_seq_blk=seq_blk, gated=gated)
        grid, scratch = (n_seq // seq_blk, nw, 1), []
        out_spec = pl.BlockSpec((tr, tw), lambda s, w, p: (s, w))
    else:
        assert seq_blk == 1
        fwd_chunk = lambda p: jnp.minimum(p, nf - 1)
        out_chunk = lambda p: jnp.maximum(p - nf, 0)
        rows = lambda col, r0=0: pl.BlockSpec((fc, tw), lambda s, w, p: (r0 + s * nf + out_chunk(p), col * nw + w))
        taps = pl.BlockSpec((fc, tw), lambda s, w, p: (fwd_chunk(p), order * nw + w))
        in_specs = [pl.BlockSpec((seq, tw), lambda s, w, p: (s, z_col * nw + w)), rows(xm_col)]
        in_specs += [rows(OFF_HGATE // HYENA_W, gate_row0 // fc)] if gated else []
        in_specs += [
            bias_spec,
            pl.BlockSpec((fc, seq), lambda s, w, p: (p % nf, 0)),
            pl.BlockSpec((fc, seq), lambda s, w, p: (fwd_chunk(p), 0)),
            pl.BlockSpec((fc, seq), lambda s, w, p: (out_chunk(p), 0)),
            taps, taps, nyq_spec,
        ]
        kern = functools.partial(_lconv_long_kernel, seq=seq, chunk=fc, gated=gated)
        grid, scratch = (n_seq, nw, 2 * nf), [pltpu.VMEM((seq, tw), BF16), pltpu.VMEM((seq, tw), BF16)]
        out_spec = pl.BlockSpec((fc, tw), lambda s, w, p: (s * nf + out_chunk(p), w))
    args = [z, xm] + ([gate] if gated else []) + [bias, *tables, *spectrum]
    return pl.pallas_call(
        kern,
        out_shape=out_shape,
        grid=grid,
        in_specs=in_specs,
        out_specs=out_spec,
        scratch_shapes=scratch,
        compiler_params=_params("arbitrary", "arbitrary", "arbitrary"),
        name=f"long_conv_{seq}_{order}",
    )(*args)


def _merge_kernel(ac_ref, al_ref, fc_ref, fl_ref, hc_ref, hl_ref, wa_ref, wf_ref, wh_ref,
                  ga_ref, gf_ref, gh_ref, o_ref, wab_ref, wfb_ref, whb_ref, *, n_ctx_tiles):
    i = pl.program_id(1)

    @pl.when(i == 0)
    def _():
        wab_ref[...] = wa_ref[...].astype(BF16)
        wfb_ref[...] = wf_ref[...].astype(BF16)
        whb_ref[...] = wh_ref[...].astype(BF16)

    def body(a_ref, f_ref, h_ref):
        m = jax.nn.sigmoid(ga_ref[...].astype(F32)) * _dot(a_ref[...], wab_ref[...])
        m = m + jax.nn.sigmoid(gf_ref[...].astype(F32)) * _dot(f_ref[...], wfb_ref[...])
        m = m + jax.nn.sigmoid(gh_ref[...].astype(F32)) * _dot(h_ref[...], whb_ref[...])
        o_ref[...] = m.astype(BF16)

    _by_group(i, n_ctx_tiles, body, (ac_ref, fc_ref, hc_ref), (al_ref, fl_ref, hl_ref))


def _merge(attn, fnet, hy, w_a, w_f, w_h, layer, proj, n_ctx, tm=512, tn=1024):
    t = proj.shape[0]
    nct = n_ctx // tm
    act = lambda w: _pair_specs((tm, w), nct, inner=True)
    wgt = lambda k: pl.BlockSpec((None, k, tn), lambda j, i: (layer, 0, j), pipeline_mode=pl.Buffered(1))
    gate = lambda off: pl.BlockSpec((tm, tn), lambda j, i: (i, off // tn + j))
    return pl.pallas_call(
        functools.partial(_merge_kernel, n_ctx_tiles=nct),
        out_shape=jax.ShapeDtypeStruct((t, D_MODEL), BF16),
        grid=(D_MODEL // tn, t // tm),
        in_specs=act(ATTN_W) + act(FNET_W) + act(HYENA_W) + [wgt(ATTN_W), wgt(FNET_W), wgt(HYENA_W),
                                                             gate(OFF_GA), gate(OFF_GF), gate(OFF_GH)],
        out_specs=pl.BlockSpec((tm, tn), lambda j, i: (i, j)),
        scratch_shapes=[pltpu.VMEM((ATTN_W, tn), BF16), pltpu.VMEM((FNET_W, tn), BF16),
                        pltpu.VMEM((HYENA_W, tn), BF16)],
        compiler_params=_params("arbitrary", "arbitrary"),
        name="merge",
    )(*attn, *fnet, *hy, w_a, w_f, w_h, proj, proj, proj)


def kernel(x_prompt, x_sample, cache_k, cache_v, c, c_ctx, w_mod, b_mod, g_pre, w_in, q_norm, k_norm, hy_short_w, hy_short_b, hy_ffn_w1, hy_ffn_b1, hy_ffn_w2, hy_ffn_b2, hy_ffn_w3, hy_ffn_b3, hy_sin_freq, hy_bias, w_attn_o, w_fnet_o, w_hy_o, w_out, g_post):
    nb, seq, _ = x_prompt.shape
    db, dseq, _ = x_sample.shape
    assert 1 + db <= MOD_ROWS
    n_ctx, n_lat = nb * seq, db * dseq

    cond = jnp.concatenate([c_ctx[None, :], c, jnp.zeros((MOD_ROWS - 1 - db, D_MODEL), F32)], axis=0)
    mod = _modulation(cond, w_mod, b_mod)

    cos2, sin2 = _rope_tables(dseq)
    fnet_tables = {s: _fnet_tables(s) for s in (seq, dseq)}
    conv_tables = {s: _conv_tables(s) for s in (seq, dseq)}
    bias = hy_bias.reshape(DEPTH * HYENA_ORDER, 1, HYENA_W)
    conv_tiles = {seq: (HYENA_W, seq, 4), dseq: (HYENA_W, 512, 1)}
    proj_tm, proj_tn = 512, 1024
    skip_kv = lambda j: jnp.where(j >= W_OFF_K // proj_tn, j + W_KV // proj_tn, j)

    xs = (x_prompt.reshape(n_ctx, D_MODEL), 0, x_sample.reshape(n_lat, D_MODEL), 0)
    new_k, new_v = [], []
    mod3s = [mod[l].reshape(MOD_ROWS, 1, 3 * D_MODEL) for l in range(DEPTH)]
    h = _prenorm(*xs, g_pre[0], mod3s[0], n_ctx, n_lat, dseq)
    for l in range(DEPTH):
        mod3 = mod3s[l]
        proj = _matmul_wcast(h, w_in, l, BF16, proj_tm, proj_tn, "in_proj", n_out=PROJ_W, col_map=skip_kv)
        k, v = _kv_proj(h, w_in, l, k_norm[l])
        new_k.append(k)
        new_v.append(v)

        attn = (_attention_ctx(proj, k, v, q_norm[l], nb, seq),
                _attention_lat(proj, k, v, cache_k, cache_v, l, q_norm[l], cos2, sin2, n_ctx, db, dseq))

        fnet, hy = [], []
        for row_start, n_seq, s in ((0, nb, seq), (n_ctx, db, dseq)):
            fnet.append(_fnet(proj, fnet_tables[s], row_start, n_seq, s))
            u = _short_conv(proj, hy_short_w[l], hy_short_b[l], row_start, n_seq * s, s)
            tap_sum, tap_dif, tap_nyq = _hyena_filters(s, hy_ffn_w1[l], hy_ffn_b1[l], hy_ffn_w2[l], hy_ffn_b2[l],
                                                       hy_ffn_w3[l], hy_ffn_b3[l], hy_sin_freq[l])
            cos, sin_f, _ = conv_tables[s]
            spec_tm = min(1024, s)
            spectrum = (_matmul(cos, tap_sum, F32, spec_tm, 1024, f"filter_spectrum_cos_{s}"),
                        _matmul(sin_f, tap_dif, F32, spec_tm, 1024, f"filter_spectrum_sin_{s}"), tap_nyq)
            tw, fc, blk = conv_tiles[s]
            z1 = _long_conv(u, 0, u, 1, None, 0, bias, l * HYENA_ORDER, 0, conv_tables[s],
                            spectrum, n_seq, s, tw, fc, blk)
            hy.append(_long_conv(z1, 0, u, 2, proj, row_start, bias, l * HYENA_ORDER + 1, 1, conv_tables[s],
                                 spectrum, n_seq, s, tw, fc, blk))

        merged = _merge(attn, fnet, hy, w_attn_o, w_fnet_o, w_hy_o, l, proj, n_ctx)
        r = _matmul_wcast(merged, w_out, l, BF16, proj_tm, proj_tn, "out_proj")
        if l < DEPTH - 1:
            x_new, h = _post(*xs, r, 0, g_post[l], mod3, n_ctx, n_lat, dseq, next_norm=(g_pre[l + 1], mod3s[l + 1]))
            xs = (x_new, 0, x_new, n_ctx)
        else:
            y_prompt = _post(*xs, r, 0, g_post[l], mod3, n_ctx, 0, dseq)
            y_sample = _post(*xs, r, n_ctx, g_post[l], mod3, 0, n_lat, dseq)

    cache_k_new, cache_v_new = _new_cache(new_k, new_v, nb, seq)
    return (y_prompt.reshape(nb, seq, D_MODEL), y_sample.reshape(db, dseq, D_MODEL), cache_k_new, cache_v_new)
```

```python
import functools
import math

import jax
import jax.numpy as jnp
from jax import lax
from jax.experimental import pallas as pl
from jax.experimental.pallas import tpu as pltpu

F32 = jnp.float32
BF16 = jnp.bfloat16

D_MODEL = 4096
DEPTH = 2
GRID_W = 64
HEAD_DIM = 128
N_HEADS = 16
N_KV_HEADS = 4
KV_GROUP = N_HEADS // N_KV_HEADS
ATTN_W = N_HEADS * HEAD_DIM
KV_W = N_KV_HEADS * HEAD_DIM
GROUP_W = KV_GROUP * HEAD_DIM
ATTN_SCALE = HEAD_DIM ** -0.5
Q_SCALE = ATTN_SCALE * math.log2(math.e)
ROPE_THETA = 10000.0
ROPE_PAIRS_AXIS = HEAD_DIM // 4
FNET_W = D_MODEL // 4
FNET_GROUPS = 4
FNET_GROUP_W = FNET_W // FNET_GROUPS
HYENA_W = D_MODEL // 4
HYENA_ORDER = 2
HYENA_SHORT = 3
HYENA_BANDS = 16
HYENA_POS_DIM = 1 + 2 * HYENA_BANDS
HYENA_POS_PAD = 128
HYENA_FFN_W = 64
HYENA_MIN_DECAY = math.log(1e-2) / 0.3
HYENA_MAX_DECAY = math.log(1e-2) / 1.5
EPS = 1e-6

W_OFF_K = ATTN_W
W_KV = 2 * KV_W
OFF_Q = 0
OFF_AGATE = OFF_Q + ATTN_W
OFF_FIN = OFF_AGATE + ATTN_W
OFF_FGATE = OFF_FIN + FNET_W
OFF_HV = OFF_FGATE + FNET_W
OFF_HX1 = OFF_HV + HYENA_W
OFF_HX2 = OFF_HX1 + HYENA_W
OFF_HGATE = OFF_HX2 + HYENA_W
OFF_GA = OFF_HGATE + HYENA_W
OFF_GF = OFF_GA + D_MODEL
OFF_GH = OFF_GF + D_MODEL
PROJ_W = OFF_GH + D_MODEL

MOD_ROWS = 8
ROW_TILE = 256
ATTN_ROWS = 256
TABLE_SPLIT = 64
VMEM_LIMIT = 56 * 1024 * 1024


def _params(*sem):
    return pltpu.CompilerParams(dimension_semantics=sem, vmem_limit_bytes=VMEM_LIMIT)


def _silu(x):
    return x * jax.nn.sigmoid(x)


def _dot(a, b):
    return jnp.dot(a, b, preferred_element_type=F32)


def _dot_exact(a, b):
    return jnp.dot(a, b, preferred_element_type=F32, precision=lax.Precision.HIGHEST)


def _pair_specs(block, n_ctx_tiles, ctx_tile0=0, lat_tile0=0, col=0, inner=False):
    def ctx(i):
        return (ctx_tile0 + jnp.clip(i, 0, max(n_ctx_tiles - 1, 0)), col)

    def lat(i):
        return (lat_tile0 + jnp.maximum(i - n_ctx_tiles, 0), col)

    if inner:
        return [pl.BlockSpec(block, lambda j, i: ctx(i)), pl.BlockSpec(block, lambda j, i: lat(i))]
    return [pl.BlockSpec(block, lambda i: ctx(i)), pl.BlockSpec(block, lambda i: lat(i))]


def _by_group(i, n_ctx_tiles, body, ctx_refs, lat_refs):
    @pl.when(i < n_ctx_tiles)
    def _():
        body(*ctx_refs)

    @pl.when(i >= n_ctx_tiles)
    def _():
        body(*lat_refs)


def _mod_kernel(c_ref, w_ref, b_ref, o_ref):
    s = _silu(c_ref[...]).astype(BF16)
    o_ref[0] = _dot(s, w_ref[0].astype(BF16)) + b_ref[0]


def _modulation(cond, w_mod, b_mod, tn=512):
    n = w_mod.shape[2]
    return pl.pallas_call(
        _mod_kernel,
        out_shape=jax.ShapeDtypeStruct((DEPTH, MOD_ROWS, n), F32),
        grid=(DEPTH, n // tn),
        in_specs=[
            pl.BlockSpec((MOD_ROWS, D_MODEL), lambda l, j: (0, 0)),
            pl.BlockSpec((1, D_MODEL, tn), lambda l, j: (l, 0, j)),
            pl.BlockSpec((1, 1, tn), lambda l, j: (l, 0, j)),
        ],
        out_specs=pl.BlockSpec((1, MOD_ROWS, tn), lambda l, j: (l, 0, j)),
        compiler_params=_params("arbitrary", "arbitrary"),
        name="modulation",
    )(cond, w_mod, b_mod.reshape(DEPTH, 1, n))


def _mod_spec(chunk, n_ctx_tiles, tiles_per_latent_seq):
    def row(i):
        return jnp.where(i < n_ctx_tiles, 0, 1 + (i - n_ctx_tiles) // tiles_per_latent_seq)
    return pl.BlockSpec((1, 1, D_MODEL), lambda i: (row(i), 0, chunk))


def _modulated_norm(x, g_ref, shift_ref, scale_ref):
    y = x * lax.rsqrt(jnp.mean(x * x, axis=-1, keepdims=True) + EPS) * g_ref[...]
    return (y * (1.0 + scale_ref[0]) + shift_ref[0]).astype(BF16)


def _prenorm_kernel(xc_ref, xl_ref, g_ref, shift_ref, scale_ref, o_ref, *, n_ctx_tiles):
    def body(x_ref):
        o_ref[...] = _modulated_norm(x_ref[...], g_ref, shift_ref, scale_ref)

    _by_group(pl.program_id(0), n_ctx_tiles, body, (xc_ref,), (xl_ref,))


def _prenorm(x_ctx, ctx_row0, x_lat, lat_row0, g, mod3, n_ctx, n_lat, lat_seq):
    tm = 2 * ROW_TILE
    nct = n_ctx // tm
    mod = functools.partial(_mod_spec, n_ctx_tiles=nct, tiles_per_latent_seq=lat_seq // tm)
    return pl.pallas_call(
        functools.partial(_prenorm_kernel, n_ctx_tiles=nct),
        out_shape=jax.ShapeDtypeStruct((n_ctx + n_lat, D_MODEL), BF16),
        grid=((n_ctx + n_lat) // tm,),
        in_specs=_pair_specs((tm, D_MODEL), nct, ctx_row0 // tm, lat_row0 // tm)
        + [pl.BlockSpec((1, D_MODEL), lambda i: (0, 0)), mod(0), mod(1)],
        out_specs=pl.BlockSpec((tm, D_MODEL), lambda i: (i, 0)),
        compiler_params=_params("arbitrary"),
        name="prenorm",
    )(x_ctx, x_lat, g.reshape(1, D_MODEL), mod3, mod3)


def _post_kernel(xc_ref, xl_ref, r_ref, g_ref, gate_ref, *rest, n_ctx_tiles, with_next):
    if with_next:
        g_next_ref, shift_ref, scale_ref, o_ref, h_ref = rest
    else:
        (o_ref,) = rest

    def body(x_ref):
        r = r_ref[...].astype(F32)
        y = r * lax.rsqrt(jnp.mean(r * r, axis=-1, keepdims=True) + EPS) * g_ref[...]
        x = x_ref[...] + gate_ref[0] * y
        o_ref[...] = x
        if with_next:
            h_ref[...] = _modulated_norm(x, g_next_ref, shift_ref, scale_ref)

    _by_group(pl.program_id(0), n_ctx_tiles, body, (xc_ref,), (xl_ref,))


def _post(x_ctx, ctx_row0, x_lat, lat_row0, r, r_row0, g, mod3, n_ctx, n_lat, lat_seq, next_norm=None):
    tm = ROW_TILE
    nct = n_ctx // tm
    r0 = r_row0 // tm
    rows = n_ctx + n_lat
    mod = functools.partial(_mod_spec, n_ctx_tiles=nct, tiles_per_latent_seq=lat_seq // tm)
    vec = pl.BlockSpec((1, D_MODEL), lambda i: (0, 0))
    tile = pl.BlockSpec((tm, D_MODEL), lambda i: (i, 0))
    in_specs = _pair_specs((tm, D_MODEL), nct, ctx_row0 // tm, lat_row0 // tm) + [
        pl.BlockSpec((tm, D_MODEL), lambda i: (r0 + i, 0)), vec, mod(2)]
    args = [x_ctx, x_lat, r, g.reshape(1, D_MODEL), mod3]
    out_shape, out_specs = jax.ShapeDtypeStruct((rows, D_MODEL), F32), tile
    if next_norm is not None:
        g_next, mod3_next = next_norm
        in_specs += [vec, mod(0), mod(1)]
        args += [g_next.reshape(1, D_MODEL), mod3_next, mod3_next]
        out_shape, out_specs = (out_shape, jax.ShapeDtypeStruct((rows, D_MODEL), BF16)), (tile, tile)
    return pl.pallas_call(
        functools.partial(_post_kernel, n_ctx_tiles=nct, with_next=next_norm is not None),
        out_shape=out_shape,
        grid=(rows // tm,),
        in_specs=in_specs,
        out_specs=out_specs,
        compiler_params=_params("arbitrary"),
        name="post",
    )(*args)


def _mm_kernel(a_ref, b_ref, o_ref):
    o_ref[...] = _dot(a_ref[...], b_ref[...]).astype(o_ref.dtype)


def _matmul(a, b, out_dtype, tm, tn, name):
    m, k = a.shape
    n = b.shape[1]
    return pl.pallas_call(
        _mm_kernel,
        out_shape=jax.ShapeDtypeStruct((m, n), out_dtype),
        grid=(m // tm, n // tn),
        in_specs=[pl.BlockSpec((tm, k), lambda i, j: (i, 0)), pl.BlockSpec((k, tn), lambda i, j: (0, j))],
        out_specs=pl.BlockSpec((tm, tn), lambda i, j: (i, j)),
        compiler_params=_params("arbitrary", "arbitrary"),
        name=name,
    )(a, b)


def _mm_wcast_kernel(a_ref, w_ref, o_ref):
    o_ref[...] = lax.dot_general(a_ref[...], w_ref[...], (((1,), (0,)), ((), ())),
                                 preferred_element_type=F32).astype(o_ref.dtype)


def _matmul_wcast(a, w, layer, out_dtype, tm, tn, name, n_out=None, col_map=None):
    m, k = a.shape
    n = n_out or w.shape[-1]
    col = col_map or (lambda j: j)
    return pl.pallas_call(
        _mm_wcast_kernel,
        out_shape=jax.ShapeDtypeStruct((m, n), out_dtype),
        grid=(n // tn, m // tm),
        in_specs=[
            pl.BlockSpec((tm, k), lambda j, i: (i, 0)),
            pl.BlockSpec((None, k, tn), lambda j, i: (layer, 0, col(j))),
        ],
        out_specs=pl.BlockSpec((tm, tn), lambda j, i: (i, j)),
        compiler_params=_params("arbitrary", "arbitrary"),
        name=name,
    )(a, w)


def _kv_kernel(h_ref, w_ref, kn_ref, k_ref, v_ref, wb_ref):
    @pl.when(pl.program_id(0) == 0)
    def _():
        wb_ref[...] = w_ref[...].astype(BF16)

    acc = _dot(h_ref[...], wb_ref[...])
    for hd in range(N_KV_HEADS):
        sl = slice(hd * HEAD_DIM, (hd + 1) * HEAD_DIM)
        k_ref[:, sl] = _norm_head(acc[:, sl], kn_ref[...])
    v_ref[...] = acc[:, KV_W:]


def _kv_proj(h, w_in, layer, k_norm, tm=512):
    t = h.shape[0]
    return pl.pallas_call(
        _kv_kernel,
        out_shape=(jax.ShapeDtypeStruct((t, KV_W), F32), jax.ShapeDtypeStruct((t, KV_W), F32)),
        grid=(t // tm,),
        in_specs=[
            pl.BlockSpec((tm, D_MODEL), lambda i: (i, 0)),
            pl.BlockSpec((None, D_MODEL, W_KV), lambda i: (layer, 0, W_OFF_K // W_KV), pipeline_mode=pl.Buffered(1)),
            pl.BlockSpec((1, HEAD_DIM), lambda i: (0, 0)),
        ],
        out_specs=(pl.BlockSpec((tm, KV_W), lambda i: (i, 0)), pl.BlockSpec((tm, KV_W), lambda i: (i, 0))),
        scratch_shapes=[pltpu.VMEM((D_MODEL, W_KV), BF16)],
        compiler_params=_params("arbitrary"),
        name="kv_proj",
    )(h, w_in, k_norm.reshape(1, HEAD_DIM))


def _cache_kernel(*refs):
    k_refs, v_refs, (ko_ref, vo_ref) = refs[:DEPTH], refs[DEPTH:2 * DEPTH], refs[2 * DEPTH:]
    n_blk, seq = ko_ref.shape[:2]
    for l in range(DEPTH):
        @pl.when(pl.program_id(0) == l)
        def _():
            for s in range(n_blk):
                rows = slice(s * seq, (s + 1) * seq)
                for hd in range(N_KV_HEADS):
                    sl = slice(hd * HEAD_DIM, (hd + 1) * HEAD_DIM)
                    ko_ref[s, :, hd, :] = k_refs[l][rows, sl]
                    vo_ref[s, :, hd, :] = v_refs[l][rows, sl]


def _new_cache(ks, vs, n_seq, seq, seq_blk=4):
    def layer_spec(l):
        return pl.BlockSpec((seq_blk * seq, KV_W), lambda cur, b: (jnp.where(cur == l, b, 0), 0))

    out_sds = jax.ShapeDtypeStruct((n_seq, DEPTH, seq, N_KV_HEADS, HEAD_DIM), F32)
    out_spec = pl.BlockSpec((seq_blk, None, seq, N_KV_HEADS, HEAD_DIM), lambda cur, b: (b, cur, 0, 0, 0))
    return pl.pallas_call(
        _cache_kernel,
        out_shape=(out_sds, out_sds),
        grid=(DEPTH, n_seq // seq_blk),
        in_specs=[layer_spec(l) for l in range(DEPTH)] * 2,
        out_specs=(out_spec, out_spec),
        compiler_params=_params("arbitrary", "arbitrary"),
        name="new_cache",
    )(*ks, *vs)


def _softmax_pv(s, v):
    m = jnp.max(s, axis=-1, keepdims=True)
    p = jnp.exp2(s - m)
    l = jnp.sum(p, axis=-1, keepdims=True)
    return _dot(p.astype(BF16), v) / l


def _qk(q, k):
    return lax.dot_general(q, k, (((1,), (1,)), ((), ())), preferred_element_type=F32)


def _norm_head(q, g):
    return q * lax.rsqrt(jnp.mean(q * q, axis=-1, keepdims=True) + EPS) * g


def _rope(x, cos2, sin2):
    return x * cos2 + pltpu.roll(x, HEAD_DIM // 2, 1) * sin2


def _attn_ctx_kernel(q_ref, k_ref, v_ref, ag_ref, qn_ref, o_ref):
    seq = q_ref.shape[0]
    for hd in range(N_KV_HEADS):
        kv = slice(hd * HEAD_DIM, (hd + 1) * HEAD_DIM)
        cols = [slice((hd * KV_GROUP + g) * HEAD_DIM, (hd * KV_GROUP + g + 1) * HEAD_DIM) for g in range(KV_GROUP)]
        q = jnp.concatenate([(_norm_head(q_ref[:, c].astype(F32), qn_ref[...]) * Q_SCALE).astype(BF16)
                             for c in cols], axis=0)
        o = _softmax_pv(_qk(q, k_ref[:, kv].astype(BF16)), v_ref[:, kv].astype(BF16))
        for g, c in enumerate(cols):
            o_ref[:, c] = (o[g * seq:(g + 1) * seq] * _silu(ag_ref[:, c].astype(F32))).astype(BF16)


def _attention_ctx(proj, k, v, q_norm, n_seq, seq):
    return pl.pallas_call(
        _attn_ctx_kernel,
        out_shape=jax.ShapeDtypeStruct((n_seq * seq, ATTN_W), BF16),
        grid=(n_seq,),
        in_specs=[
            pl.BlockSpec((seq, ATTN_W), lambda b: (b, OFF_Q // ATTN_W)),
            pl.BlockSpec((seq, KV_W), lambda b: (b, 0)),
            pl.BlockSpec((seq, KV_W), lambda b: (b, 0)),
            pl.BlockSpec((seq, ATTN_W), lambda b: (b, OFF_AGATE // ATTN_W)),
            pl.BlockSpec((1, HEAD_DIM), lambda b: (0, 0)),
        ],
        out_specs=pl.BlockSpec((seq, ATTN_W), lambda b: (b, 0)),
        compiler_params=_params("arbitrary"),
        name="attn_ctx",
    )(proj, k, v, proj, q_norm.reshape(1, HEAD_DIM))


def _attn_lat_kernel(q_ref, k_ref, v_ref, ck_ref, cv_ref, ag_ref, qn_ref, cosq_ref, sinq_ref,
                     cosk_ref, sink_ref, o_ref, k_s, v_s, *, seq):
    @pl.when(pl.program_id(2) == 0)
    def _():
        k_s[:seq] = _rope(k_ref[...], cosk_ref[...], sink_ref[...]).astype(BF16)
        k_s[seq:] = ck_ref[...].astype(BF16)
        v_s[:seq] = v_ref[...].astype(BF16)
        v_s[seq:] = cv_ref[...].astype(BF16)

    k = k_s[...]
    v = v_s[...]
    for g in range(KV_GROUP):
        sl = slice(g * HEAD_DIM, (g + 1) * HEAD_DIM)
        for r0 in range(0, q_ref.shape[0], ATTN_ROWS):
            rows = slice(r0, r0 + ATTN_ROWS)
            q = _norm_head(q_ref[rows, sl].astype(F32), qn_ref[...])
            q = _rope(q, cosq_ref[rows, :], sinq_ref[rows, :]) * Q_SCALE
            o = _softmax_pv(_qk(q.astype(BF16), k), v)
            o_ref[rows, sl] = (o * _silu(ag_ref[rows, sl].astype(F32))).astype(BF16)


def _attention_lat(proj, k, v, cache_k, cache_v, layer, q_norm, cos2, sin2, n_ctx, n_seq, seq, tq=1024):
    past = cache_k.shape[2]
    ck = cache_k.reshape(cache_k.shape[0], DEPTH, past, KV_W)
    cv = cache_v.reshape(cache_v.shape[0], DEPTH, past, KV_W)
    nq = seq // tq
    row0, seq0 = n_ctx // tq, n_ctx // seq
    return pl.pallas_call(
        functools.partial(_attn_lat_kernel, seq=seq),
        out_shape=jax.ShapeDtypeStruct((n_seq * seq, ATTN_W), BF16),
        grid=(n_seq, N_KV_HEADS, nq),
        in_specs=[
            pl.BlockSpec((tq, GROUP_W), lambda b, h, i: (row0 + b * nq + i, OFF_Q // GROUP_W + h)),
            pl.BlockSpec((seq, HEAD_DIM), lambda b, h, i: (seq0 + b, h)),
            pl.BlockSpec((seq, HEAD_DIM), lambda b, h, i: (seq0 + b, h)),
            pl.BlockSpec((None, None, past, HEAD_DIM), lambda b, h, i: (b, layer, 0, h)),
            pl.BlockSpec((None, None, past, HEAD_DIM), lambda b, h, i: (b, layer, 0, h)),
            pl.BlockSpec((tq, GROUP_W), lambda b, h, i: (row0 + b * nq + i, OFF_AGATE // GROUP_W + h)),
            pl.BlockSpec((1, HEAD_DIM), lambda b, h, i: (0, 0)),
            pl.BlockSpec((tq, HEAD_DIM), lambda b, h, i: (i, 0)),
            pl.BlockSpec((tq, HEAD_DIM), lambda b, h, i: (i, 0)),
            pl.BlockSpec((seq, HEAD_DIM), lambda b, h, i: (0, 0)),
            pl.BlockSpec((seq, HEAD_DIM), lambda b, h, i: (0, 0)),
        ],
        out_specs=pl.BlockSpec((tq, GROUP_W), lambda b, h, i: (b * nq + i, h)),
        scratch_shapes=[pltpu.VMEM((seq + past, HEAD_DIM), BF16), pltpu.VMEM((seq + past, HEAD_DIM), BF16)],
        compiler_params=_params("arbitrary", "arbitrary", "arbitrary"),
        name="attn_lat",
    )(proj, k, v, ck, cv, proj, q_norm.reshape(1, HEAD_DIM), cos2, sin2, cos2, sin2)


def _rope_tables(seq):
    rows = seq // GRID_W
    row = jnp.repeat(jnp.arange(rows, dtype=F32), GRID_W)
    col = jnp.tile(jnp.arange(GRID_W, dtype=F32), rows)
    inv = ROPE_THETA ** (-jnp.arange(ROPE_PAIRS_AXIS, dtype=F32) / ROPE_PAIRS_AXIS)
    ang = jnp.concatenate([row[:, None] * inv, col[:, None] * inv], axis=-1)
    cos, sin = jnp.cos(ang), jnp.sin(ang)
    return jnp.concatenate([cos, cos], axis=-1), jnp.concatenate([-sin, sin], axis=-1)


def _dft_angle(i, j, n):
    return (2.0 * math.pi / n) * ((i[:, None] * j[None, :]) % n).astype(F32)


def _cos_sin_table(rows, cols, n):
    c = jnp.arange(cols, dtype=jnp.int32)
    hi = _dft_angle(TABLE_SPLIT * jnp.arange(rows // TABLE_SPLIT, dtype=jnp.int32), c, n)
    lo = _dft_angle(jnp.arange(TABLE_SPLIT, dtype=jnp.int32), c, n)
    ch, sh = jnp.cos(hi)[:, None, :], jnp.sin(hi)[:, None, :]
    cl, sl = jnp.cos(lo)[None, :, :], jnp.sin(lo)[None, :, :]
    return (ch * cl - sh * sl).reshape(rows, cols), (sh * cl + ch * sl).reshape(rows, cols)


def _fnet_kernel(u_ref, fg_ref, csc_ref, cos_ref, sin_ref, o_ref, tc_s, ts_s, *, scale):
    @pl.when(pl.program_id(1) == 0)
    def _():
        for g in range(FNET_GROUPS):
            sl = slice(g * FNET_GROUP_W, (g + 1) * FNET_GROUP_W)
            t = _dot(u_ref[:, sl], csc_ref[...])
            tc_s[:, sl] = t[:, :FNET_GROUP_W].astype(BF16)
            ts_s[:, sl] = t[:, FNET_GROUP_W:].astype(BF16)

    y = (_dot(cos_ref[...], tc_s[...]) - _dot(sin_ref[...], ts_s[...])) * scale
    o_ref[...] = (y * _silu(fg_ref[...].astype(F32))).astype(BF16)


def _fnet_tables(seq):
    j = jnp.arange(FNET_GROUP_W, dtype=jnp.int32)
    ang = _dft_angle(j, j, FNET_GROUP_W)
    csc = jnp.concatenate([jnp.cos(ang), jnp.sin(ang)], axis=1).astype(BF16)
    cos, sin = _cos_sin_table(seq, seq, seq)
    return csc, cos.astype(BF16), sin.astype(BF16)


def _fnet(proj, tables, row_start, n_seq, seq, tr=256):
    csc, cos, sin = tables
    nr = seq // tr
    s0, r0 = row_start // seq, row_start // tr
    return pl.pallas_call(
        functools.partial(_fnet_kernel, scale=1.0 / math.sqrt(seq * FNET_GROUP_W)),
        out_shape=jax.ShapeDtypeStruct((n_seq * seq, FNET_W), BF16),
        grid=(n_seq, nr),
        in_specs=[
            pl.BlockSpec((seq, FNET_W), lambda b, r: (s0 + b, OFF_FIN // FNET_W)),
            pl.BlockSpec((tr, FNET_W), lambda b, r: (r0 + b * nr + r, OFF_FGATE // FNET_W)),
            pl.BlockSpec((FNET_GROUP_W, 2 * FNET_GROUP_W), lambda b, r: (0, 0)),
            pl.BlockSpec((tr, seq), lambda b, r: (r, 0)),
            pl.BlockSpec((tr, seq), lambda b, r: (r, 0)),
        ],
        out_specs=pl.BlockSpec((tr, FNET_W), lambda b, r: (b * nr + r, 0)),
        scratch_shapes=[pltpu.VMEM((seq, FNET_W), BF16), pltpu.VMEM((seq, FNET_W), BF16)],
        compiler_params=_params("arbitrary", "arbitrary"),
        name=f"fnet_{seq}",
    )(proj, proj, csc, cos, sin)


def _short_kernel(x_ref, w_ref, b_ref, o_ref, *, seq):
    x = x_ref[...].astype(F32)
    n = x.shape[0]
    pos = lax.broadcasted_iota(jnp.int32, x.shape, 0) % seq
    prev = jnp.where(pos == 0, 0.0, pltpu.roll(x, 1, 0))
    nxt = jnp.where(pos == seq - 1, 0.0, pltpu.roll(x, n - 1, 0))
    y = b_ref[...] + prev * w_ref[0:1, :] + x * w_ref[1:2, :] + nxt * w_ref[2:3, :]
    o_ref[...] = y.astype(BF16)


def _short_conv(proj, w, b, row_start, n_rows, seq, tr=2048, tw=512):
    width = 3 * HYENA_W
    r0 = row_start // tr
    return pl.pallas_call(
        functools.partial(_short_kernel, seq=seq),
        out_shape=jax.ShapeDtypeStruct((n_rows, width), BF16),
        grid=(n_rows // tr, width // tw),
        in_specs=[
            pl.BlockSpec((tr, tw), lambda s, c: (r0 + s, OFF_HV // tw + c)),
            pl.BlockSpec((HYENA_SHORT, tw), lambda s, c: (0, c)),
            pl.BlockSpec((1, tw), lambda s, c: (0, c)),
        ],
        out_specs=pl.BlockSpec((tr, tw), lambda s, c: (s, c)),
        compiler_params=_params("arbitrary", "arbitrary"),
        name=f"short_conv_{seq}",
    )(proj, w, b.reshape(1, width))


def _filt_mlp_kernel(feats_ref, w1_ref, b1_ref, w2_ref, b2_ref, fr_ref, o_ref):
    fr = fr_ref[...]
    h = jnp.sin(fr * (_dot_exact(feats_ref[...], w1_ref[...]) + b1_ref[...]))
    o_ref[...] = jnp.sin(fr * (_dot_exact(h, w2_ref[...]) + b2_ref[...]))


def _filt_kernel(h_ref, t_ref, w3f_ref, w3b_ref, b3f_ref, b3b_ref, dl_ref, sum_ref, dif_ref, nyq_ref):
    h = h_ref[...]
    decay = jnp.exp(-t_ref[...] * dl_ref[...])
    hf = (_dot_exact(h, w3f_ref[...]) + b3f_ref[...]) * decay
    hb = (_dot_exact(h, w3b_ref[...]) + b3b_ref[...]) * decay
    den = jnp.sum(jnp.abs(hf), axis=0, keepdims=True) + jnp.sum(jnp.abs(hb), axis=0, keepdims=True) + EPS
    hs = (hf + hb) / den
    sum_ref[...] = hs.astype(BF16)
    dif_ref[...] = ((hf - hb) / den).astype(BF16)
    sign = (1 - 2 * (lax.broadcasted_iota(jnp.int32, hs.shape, 0) % 2)).astype(F32)
    nyq_ref[...] = jnp.sum(hs * sign, axis=0, keepdims=True)


def _hyena_filters(seq, w1, b1, w2, b2, w3, b3, freq, tw=512):
    t = jnp.arange(seq, dtype=F32)[:, None] / seq
    bands = jnp.arange(1, HYENA_BANDS + 1, dtype=F32)[None, :]
    feats = jnp.concatenate([t, jnp.cos(2 * math.pi * t * bands), jnp.sin(2 * math.pi * t * bands),
                             jnp.zeros((seq, HYENA_POS_PAD - HYENA_POS_DIM), F32)], axis=-1)
    w1p = jnp.concatenate([w1, jnp.zeros((HYENA_POS_PAD - HYENA_POS_DIM, HYENA_FFN_W), F32)], axis=0)
    deltas = jnp.abs(jnp.linspace(HYENA_MIN_DECAY, HYENA_MAX_DECAY, HYENA_W, dtype=F32))[None, :]
    hidden = pl.pallas_call(
        _filt_mlp_kernel,
        out_shape=jax.ShapeDtypeStruct((seq, HYENA_FFN_W), F32),
        name=f"hyena_filter_mlp_{seq}",
    )(feats, w1p, b1.reshape(1, -1), w2, b2.reshape(1, -1), freq.reshape(1, -1))
    nw = HYENA_W // tw
    small = lambda shape: pl.BlockSpec(shape, lambda o, c: (0, 0))
    out_sds = jax.ShapeDtypeStruct((seq, HYENA_ORDER * HYENA_W), BF16)
    fwd = lambda o, c: (0, (2 * o) * nw + c)
    bwd = lambda o, c: (0, (2 * o + 1) * nw + c)
    b3r = b3.reshape(1, -1)
    return pl.pallas_call(
        _filt_kernel,
        out_shape=(out_sds, out_sds, jax.ShapeDtypeStruct((1, HYENA_ORDER * HYENA_W), F32)),
        grid=(HYENA_ORDER, nw),
        in_specs=[
            small((seq, HYENA_FFN_W)), small((seq, 1)),
            pl.BlockSpec((HYENA_FFN_W, tw), fwd), pl.BlockSpec((HYENA_FFN_W, tw), bwd),
            pl.BlockSpec((1, tw), fwd), pl.BlockSpec((1, tw), bwd),
            pl.BlockSpec((1, tw), lambda o, c: (0, c)),
        ],
        out_specs=(pl.BlockSpec((seq, tw), lambda o, c: (0, o * nw + c)),
                   pl.BlockSpec((seq, tw), lambda o, c: (0, o * nw + c)),
                   pl.BlockSpec((1, tw), lambda o, c: (0, o * nw + c))),
        compiler_params=_params("arbitrary", "arbitrary"),
        name=f"hyena_filters_{seq}",
    )(hidden, t, w3, w3, b3r, b3r, deltas)


def _conv_tables(seq):
    i = jnp.arange(seq, dtype=jnp.int32)
    nyq = (1 - 2 * (i % 2)).astype(F32)
    cos, sin = _cos_sin_table(seq, seq, 2 * seq)
    sin_f = jnp.where(i[:, None] == 0, nyq[None, :], sin)
    sin_t = jnp.where(i[None, :] == 0, nyq[:, None], sin)
    return cos.astype(BF16), sin_f.astype(BF16), sin_t.astype(BF16)


def _filter_bins(uc_ref, us_ref, nyq_ref, first_chunk):
    gr = uc_ref[...]
    packed = jnp.logical_and(lax.broadcasted_iota(jnp.int32, gr.shape, 0) == 0, first_chunk)
    gi = jnp.where(packed, 0.0, us_ref[...])
    gn = jnp.where(packed, nyq_ref[...], gr)
    wgt = jnp.where(packed, 1.0, 2.0)
    return gr * wgt, gi * wgt, gn * wgt


def _spectrum_product(cos_ref, sin_ref, z, bins):
    gr, gi, gn = bins
    zr = _dot(cos_ref[...], z)
    zi = _dot(sin_ref[...], z)
    return (zr * gr - zi * gi).astype(BF16), (zr * gi + zi * gn).astype(BF16)


def _conv_epilogue(y, z, bias_ref, xm, gate, n):
    y = (y * (1.0 / n) + z.astype(F32) * bias_ref[...]) * xm.astype(F32)
    if gate is not None:
        y = y * _silu(gate.astype(F32))
    return y.astype(BF16)


def _lconv_short_kernel(*refs, seq, n_seq_blk, gated):
    refs = list(refs)
    z_ref, xm_ref = refs[:2]
    gate_ref = refs[2] if gated else None
    bias_ref, cos_ref, sin_ref, sin_t_ref, uc_ref, us_ref, nyq_ref, o_ref = refs[2 + gated:]
    bins = _filter_bins(uc_ref, us_ref, nyq_ref, True)
    for s in range(n_seq_blk):
        rows = slice(s * seq, (s + 1) * seq)
        z = z_ref[rows, :]
        pr, pi = _spectrum_product(cos_ref, sin_ref, z, bins)
        y = _dot(cos_ref[...], pr) + _dot(sin_t_ref[...], pi)
        o_ref[rows, :] = _conv_epilogue(y, z, bias_ref, xm_ref[rows, :],
                                        gate_ref[rows, :] if gated else None, 2 * seq)


def _lconv_long_kernel(*refs, seq, chunk, gated):
    refs = list(refs)
    z_ref, xm_ref = refs[:2]
    gate_ref = refs[2] if gated else None
    bias_ref, cos_ref, sin_ref, sin_t_ref, uc_ref, us_ref, nyq_ref, o_ref, pr_s, pi_s = refs[2 + gated:]
    n_chunks = seq // chunk
    p = pl.program_id(2)

    @pl.when(p < n_chunks)
    def _():
        bins = _filter_bins(uc_ref, us_ref, nyq_ref, p == 0)
        pr, pi = _spectrum_product(cos_ref, sin_ref, z_ref[...], bins)
        rows = pl.ds(pl.multiple_of(p * chunk, chunk), chunk)
        pr_s[rows, :] = pr
        pi_s[rows, :] = pi

    @pl.when(p >= n_chunks)
    def _():
        y = _dot(cos_ref[...], pr_s[...]) + _dot(sin_t_ref[...], pi_s[...])
        rows = pl.ds(pl.multiple_of((p - n_chunks) * chunk, chunk), chunk)
        o_ref[...] = _conv_epilogue(y, z_ref[rows, :], bias_ref, xm_ref[...],
                                    gate_ref[...] if gated else None, 2 * seq)


def _long_conv(z, z_col, xm, xm_col, gate, gate_row0, bias, idx, order, tables, spectrum,
               n_seq, seq, tw, fc, seq_blk):
    nw, nf = HYENA_W // tw, seq // fc
    gated = gate is not None
    out_shape = jax.ShapeDtypeStruct((n_seq * seq, HYENA_W), BF16)
    bias_spec = pl.BlockSpec((None, 1, tw), lambda s, w, p: (idx, 0, w))
    nyq_spec = pl.BlockSpec((1, tw), lambda s, w, p: (0, order * nw + w))
    if nf == 1:
        tr = seq_blk * seq
        tok = lambda col, r0=0: pl.BlockSpec((tr, tw), lambda s, w, p: (r0 + s, col * nw + w))
        taps = pl.BlockSpec((seq, tw), lambda s, w, p: (0, order * nw + w))
        table = pl.BlockSpec((seq, seq), lambda s, w, p: (0, 0))
        in_specs = [tok(z_col), tok(xm_col)] + ([tok(OFF_HGATE // HYENA_W, gate_row0 // tr)] if gated else [])
        in_specs += [bias_spec, table, table, table, taps, taps, nyq_spec]
        kern = functools.partial(_lconv_short_kernel, seq=seq, n_seq_blk=seq_blk, gated=gated)
        grid, scratch = (n_seq // seq_blk, nw, 1), []
        out_spec = pl.BlockSpec((tr, tw), lambda s, w, p: (s, w))
    else:
        assert seq_blk == 1
        fwd_chunk = lambda p: jnp.minimum(p, nf - 1)
        out_chunk = lambda p: jnp.maximum(p - nf, 0)
        rows = lambda col, r0=0: pl.BlockSpec((fc, tw), lambda s, w, p: (r0 + s * nf + out_chunk(p), col * nw + w))
        taps = pl.BlockSpec((fc, tw), lambda s, w, p: (fwd_chunk(p), order * nw + w))
        in_specs = [pl.BlockSpec((seq, tw), lambda s, w, p: (s, z_col * nw + w)), rows(xm_col)]
        in_specs += [rows(OFF_HGATE // HYENA_W, gate_row0 // fc)] if gated else []
        in_specs += [
            bias_spec,
            pl.BlockSpec((fc, seq), lambda s, w, p: (p % nf, 0)),
            pl.BlockSpec((fc, seq), lambda s, w, p: (fwd_chunk(p), 0)),
            pl.BlockSpec((fc, seq), lambda s, w, p: (out_chunk(p), 0)),
            taps, taps, nyq_spec,
        ]
        kern = functools.partial(_lconv_long_kernel, seq=seq, chunk=fc, gated=gated)
        grid, scratch = (n_seq, nw, 2 * nf), [pltpu.VMEM((seq, tw), BF16), pltpu.VMEM((seq, tw), BF16)]
        out_spec = pl.BlockSpec((fc, tw), lambda s, w, p: (s * nf + out_chunk(p), w))
    args = [z, xm] + ([gate] if gated else []) + [bias, *tables, *spectrum]
    return pl.pallas_call(
        kern,
        out_shape=out_shape,
        grid=grid,
        in_specs=in_specs,
        out_specs=out_spec,
        scratch_shapes=scratch,
        compiler_params=_params("arbitrary", "arbitrary", "arbitrary"),
        name=f"long_conv_{seq}_{order}",
    )(*args)


def _merge_kernel(ac_ref, al_ref, fc_ref, fl_ref, hc_ref, hl_ref, wa_ref, wf_ref, wh_ref,
                  ga_ref, gf_ref, gh_ref, o_ref, wab_ref, wfb_ref, whb_ref, *, n_ctx_tiles):
    i = pl.program_id(1)

    @pl.when(i == 0)
    def _():
        wab_ref[...] = wa_ref[...].astype(BF16)
        wfb_ref[...] = wf_ref[...].astype(BF16)
        whb_ref[...] = wh_ref[...].astype(BF16)

    def body(a_ref, f_ref, h_ref):
        m = jax.nn.sigmoid(ga_ref[...].astype(F32)) * _dot(a_ref[...], wab_ref[...])
        m = m + jax.nn.sigmoid(gf_ref[...].astype(F32)) * _dot(f_ref[...], wfb_ref[...])
        m = m + jax.nn.sigmoid(gh_ref[...].astype(F32)) * _dot(h_ref[...], whb_ref[...])
        o_ref[...] = m.astype(BF16)

    _by_group(i, n_ctx_tiles, body, (ac_ref, fc_ref, hc_ref), (al_ref, fl_ref, hl_ref))


def _merge(attn, fnet, hy, w_a, w_f, w_h, layer, proj, n_ctx, tm=512, tn=1024):
    t = proj.shape[0]
    nct = n_ctx // tm
    act = lambda w: _pair_specs((tm, w), nct, inner=True)
    wgt = lambda k: pl.BlockSpec((None, k, tn), lambda j, i: (layer, 0, j), pipeline_mode=pl.Buffered(1))
    gate = lambda off: pl.BlockSpec((tm, tn), lambda j, i: (i, off // tn + j))
    return pl.pallas_call(
        functools.partial(_merge_kernel, n_ctx_tiles=nct),
        out_shape=jax.ShapeDtypeStruct((t, D_MODEL), BF16),
        grid=(D_MODEL // tn, t // tm),
        in_specs=act(ATTN_W) + act(FNET_W) + act(HYENA_W) + [wgt(ATTN_W), wgt(FNET_W), wgt(HYENA_W),
                                                             gate(OFF_GA), gate(OFF_GF), gate(OFF_GH)],
        out_specs=pl.BlockSpec((tm, tn), lambda j, i: (i, j)),
        scratch_shapes=[pltpu.VMEM((ATTN_W, tn), BF16), pltpu.VMEM((FNET_W, tn), BF16),
                        pltpu.VMEM((HYENA_W, tn), BF16)],
        compiler_params=_params("arbitrary", "arbitrary"),
        name="merge",
    )(*attn, *fnet, *hy, w_a, w_f, w_h, proj, proj, proj)


def kernel(x_prompt, x_sample, cache_k, cache_v, c, c_ctx, w_mod, b_mod, g_pre, w_in, q_norm, k_norm, hy_short_w, hy_short_b, hy_ffn_w1, hy_ffn_b1, hy_ffn_w2, hy_ffn_b2, hy_ffn_w3, hy_ffn_b3, hy_sin_freq, hy_bias, w_attn_o, w_fnet_o, w_hy_o, w_out, g_post):
    nb, seq, _ = x_prompt.shape
    db, dseq, _ = x_sample.shape
    assert 1 + db <= MOD_ROWS
    n_ctx, n_lat = nb * seq, db * dseq

    cond = jnp.concatenate([c_ctx[None, :], c, jnp.zeros((MOD_ROWS - 1 - db, D_MODEL), F32)], axis=0)
    mod = _modulation(cond, w_mod, b_mod)

    cos2, sin2 = _rope_tables(dseq)
    fnet_tables = {s: _fnet_tables(s) for s in (seq, dseq)}
    conv_tables = {s: _conv_tables(s) for s in (seq, dseq)}
    bias = hy_bias.reshape(DEPTH * HYENA_ORDER, 1, HYENA_W)
    conv_tiles = {seq: (HYENA_W, seq, 4), dseq: (HYENA_W, 512, 1)}
    proj_tm, proj_tn = 512, 1024
    skip_kv = lambda j: jnp.where(j >= W_OFF_K // proj_tn, j + W_KV // proj_tn, j)

    xs = (x_prompt.reshape(n_ctx, D_MODEL), 0, x_sample.reshape(n_lat, D_MODEL), 0)
    new_k, new_v = [], []
    mod3s = [mod[l].reshape(MOD_ROWS, 1, 3 * D_MODEL) for l in range(DEPTH)]
    h = _prenorm(*xs, g_pre[0], mod3s[0], n_ctx, n_lat, dseq)
    for l in range(DEPTH):
        mod3 = mod3s[l]
        proj = _matmul_wcast(h, w_in, l, BF16, proj_tm, proj_tn, "in_proj", n_out=PROJ_W, col_map=skip_kv)
        k, v = _kv_proj(h, w_in, l, k_norm[l])
        new_k.append(k)
        new_v.append(v)

        attn = (_attention_ctx(proj, k, v, q_norm[l], nb, seq),
                _attention_lat(proj, k, v, cache_k, cache_v, l, q_norm[l], cos2, sin2, n_ctx, db, dseq))

        fnet, hy = [], []
        for row_start, n_seq, s in ((0, nb, seq), (n_ctx, db, dseq)):
            fnet.append(_fnet(proj, fnet_tables[s], row_start, n_seq, s))
            u = _short_conv(proj, hy_short_w[l], hy_short_b[l], row_start, n_seq * s, s)
            tap_sum, tap_dif, tap_nyq = _hyena_filters(s, hy_ffn_w1[l], hy_ffn_b1[l], hy_ffn_w2[l], hy_ffn_b2[l],
                                                       hy_ffn_w3[l], hy_ffn_b3[l], hy_sin_freq[l])
            cos, sin_f, _ = conv_tables[s]
            spec_tm = min(1024, s)
            spectrum = (_matmul(cos, tap_sum, F32, spec_tm, 1024, f"filter_spectrum_cos_{s}"),
                        _matmul(sin_f, tap_dif, F32, spec_tm, 1024, f"filter_spectrum_sin_{s}"), tap_nyq)
            tw, fc, blk = conv_tiles[s]
            z1 = _long_conv(u, 0, u, 1, None, 0, bias, l * HYENA_ORDER, 0, conv_tables[s],
                            spectrum, n_seq, s, tw, fc, blk)
            hy.append(_long_conv(z1, 0, u, 2, proj, row_start, bias, l * HYENA_ORDER + 1, 1, conv_tables[s],
                                 spectrum, n_seq, s, tw, fc, blk))

        merged = _merge(attn, fnet, hy, w_attn_o, w_fnet_o, w_hy_o, l, proj, n_ctx)
        r = _matmul_wcast(merged, w_out, l, BF16, proj_tm, proj_tn, "out_proj")
        if l < DEPTH - 1:
            x_new, h = _post(*xs, r, 0, g_post[l], mod3, n_ctx, n_lat, dseq, next_norm=(g_pre[l + 1], mod3s[l + 1]))
            xs = (x_new, 0, x_new, n_ctx)
        else:
            y_prompt = _post(*xs, r, 0, g_post[l], mod3, n_ctx, 0, dseq)
            y_sample = _post(*xs, r, n_ctx, g_post[l], mod3, 0, n_lat, dseq)

    cache_k_new, cache_v_new = _new_cache(new_k, new_v, nb, seq)
    return (y_prompt.reshape(nb, seq, D_MODEL), y_sample.reshape(db, dseq, D_MODEL), cache_k_new, cache_v_new)
```
